```python
import math
import jax
import jax.numpy as jnp
from jax import lax
import numpy as np

D_MODEL = 2048
BATCH = 2
SEQ = 8192
DEPTH = 1

N_META = 16
BLOCK = 128
PAD = BLOCK - N_META
WINDOW = 128
HQ = 16
HKV = 4
Q_PER_KV = HQ // HKV
HD = 64
D_INNER = 2048
SSD_HEADDIM = 64
SSD_HEADS = D_INNER // SSD_HEADDIM
SSD_GROUPS = 4
HEADS_PER_GROUP = SSD_HEADS // SSD_GROUPS
D_STATE = 128
CONV_W = 4
CONV_DIM = D_INNER + 2 * SSD_GROUPS * D_STATE
MOE_GROUPS = 8
EXPERTS_PER_GROUP = 8
N_EXPERTS = MOE_GROUPS * EXPERTS_PER_GROUP
TOP_K = 2
D_EXPERT = 512
MOE_BLOCK = 128
LN_EPS = 1e-5
RMS_EPS = 1e-5
NEG_INF = -1e30
ALPHA = (2.0 * DEPTH) ** 0.25
BETA = (8.0 * DEPTH) ** -0.25
IN_SIZES = (HQ * HD, HKV * HD, HKV * HD, D_INNER, CONV_DIM, SSD_HEADS, D_MODEL, D_MODEL)
IN_DIM = sum(IN_SIZES)
IN_SPLITS = tuple(int(s) for s in np.cumsum(IN_SIZES)[:-1])

kernel_name = 'hybrid_swa_ssd_hmoe_deepnorm'


def layer_norm(x, g, b):
    xf = x.astype(jnp.float32)
    mu = jnp.mean(xf, axis=-1, keepdims=True)
    var = jnp.mean(jnp.square(xf - mu), axis=-1, keepdims=True)
    y = (xf - mu) * lax.rsqrt(var + LN_EPS) * g.astype(jnp.float32) + b.astype(jnp.float32)
    return y.astype(x.dtype)


def pad_front(a):
    return jnp.pad(a, [(0, 0), (PAD, 0)] + [(0, 0)] * (a.ndim - 2))


def sliding_window_attention_with_sinks(q, k, v, sinks):
    b, lp = q.shape[0], q.shape[1]
    nb = lp // BLOCK
    qb = q.reshape(b, nb, BLOCK, HKV, Q_PER_KV, HD)
    kb = k.reshape(b, nb, BLOCK, HKV, HD)
    vb = v.reshape(b, nb, BLOCK, HKV, HD)

    def band(t):
        prev = jnp.concatenate([jnp.zeros_like(t[:, :1]), t[:, :-1]], axis=1)
        return jnp.concatenate([prev, t], axis=2)

    k_band, v_band = band(kb), band(vb)
    k_meta = k[:, PAD:PAD + N_META]
    v_meta = v[:, PAD:PAD + N_META]
    scale = HD ** -0.5
    s_band = jnp.einsum('bnqhgd,bnshd->bnhgqs', qb, k_band).astype(jnp.float32) * scale
    s_meta = jnp.einsum('bnqhgd,bmhd->bnhgqm', qb, k_meta).astype(jnp.float32) * scale
    blk = jnp.arange(nb)[:, None]
    q_pos = blk * BLOCK + jnp.arange(BLOCK)[None, :] - PAD
    k_pos = (blk - 1) * BLOCK + jnp.arange(2 * BLOCK)[None, :] - PAD
    dist = q_pos[:, :, None] - k_pos[:, None, :]
    band_ok = (k_pos[:, None, :] >= N_META) & (dist >= 0) & (dist < WINDOW)
    meta_ok = jnp.arange(N_META)[None, None, :] <= q_pos[:, :, None]
    s_band = jnp.where(band_ok[None, :, None, None], s_band, NEG_INF)
    s_meta = jnp.where(meta_ok[None, :, None, None], s_meta, NEG_INF)
    sink = jnp.broadcast_to(sinks.astype(jnp.float32).reshape(1, 1, HKV, Q_PER_KV, 1, 1),
                            s_band.shape[:-1] + (1,))
    probs = jax.nn.softmax(jnp.concatenate([s_band, s_meta, sink], axis=-1), axis=-1).astype(v.dtype)
    out = (jnp.einsum('bnhgqs,bnshd->bnqhgd', probs[..., :2 * BLOCK], v_band)
           + jnp.einsum('bnhgqm,bmhd->bnqhgd', probs[..., 2 * BLOCK:2 * BLOCK + N_META], v_meta))
    return out.reshape(b, lp, HQ * HD)


def causal_depthwise_conv(x, w, bias):
    y = lax.conv_general_dilated(x, w[:, None, :], window_strides=(1,), padding=[(CONV_W - 1, 0)],
                                 dimension_numbers=('NWC', 'WIO', 'NWC'),
                                 feature_group_count=x.shape[-1])
    return y + bias


def ssd_chunked(xh, dt, a, bm, cm):
    b, lp = xh.shape[0], xh.shape[1]
    nc = lp // BLOCK
    xdt = (xh * dt[..., None]).reshape(b, nc, BLOCK, SSD_GROUPS, HEADS_PER_GROUP, SSD_HEADDIM)
    adt = (dt * a).reshape(b, nc, BLOCK, SSD_GROUPS, HEADS_PER_GROUP).transpose(0, 3, 4, 1, 2)
    a_cs = jnp.cumsum(adt, axis=-1)
    bc = bm.reshape(b, nc, BLOCK, SSD_GROUPS, D_STATE)
    cc = cm.reshape(b, nc, BLOCK, SSD_GROUPS, D_STATE)
    causal = jnp.tril(jnp.ones((BLOCK, BLOCK), dtype=bool))
    seg = a_cs[..., :, None] - a_cs[..., None, :]
    decay_ls = jnp.exp(jnp.where(causal, seg, -jnp.inf))
    cb = jnp.einsum('bclgn,bcsgn->bcgls', cc, bc)
    y_diag = jnp.einsum('bcgls,bgrcls,bcsgrp->bclgrp', cb, decay_ls, xdt)
    decay_to_end = jnp.exp(a_cs[..., -1:] - a_cs)
    states = jnp.einsum('bclgn,bgrcl,bclgrp->bcgrpn', bc, decay_to_end, xdt)
    chunk_decay = jnp.exp(a_cs[..., -1])

    def step(h, inp):
        s_c, d_c = inp
        return h * d_c[..., None, None] + s_c, h

    h0 = jnp.zeros((b, SSD_GROUPS, HEADS_PER_GROUP, SSD_HEADDIM, D_STATE), xh.dtype)
    _, prev = lax.scan(step, h0, (jnp.moveaxis(states, 1, 0), jnp.moveaxis(chunk_decay, -1, 0)))
    prev = jnp.moveaxis(prev, 0, 1)
    y_off = jnp.einsum('bclgn,bcgrpn,bgrcl->bclgrp', cc, prev, jnp.exp(a_cs))
    return (y_diag + y_off).reshape(b, lp, SSD_HEADS, SSD_HEADDIM)


def gated_rmsnorm(y, z, g):
    b, l, _ = y.shape
    yz = (y * jax.nn.silu(z.astype(jnp.float32))).reshape(b, l, SSD_GROUPS, D_INNER // SSD_GROUPS)
    yz = yz * lax.rsqrt(jnp.mean(jnp.square(yz), axis=-1, keepdims=True) + RMS_EPS)
    return yz.reshape(b, l, D_INNER) * g.astype(jnp.float32)


def hybrid_mixer(u, w_in, conv_w, conv_b, dt_bias, a_log, d_skip, ssd_norm_g, sinks,
                 w_br_attn, w_br_ssd, w_o):
    b, l, _ = u.shape
    q, k, v, z, xbc, dt_raw, g_attn, g_ssd = jnp.split(u @ w_in, IN_SPLITS, axis=-1)
    q = pad_front(q.reshape(b, l, HKV, Q_PER_KV, HD))
    k = pad_front(k.reshape(b, l, HKV, HD))
    v = pad_front(v.reshape(b, l, HKV, HD))
    attn = sliding_window_attention_with_sinks(q, k, v, sinks)[:, PAD:]
    xbc = jax.nn.silu(causal_depthwise_conv(xbc, conv_w, conv_b))
    xs, bs, cs = jnp.split(xbc, [D_INNER, D_INNER + SSD_GROUPS * D_STATE], axis=-1)
    dt = jax.nn.softplus(dt_raw.astype(jnp.float32) + dt_bias.astype(jnp.float32))
    a = -jnp.exp(a_log.astype(jnp.float32))
    xh = xs.reshape(b, l, SSD_HEADS, SSD_HEADDIM).astype(jnp.float32)
    y = ssd_chunked(pad_front(xh), pad_front(dt), a,
                    pad_front(bs.reshape(b, l, SSD_GROUPS, D_STATE).astype(jnp.float32)),
                    pad_front(cs.reshape(b, l, SSD_GROUPS, D_STATE).astype(jnp.float32)))[:, PAD:]
    y = y + d_skip.astype(jnp.float32)[:, None] * xh
    y = gated_rmsnorm(y.reshape(b, l, D_INNER), z, ssd_norm_g).astype(u.dtype)
    merged = jax.nn.sigmoid(g_attn) * (attn @ w_br_attn) + jax.nn.sigmoid(g_ssd) * (y @ w_br_ssd)
    return merged @ w_o


def hierarchical_moe(h, w_rg, b_rg, w_re, b_re, w_gate, w_up, w_down):
    b, l, d = h.shape
    t = b * l
    xt = h.reshape(t, d)
    p_group = jax.nn.softmax((xt @ w_rg).astype(jnp.float32) + b_rg.astype(jnp.float32), axis=-1)
    p_top, g_idx = lax.top_k(p_group, 1)
    le = ((xt @ w_re).astype(jnp.float32) + b_re.astype(jnp.float32)).reshape(t, MOE_GROUPS, EXPERTS_PER_GROUP)
    le_sel = jnp.take_along_axis(le, g_idx[:, :, None], axis=1)[:, 0]
    w_in_grp, e_in_grp = lax.top_k(jax.nn.softmax(le_sel, axis=-1), TOP_K)
    w_in_grp = w_in_grp / jnp.sum(w_in_grp, axis=-1, keepdims=True)
    weight = (p_top * w_in_grp).astype(h.dtype)
    expert = g_idx * EXPERTS_PER_GROUP + e_in_grp
    n = t * TOP_K
    nb = -(-n // MOE_BLOCK) + N_EXPERTS
    e_flat = expert.reshape(n)
    w_flat = weight.reshape(n)
    tok_flat = jnp.arange(n, dtype=jnp.int32) // TOP_K
    order = jnp.argsort(e_flat, stable=True)
    e_sorted = e_flat[order]
    counts = jnp.zeros((N_EXPERTS,), jnp.int32).at[e_flat].add(1)
    start = jnp.cumsum(counts) - counts
    pcounts = (counts + MOE_BLOCK - 1) // MOE_BLOCK * MOE_BLOCK
    pend = jnp.cumsum(pcounts)
    pstart = pend - pcounts
    dest = pstart[e_sorted] + jnp.arange(n, dtype=jnp.int32) - start[e_sorted]
    buf_tok = jnp.full((nb * MOE_BLOCK,), t, jnp.int32).at[dest].set(tok_flat[order])
    buf_w = jnp.zeros((nb * MOE_BLOCK,), h.dtype).at[dest].set(w_flat[order])
    blk_e = jnp.minimum(jnp.searchsorted(pend, jnp.arange(nb, dtype=jnp.int32) * MOE_BLOCK, side='right'),
                        N_EXPERTS - 1).astype(jnp.int32)
    x_pad = jnp.concatenate([xt, jnp.zeros((1, d), xt.dtype)], axis=0)
    xb = x_pad[buf_tok].reshape(nb, MOE_BLOCK, d)

    def expert_block(args):
        xblk, e = args
        hid = jax.nn.silu(xblk @ w_gate[e]) * (xblk @ w_up[e])
        return hid @ w_down[e]

    yb = lax.map(expert_block, (xb, blk_e)).reshape(nb * MOE_BLOCK, d) * buf_w[:, None]
    out = jnp.zeros((t + 1, d), h.dtype).at[buf_tok].add(yb)[:t]
    return out.reshape(b, l, d)


def setup_inputs(seed: int = 0) -> dict:
    key = jax.random.key(seed)
    ks = jax.random.split(key, 26)
    f32 = jnp.float32

    def nrm(k, shape, scale):
        return jax.random.normal(k, shape, f32) * scale

    dt = jnp.exp(jax.random.uniform(ks[7], (DEPTH, SSD_HEADS), f32, math.log(1e-3), math.log(1e-1)))
    return {
        'x': nrm(ks[0], (BATCH, SEQ, D_MODEL), 1.0),
        'meta_tokens': nrm(ks[1], (N_META, D_MODEL), 1.0),
        'ln_emb_g': 1.0 + nrm(ks[2], (D_MODEL,), 0.02),
        'ln_emb_b': nrm(ks[3], (D_MODEL,), 0.02),
        'w_in': nrm(ks[4], (DEPTH, D_MODEL, IN_DIM), D_MODEL ** -0.5),
        'conv_w': nrm(ks[5], (DEPTH, CONV_W, CONV_DIM), CONV_W ** -0.5),
        'conv_b': nrm(ks[6], (DEPTH, CONV_DIM), 0.01),
        'dt_bias': dt + jnp.log(-jnp.expm1(-dt)),
        'a_log': jnp.log(jax.random.uniform(ks[8], (DEPTH, SSD_HEADS), f32, 1.0, 16.0)),
        'd_skip': 1.0 + nrm(ks[9], (DEPTH, SSD_HEADS), 0.02),
        'ssd_norm_g': 1.0 + nrm(ks[10], (DEPTH, D_INNER), 0.02),
        'sinks': nrm(ks[11], (DEPTH, HQ), 0.5),
        'w_br_attn': nrm(ks[12], (DEPTH, HQ * HD, D_MODEL), (HQ * HD) ** -0.5),
        'w_br_ssd': nrm(ks[13], (DEPTH, D_INNER, D_MODEL), D_INNER ** -0.5),
        'w_o': nrm(ks[14], (DEPTH, D_MODEL, D_MODEL), BETA * D_MODEL ** -0.5),
        'ln1_g': 1.0 + nrm(ks[15], (DEPTH, D_MODEL), 0.02),
        'ln1_b': nrm(ks[16], (DEPTH, D_MODEL), 0.02),
        'w_router_group': nrm(ks[17], (DEPTH, D_MODEL, MOE_GROUPS), D_MODEL ** -0.5),
        'b_router_group': nrm(ks[18], (DEPTH, MOE_GROUPS), 0.01),
        'w_router_expert': nrm(ks[19], (DEPTH, D_MODEL, N_EXPERTS), D_MODEL ** -0.5),
        'b_router_expert': nrm(ks[20], (DEPTH, N_EXPERTS), 0.01),
        'w_gate': nrm(ks[21], (DEPTH, N_EXPERTS, D_MODEL, D_EXPERT), D_MODEL ** -0.5),
        'w_up': nrm(ks[22], (DEPTH, N_EXPERTS, D_MODEL, D_EXPERT), D_MODEL ** -0.5),
        'w_down': nrm(ks[23], (DEPTH, N_EXPERTS, D_EXPERT, D_MODEL), BETA * D_EXPERT ** -0.5),
        'ln2_g': 1.0 + nrm(ks[24], (DEPTH, D_MODEL), 0.02),
        'ln2_b': nrm(ks[25], (DEPTH, D_MODEL), 0.02),
    }


def reference(x, meta_tokens, ln_emb_g, ln_emb_b, w_in, conv_w, conv_b, dt_bias, a_log, d_skip,
              ssd_norm_g, sinks, w_br_attn, w_br_ssd, w_o, ln1_g, ln1_b, w_router_group,
              b_router_group, w_router_expert, b_router_expert, w_gate, w_up, w_down, ln2_g, ln2_b):
    b = x.shape[0]
    meta = jnp.broadcast_to(meta_tokens.astype(x.dtype)[None], (b, N_META, x.shape[-1]))
    h = layer_norm(jnp.concatenate([meta, x], axis=1), ln_emb_g, ln_emb_b)
    for i in range(DEPTH):
        mix = hybrid_mixer(h, w_in[i], conv_w[i], conv_b[i], dt_bias[i], a_log[i], d_skip[i],
                           ssd_norm_g[i], sinks[i], w_br_attn[i], w_br_ssd[i], w_o[i])
        h = layer_norm(ALPHA * h + mix, ln1_g[i], ln1_b[i])
        ffn = hierarchical_moe(h, w_router_group[i], b_router_group[i], w_router_expert[i],
                               b_router_expert[i], w_gate[i], w_up[i], w_down[i])
        h = layer_norm(ALPHA * h + ffn, ln2_g[i], ln2_b[i])
    return h[:, N_META:]
```

```python
import functools

import jax
import jax.numpy as jnp
from jax import lax
from jax.experimental import pallas as pl
from jax.experimental.pallas import tpu as pltpu

F32 = jnp.float32
BF16 = jnp.bfloat16
HIGHEST = lax.Precision.HIGHEST

D_MODEL = 2048
N_META = 16
BLOCK = 128
PAD = BLOCK - N_META
WINDOW = 128
HQ, HKV, HD = 16, 4, 64
Q_PER_KV = HQ // HKV
D_INNER = 2048
SSD_HEADDIM = 64
SSD_HEADS = D_INNER // SSD_HEADDIM
SSD_GROUPS = 4
HEADS_PER_GROUP = SSD_HEADS // SSD_GROUPS
D_STATE = 128
CONV_W = 4
CONV_DIM = D_INNER + 2 * SSD_GROUPS * D_STATE
MOE_GROUPS = 8
EXPERTS_PER_GROUP = 8
N_EXPERTS = MOE_GROUPS * EXPERTS_PER_GROUP
TOP_K = 2
D_EXPERT = 512
LN_EPS = 1e-5
RMS_EPS = 1e-5
NEG_INF = -1e30
DEPTH = 1
ALPHA = (2.0 * DEPTH) ** 0.25

Q_DIM = HQ * HD
KV_DIM = HKV * HD
MAIN_DIM = Q_DIM + 2 * KV_DIM + D_INNER + CONV_DIM
DT_OFF = MAIN_DIM
GATE_OFF = MAIN_DIM + SSD_HEADS
COL_TILE = 512
MAIN_TILES = MAIN_DIM // COL_TILE
Z_COL, XS_COL, Q_COL, K_COL, V_COL, B_COL, C_COL = 0, 2048, 4096, 5120, 5376, 5632, 6144

LANES = 128
MOE_ROWS = 256
VMEM_LIMIT = 56 * 1024 * 1024


def _cparams(sem, vmem=VMEM_LIMIT):
    return pltpu.CompilerParams(dimension_semantics=sem, vmem_limit_bytes=vmem)


def _ln_rows(x, g, b):
    mu = jnp.mean(x, axis=-1, keepdims=True)
    xc = x - mu
    var = jnp.mean(xc * xc, axis=-1, keepdims=True)
    return xc * lax.rsqrt(var + LN_EPS) * g + b


def _silu(x):
    return x * (1.0 / (1.0 + jnp.exp(-x)))


def _sigmoid(x):
    return 1.0 / (1.0 + jnp.exp(-x))


def _row_tile(n_blocks, max_blocks):
    best = 1
    for c in range(1, max_blocks + 1):
        if n_blocks % c == 0:
            best = c
    return best * BLOCK


def _embed_ln_kernel(x_ref, meta_ref, g_ref, b_ref, h_ref, hb_ref):
    i = pl.program_id(1)
    src = jnp.where(i == 0, meta_ref[...], x_ref[...])
    y = _ln_rows(src, g_ref[...], b_ref[...])
    row = lax.broadcasted_iota(jnp.int32, (BLOCK, 1), 0)
    y = jnp.where((i > 0) | (row >= PAD), y, 0.0)
    h_ref[...] = y
    hb_ref[...] = y.astype(BF16)


def _embed_ln(x, meta_pad, g, b):
    bsz, seq, d = x.shape
    nb = seq // BLOCK + 1
    m = bsz * nb * BLOCK
    return pl.pallas_call(
        _embed_ln_kernel,
        out_shape=(jax.ShapeDtypeStruct((m, d), F32), jax.ShapeDtypeStruct((m, d), BF16)),
        grid=(bsz, nb),
        in_specs=[
            pl.BlockSpec((None, BLOCK, d), lambda bb, i: (bb, jnp.maximum(i - 1, 0), 0)),
            pl.BlockSpec((BLOCK, d), lambda bb, i: (0, 0)),
            pl.BlockSpec((1, d), lambda bb, i: (0, 0)),
            pl.BlockSpec((1, d), lambda bb, i: (0, 0)),
        ],
        out_specs=(pl.BlockSpec((BLOCK, d), lambda bb, i: (bb * nb + i, 0)),
                   pl.BlockSpec((BLOCK, d), lambda bb, i: (bb * nb + i, 0))),
        compiler_params=_cparams(("parallel", "parallel")),
        name="embed_ln",
    )(x, meta_pad, g, b)


def _inproj_kernel(a_ref, w_ref, o_ref):
    o_ref[...] = jnp.dot(a_ref[...], w_ref[...].astype(BF16),
                         preferred_element_type=F32).astype(o_ref.dtype)


def _main_dest_tile(j):
    return jnp.where(j < 2, j + 8, jnp.where(j == 2, 10, jnp.where(j < 11, j - 3, j)))


def _inproj(hb, w_in):
    m, d = hb.shape
    tm = _row_tile(m // BLOCK, 13)
    return pl.pallas_call(
        _inproj_kernel,
        out_shape=jax.ShapeDtypeStruct((m, MAIN_DIM), BF16),
        grid=(m // tm, MAIN_TILES),
        in_specs=[
            pl.BlockSpec((tm, d), lambda i, j: (i, 0)),
            pl.BlockSpec((None, d, COL_TILE), lambda i, j: (0, 0, j)),
        ],
        out_specs=pl.BlockSpec((tm, COL_TILE), lambda i, j: (i, _main_dest_tile(j))),
        compiler_params=_cparams(("parallel", "arbitrary")),
        name="in_proj",
    )(hb, w_in)


def _attn_kernel(sinks_ref, q_ref, kc_ref, vc_ref, kp_ref, vp_ref, km_ref, vm_ref, o_ref):
    n = pl.program_id(1)
    r = lax.broadcasted_iota(jnp.int32, (BLOCK, BLOCK), 0)
    c = lax.broadcasted_iota(jnp.int32, (BLOCK, BLOCK), 1)
    ok_meta = (c >= PAD) & ((n > 0) | (c <= r))
    ok_prev = (c > r) & (n >= 2)
    ok_cur = (c <= r) & (n >= 1)
    ok = jnp.concatenate([ok_meta, ok_prev, ok_cur], axis=1)
    ok = jnp.concatenate([ok] * Q_PER_KV, axis=0)
    scale = HD ** -0.5
    for h in range(HKV):
        ks = slice(h * HD, (h + 1) * HD)
        k_all = jnp.concatenate([km_ref[:, ks], kp_ref[:, ks], kc_ref[:, ks]], axis=0)
        v_all = jnp.concatenate([vm_ref[:, ks], vp_ref[:, ks], vc_ref[:, ks]], axis=0)
        q4 = jnp.concatenate(
            [q_ref[:, (h * Q_PER_KV + g) * HD:(h * Q_PER_KV + g + 1) * HD] for g in range(Q_PER_KV)],
            axis=0)
        s = lax.dot_general(q4, k_all, (((1,), (1,)), ((), ())),
                            preferred_element_type=F32) * scale
        s = jnp.where(ok, s, NEG_INF)
        sink = jnp.concatenate(
            [jnp.full((BLOCK, 1), sinks_ref[h * Q_PER_KV + g], F32) for g in range(Q_PER_KV)], axis=0)
        mx = jnp.maximum(jnp.max(s, axis=-1, keepdims=True), sink)
        p = jnp.exp(s - mx)
        den = jnp.sum(p, axis=-1, keepdims=True) + jnp.exp(sink - mx)
        o = jnp.dot(p.astype(BF16), v_all, preferred_element_type=F32) / den
        for g in range(Q_PER_KV):
            hq = h * Q_PER_KV + g
            o_ref[:, hq * HD:(hq + 1) * HD] = o[g * BLOCK:(g + 1) * BLOCK].astype(o_ref.dtype)


def _attention(main, sinks, bsz, nb):
    m = main.shape[0]
    kb, vb = K_COL // KV_DIM, V_COL // KV_DIM
    cur = lambda bb, i: bb * nb + i
    prev = lambda bb, i: bb * nb + jnp.maximum(i - 1, 0)
    first = lambda bb, i: bb * nb
    return pl.pallas_call(
        _attn_kernel,
        out_shape=jax.ShapeDtypeStruct((m, Q_DIM), BF16),
        grid=(bsz, nb),
        in_specs=[
            pl.BlockSpec(memory_space=pltpu.SMEM),
            pl.BlockSpec((BLOCK, Q_DIM), lambda bb, i: (cur(bb, i), Q_COL // Q_DIM)),
            pl.BlockSpec((BLOCK, KV_DIM), lambda bb, i: (cur(bb, i), kb)),
            pl.BlockSpec((BLOCK, KV_DIM), lambda bb, i: (cur(bb, i), vb)),
            pl.BlockSpec((BLOCK, KV_DIM), lambda bb, i: (prev(bb, i), kb)),
            pl.BlockSpec((BLOCK, KV_DIM), lambda bb, i: (prev(bb, i), vb)),
            pl.BlockSpec((BLOCK, KV_DIM), lambda bb, i: (first(bb, i), kb)),
            pl.BlockSpec((BLOCK, KV_DIM), lambda bb, i: (first(bb, i), vb)),
        ],
        out_specs=pl.BlockSpec((BLOCK, Q_DIM), lambda bb, i: (cur(bb, i), 0)),
        compiler_params=_cparams(("parallel", "parallel")),
        name="swa_attention",
    )(sinks, main, main, main, main, main, main, main)


CIN_ROWS = BLOCK + 8


def _ssd_kernel(hb_ref, wdt_ref, z_ref, xs_ref, bm_ref, cm_ref,
                cw_ref, cbias_ref, dtb_ref, a_ref, dsk_ref, ng_ref,
                y_ref,
                cin_ref, xc_ref, state_ref, yacc_ref, acs_ref, acst_ref, dtt_ref, wt_ref, ea_ref):
    i = pl.program_id(1)
    row = lax.broadcasted_iota(jnp.int32, (BLOCK, 1), 0)
    live = jnp.where((i > 0) | (row >= PAD), 1.0, 0.0)

    @pl.when(i == 0)
    def _():
        state_ref[...] = jnp.zeros_like(state_ref)
        cin_ref[0:8, :] = jnp.zeros((8, CONV_DIM), F32)

    @pl.when(i > 0)
    def _():
        cin_ref[0:8, :] = cin_ref[BLOCK:BLOCK + 8, :]

    cin_ref[8:CIN_ROWS, 0:D_INNER] = xs_ref[...].astype(F32)
    cin_ref[8:CIN_ROWS, D_INNER:D_INNER + 512] = bm_ref[...].astype(F32)
    cin_ref[8:CIN_ROWS, D_INNER + 512:CONV_DIM] = cm_ref[...].astype(F32)

    for cc in range(CONV_DIM // COL_TILE):
        cs = slice(cc * COL_TILE, (cc + 1) * COL_TILE)
        acc = jnp.broadcast_to(cbias_ref[:, cs], (BLOCK, COL_TILE))
        for j in range(CONV_W):
            acc = acc + cw_ref[j:j + 1, cs] * cin_ref[8 - (CONV_W - 1) + j:8 - (CONV_W - 1) + j + BLOCK, cs]
        xc_ref[:, cs] = _silu(acc) * live

    dt_raw = jnp.dot(hb_ref[...], wdt_ref[...], preferred_element_type=F32) + dtb_ref[...]
    dt = (jnp.maximum(dt_raw, 0.0) + jnp.log1p(jnp.exp(-jnp.abs(dt_raw)))) * live
    adt = dt * a_ref[...]
    rr = lax.broadcasted_iota(jnp.int32, (BLOCK, BLOCK), 0)
    cl = lax.broadcasted_iota(jnp.int32, (BLOCK, BLOCK), 1)
    causal = rr >= cl
    a_cs = jnp.dot(jnp.where(causal, 1.0, 0.0), adt, precision=HIGHEST, preferred_element_type=F32)
    a_cs_t = a_cs.T
    dt_t = dt.T
    last = a_cs_t[:, BLOCK - 1:BLOCK]
    acs_ref[...] = a_cs
    acst_ref[...] = a_cs_t
    dtt_ref[...] = dt_t
    wt_ref[...] = dt_t * jnp.exp(last - a_cs_t)
    ea_ref[...] = jnp.exp(a_cs)
    cd_t = jnp.exp(last)

    lane = lax.broadcasted_iota(jnp.int32, (1, LANES), 1)
    lo = lane < SSD_HEADDIM
    nn = (((1,), (1,)), ((), ()))
    for g in range(SSD_GROUPS):
        bg = xc_ref[:, D_INNER + g * D_STATE:D_INNER + (g + 1) * D_STATE]
        cg = xc_ref[:, D_INNER + 512 + g * D_STATE:D_INNER + 512 + (g + 1) * D_STATE]
        cb = lax.dot_general(cg.astype(BF16), bg.astype(BF16), nn, preferred_element_type=F32)
        bt = bg.T
        for pp in range(HEADS_PER_GROUP // 2):
            pr = g * (HEADS_PER_GROUP // 2) + pp
            ps = slice(pr * LANES, (pr + 1) * LANES)
            xs_pair = xc_ref[:, ps]
            st_pair = state_ref[:, ps]
            lhs, lhs_s = [], []
            for hd in (2 * pr, 2 * pr + 1):
                col = jnp.broadcast_to(acs_ref[:, hd:hd + 1], (BLOCK, BLOCK))
                seg = col - acst_ref[hd:hd + 1, :]
                dec = jnp.exp(jnp.where(causal, seg, NEG_INF))
                lhs.append((cb * dec * dtt_ref[hd:hd + 1, :]).astype(BF16))
                lhs.append((cg * jnp.broadcast_to(ea_ref[:, hd:hd + 1], (BLOCK, BLOCK))).astype(BF16))
                lhs_s.append((bt * wt_ref[hd:hd + 1, :]).astype(BF16))
            xs_lo = jnp.where(lo, xs_pair, 0.0).astype(BF16)
            xs_hi = jnp.where(lo, 0.0, xs_pair).astype(BF16)
            st_lo = jnp.where(lo, st_pair, 0.0).astype(BF16)
            st_hi = jnp.where(lo, 0.0, st_pair).astype(BF16)
            y_pair = jnp.dot(jnp.concatenate(lhs, axis=1),
                             jnp.concatenate([xs_lo, st_lo, xs_hi, st_hi], axis=0),
                             preferred_element_type=F32)
            yacc_ref[:, ps] = y_pair + dsk_ref[:, ps] * xs_pair
            st_new = jnp.dot(jnp.concatenate(lhs_s, axis=1),
                             jnp.concatenate([xs_lo, xs_hi], axis=0), preferred_element_type=F32)
            cd_row = jnp.where(lo, jnp.broadcast_to(cd_t[2 * pr:2 * pr + 1, :], (1, LANES)),
                               jnp.broadcast_to(cd_t[2 * pr + 1:2 * pr + 2, :], (1, LANES)))
            state_ref[:, ps] = st_pair * cd_row + st_new

    gw = D_INNER // SSD_GROUPS
    for g in range(SSD_GROUPS):
        gs = slice(g * gw, (g + 1) * gw)
        yz = yacc_ref[:, gs] * _silu(z_ref[:, gs].astype(F32))
        ms = jnp.mean(yz * yz, axis=-1, keepdims=True)
        y_ref[:, gs] = (yz * lax.rsqrt(ms + RMS_EPS) * ng_ref[:, gs]).astype(y_ref.dtype)


def _ssd(hb, main, w_dt, conv_w, conv_b, dt_bias, a_neg, d_skip_x, norm_g, bsz, nb):
    m = hb.shape[0]
    cur = lambda bb, i: bb * nb + i
    full = lambda shape: pl.BlockSpec(shape, lambda bb, i: (0, 0))
    return pl.pallas_call(
        _ssd_kernel,
        out_shape=jax.ShapeDtypeStruct((m, D_INNER), BF16),
        grid=(bsz, nb),
        in_specs=[
            pl.BlockSpec((BLOCK, D_MODEL), lambda bb, i: (cur(bb, i), 0)),
            full((D_MODEL, LANES)),
            pl.BlockSpec((BLOCK, D_INNER), lambda bb, i: (cur(bb, i), Z_COL // D_INNER)),
            pl.BlockSpec((BLOCK, D_INNER), lambda bb, i: (cur(bb, i), XS_COL // D_INNER)),
            pl.BlockSpec((BLOCK, 512), lambda bb, i: (cur(bb, i), B_COL // 512)),
            pl.BlockSpec((BLOCK, 512), lambda bb, i: (cur(bb, i), C_COL // 512)),
            full((CONV_W, CONV_DIM)), full((1, CONV_DIM)), full((1, LANES)), full((1, LANES)),
            full((1, D_INNER)), full((1, D_INNER)),
        ],
        out_specs=pl.BlockSpec((BLOCK, D_INNER), lambda bb, i: (cur(bb, i), 0)),
        scratch_shapes=[
            pltpu.VMEM((CIN_ROWS, CONV_DIM), F32),
            pltpu.VMEM((BLOCK, CONV_DIM), F32),
            pltpu.VMEM((D_STATE, D_INNER), F32),
            pltpu.VMEM((BLOCK, D_INNER), F32),
            pltpu.VMEM((BLOCK, LANES), F32),
            pltpu.VMEM((LANES, BLOCK), F32),
            pltpu.VMEM((LANES, BLOCK), F32),
            pltpu.VMEM((LANES, BLOCK), F32),
            pltpu.VMEM((BLOCK, LANES), F32),
        ],
        compiler_params=_cparams(("arbitrary", "arbitrary")),
        name="ssd",
    )(hb, w_dt, main, main, main, main, conv_w, conv_b, dt_bias, a_neg, d_skip_x, norm_g)


def _merge_kernel(hb_ref, at_ref, y_ref, wga_ref, wgs_ref, wa_ref, ws_ref, o_ref):
    hb = hb_ref[...]
    ga = jnp.dot(hb, wga_ref[...], preferred_element_type=F32)
    gs = jnp.dot(hb, wgs_ref[...], preferred_element_type=F32)
    pa = jnp.dot(at_ref[...], wa_ref[...], preferred_element_type=F32)
    ps = jnp.dot(y_ref[...], ws_ref[...], preferred_element_type=F32)
    o_ref[...] = (_sigmoid(ga) * pa + _sigmoid(gs) * ps).astype(o_ref.dtype)


def _merge(hb, attn, y, w_gates, w_a, w_s):
    m = hb.shape[0]
    tm = _row_tile(m // BLOCK, 5)
    nt = D_MODEL // COL_TILE
    return pl.pallas_call(
        _merge_kernel,
        out_shape=jax.ShapeDtypeStruct((m, D_MODEL), BF16),
        grid=(m // tm, nt),
        in_specs=[
            pl.BlockSpec((tm, D_MODEL), lambda i, j: (i, 0)),
            pl.BlockSpec((tm, Q_DIM), lambda i, j: (i, 0)),
            pl.BlockSpec((tm, D_INNER), lambda i, j: (i, 0)),
            pl.BlockSpec((D_MODEL, COL_TILE), lambda i, j: (0, j)),
            pl.BlockSpec((D_MODEL, COL_TILE), lambda i, j: (0, j + nt)),
            pl.BlockSpec((Q_DIM, COL_TILE), lambda i, j: (0, j)),
            pl.BlockSpec((D_INNER, COL_TILE), lambda i, j: (0, j)),
        ],
        out_specs=pl.BlockSpec((tm, COL_TILE), lambda i, j: (i, j)),
        compiler_params=_cparams(("parallel", "arbitrary")),
        name="branch_merge",
    )(hb, attn, y, w_gates, w_gates, w_a, w_s)


def _outproj_kernel(mg_ref, h_ref, wo_ref, g_ref, b_ref, o_ref):
    mix = jnp.dot(mg_ref[...], wo_ref[...], preferred_element_type=F32)
    o_ref[...] = _ln_rows(ALPHA * h_ref[...] + mix, g_ref[...], b_ref[...])


def _outproj_ln(merged, h0, w_o, g, b):
    m = h0.shape[0]
    tm = _row_tile(m // BLOCK, 5)
    return pl.pallas_call(
        _outproj_kernel,
        out_shape=jax.ShapeDtypeStruct((m, D_MODEL), F32),
        grid=(m // tm,),
        in_specs=[
            pl.BlockSpec((tm, D_MODEL), lambda i: (i, 0)),
            pl.BlockSpec((tm, D_MODEL), lambda i: (i, 0)),
            pl.BlockSpec((D_MODEL, D_MODEL), lambda i: (0, 0)),
            pl.BlockSpec((1, D_MODEL), lambda i: (0, 0)),
            pl.BlockSpec((1, D_MODEL), lambda i: (0, 0)),
        ],
        out_specs=pl.BlockSpec((tm, D_MODEL), lambda i: (i, 0)),
        compiler_params=_cparams(("parallel",)),
        name="out_proj_ln1",
    )(merged, h0, w_o, g, b)


def _router_kernel(h_ref, w_ref, b_ref, ids_ref, wts_ref, cnt_ref, carry_ref, *, nb):
    step = pl.program_id(0)

    @pl.when(step == 0)
    def _():
        carry_ref[...] = jnp.zeros_like(carry_ref)

    logits = jnp.dot(h_ref[...], w_ref[...], precision=HIGHEST, preferred_element_type=F32) + b_ref[...]
    lane = lax.broadcasted_iota(jnp.int32, (BLOCK, LANES), 1)
    row = lax.broadcasted_iota(jnp.int32, (BLOCK, 1), 0)
    real = ((step % nb) > 0) | (row >= PAD)
    big = jnp.int32(LANES)

    is_g = lane < MOE_GROUPS
    gl = jnp.where(is_g, logits, -jnp.inf)
    ge = jnp.exp(gl - jnp.max(gl, axis=-1, keepdims=True))
    pg = ge / jnp.sum(ge, axis=-1, keepdims=True)
    p_top = jnp.max(pg, axis=-1, keepdims=True)
    g_idx = jnp.min(jnp.where(is_g & (pg == p_top), lane, big), axis=-1, keepdims=True)

    base = MOE_GROUPS + g_idx * EXPERTS_PER_GROUP
    sel = (lane >= base) & (lane < base + EXPERTS_PER_GROUP)
    el = jnp.where(sel, logits, -jnp.inf)
    ee = jnp.exp(el - jnp.max(el, axis=-1, keepdims=True))
    pe = ee / jnp.sum(ee, axis=-1, keepdims=True)
    v1 = jnp.max(jnp.where(sel, pe, -1.0), axis=-1, keepdims=True)
    i1 = jnp.min(jnp.where(sel & (pe == v1), lane, big), axis=-1, keepdims=True)
    sel2 = sel & (lane != i1)
    v2 = jnp.max(jnp.where(sel2, pe, -1.0), axis=-1, keepdims=True)
    i2 = jnp.min(jnp.where(sel2 & (pe == v2), lane, big), axis=-1, keepdims=True)
    vs = v1 + v2
    w1 = p_top * (v1 / vs)
    w2 = p_top * (v2 / vs)

    hit1 = (lane == i1) & real
    hit2 = (lane == i2) & real
    onehot = jnp.where(hit1 | hit2, 1.0, 0.0)
    rr = lax.broadcasted_iota(jnp.int32, (BLOCK, BLOCK), 0)
    cl = lax.broadcasted_iota(jnp.int32, (BLOCK, BLOCK), 1)
    before = jnp.where(rr > cl, 1.0, 0.0).astype(BF16)
    cum = jnp.dot(before, onehot.astype(BF16), preferred_element_type=F32) + carry_ref[...]
    r1 = jnp.sum(jnp.where(hit1, cum, 0.0), axis=-1, keepdims=True)
    r2 = jnp.sum(jnp.where(hit2, cum, 0.0), axis=-1, keepdims=True)
    carry_ref[...] = carry_ref[...] + jnp.sum(onehot, axis=0, keepdims=True)
    cnt_ref[...] = carry_ref[...].astype(jnp.int32)

    e1 = jnp.where(real, i1 - MOE_GROUPS, -1)
    e2 = jnp.where(real, i2 - MOE_GROUPS, -1)
    l8 = lax.broadcasted_iota(jnp.int32, (BLOCK, 8), 1)
    ids_ref[...] = jnp.where(l8 == 0, e1, jnp.where(l8 == 1, e2, jnp.where(
        l8 == 2, r1.astype(jnp.int32), jnp.where(l8 == 3, r2.astype(jnp.int32), 0))))
    wts_ref[...] = jnp.where(l8 == 0, w1, jnp.where(l8 == 1, w2, 0.0))


def _router(h1, w_r, b_r, nb):
    m = h1.shape[0]
    return pl.pallas_call(
        functools.partial(_router_kernel, nb=nb),
        out_shape=(jax.ShapeDtypeStruct((m, 8), jnp.int32), jax.ShapeDtypeStruct((m, 8), F32),
                   jax.ShapeDtypeStruct((1, LANES), jnp.int32)),
        grid=(m // BLOCK,),
        in_specs=[
            pl.BlockSpec((BLOCK, D_MODEL), lambda i: (i, 0)),
            pl.BlockSpec((D_MODEL, LANES), lambda i: (0, 0)),
            pl.BlockSpec((1, LANES), lambda i: (0, 0)),
        ],
        out_specs=(pl.BlockSpec((BLOCK, 8), lambda i: (i, 0)),
                   pl.BlockSpec((BLOCK, 8), lambda i: (i, 0)),
                   pl.BlockSpec((1, LANES), lambda i: (0, 0))),
        scratch_shapes=[pltpu.VMEM((1, LANES), F32)],
        compiler_params=_cparams(("arbitrary",)),
        name="moe_router",
    )(h1, w_r, b_r)


def _expert_kernel(blk_e_ref, src_ref, nused_ref, h_hbm, wg_ref, wu_ref, wd_ref, o_ref,
                   xbuf, wgb, wub, wdb, sems):
    i = pl.program_id(0)
    n_used = nused_ref[0]

    def row_copy(blk, slot, r):
        tok = src_ref[blk * MOE_ROWS + r]
        return pltpu.make_async_copy(h_hbm.at[pl.ds(tok, 1), :], xbuf.at[slot, pl.ds(r, 1), :],
                                     sems.at[slot])

    def start_gather(blk, slot):
        def body(r, carry):
            row_copy(blk, slot, r).start()
            return carry
        lax.fori_loop(0, MOE_ROWS, body, 0)

    def wait_gather(blk, slot):
        def body(r, carry):
            row_copy(blk, slot, r).wait()
            return carry
        lax.fori_loop(0, MOE_ROWS, body, 0)

    @pl.when(i == 0)
    def _():
        start_gather(0, 0)

    @pl.when(i + 1 < n_used)
    def _():
        start_gather(i + 1, (i + 1) % 2)

    @pl.when(i < n_used)
    def _():
        fresh = (i == 0) | (blk_e_ref[i] != blk_e_ref[jnp.maximum(i - 1, 0)])

        @pl.when(fresh)
        def _():
            wgb[...] = wg_ref[...].astype(BF16)
            wub[...] = wu_ref[...].astype(BF16)
            wdb[...] = wd_ref[...].astype(BF16)

        slot = i % 2
        wait_gather(i, slot)
        xb = xbuf[slot].astype(BF16)
        gate = jnp.dot(xb, wgb[...], preferred_element_type=F32)
        up = jnp.dot(xb, wub[...], preferred_element_type=F32)
        hid = (_silu(gate) * up).astype(BF16)
        o_ref[...] = jnp.dot(hid, wdb[...], preferred_element_type=F32)

    @pl.when(i >= n_used)
    def _():
        o_ref[...] = jnp.zeros_like(o_ref)


def _experts(h1, w_gate, w_up, w_down, blk_e, src_rows, n_used, n_blocks):
    grid_spec = pltpu.PrefetchScalarGridSpec(
        num_scalar_prefetch=3,
        grid=(n_blocks,),
        in_specs=[
            pl.BlockSpec(memory_space=pl.ANY),
            pl.BlockSpec((None, None, D_MODEL, D_EXPERT), lambda i, be, sr, nu: (0, be[i], 0, 0)),
            pl.BlockSpec((None, None, D_MODEL, D_EXPERT), lambda i, be, sr, nu: (0, be[i], 0, 0)),
            pl.BlockSpec((None, None, D_EXPERT, D_MODEL), lambda i, be, sr, nu: (0, be[i], 0, 0)),
        ],
        out_specs=pl.BlockSpec((MOE_ROWS, D_MODEL), lambda i, be, sr, nu: (i, 0)),
        scratch_shapes=[
            pltpu.VMEM((2, MOE_ROWS, D_MODEL), F32),
            pltpu.VMEM((D_MODEL, D_EXPERT), BF16),
            pltpu.VMEM((D_MODEL, D_EXPERT), BF16),
            pltpu.VMEM((D_EXPERT, D_MODEL), BF16),
            pltpu.SemaphoreType.DMA((2,)),
        ],
    )
    return pl.pallas_call(
        _expert_kernel,
        out_shape=jax.ShapeDtypeStruct((n_blocks * MOE_ROWS, D_MODEL), F32),
        grid_spec=grid_spec,
        compiler_params=_cparams(("arbitrary",)),
        name="moe_experts",
    )(blk_e, src_rows, n_used, h1, w_gate, w_up, w_down)


def _combine_kernel(dest_ref, h_ref, wts_ref, yb_hbm, g_ref, b_ref, o_ref, ybuf, sems, *, nb):
    bb = pl.program_id(0)
    i = pl.program_id(1)
    n_i = pl.num_programs(1)
    step = bb * n_i + i
    n_steps = pl.num_programs(0) * n_i

    def row_copy(s, slot, r):
        tok = ((s // n_i) * nb + (s % n_i) + 1) * BLOCK + r // TOP_K
        d = dest_ref[tok * TOP_K + r % TOP_K]
        return pltpu.make_async_copy(yb_hbm.at[pl.ds(d, 1), :],
                                     ybuf.at[slot, r % TOP_K, pl.ds(r // TOP_K, 1), :], sems.at[slot])

    def start_gather(s, slot):
        def body(r, carry):
            row_copy(s, slot, r).start()
            return carry
        lax.fori_loop(0, TOP_K * BLOCK, body, 0)

    def wait_gather(s, slot):
        def body(r, carry):
            row_copy(s, slot, r).wait()
            return carry
        lax.fori_loop(0, TOP_K * BLOCK, body, 0)

    @pl.when(step == 0)
    def _():
        start_gather(0, 0)

    @pl.when(step + 1 < n_steps)
    def _():
        start_gather(step + 1, (step + 1) % 2)

    slot = step % 2
    wait_gather(step, slot)
    wts = wts_ref[...]
    ffn = ybuf[slot, 0] * wts[:, 0:1] + ybuf[slot, 1] * wts[:, 1:2]
    o_ref[...] = _ln_rows(ALPHA * h_ref[...] + ffn, g_ref[...], b_ref[...])


def _combine_ln(h1, wts, yb, dest, g, b, bsz, nb):
    grid_spec = pltpu.PrefetchScalarGridSpec(
        num_scalar_prefetch=1,
        grid=(bsz, nb - 1),
        in_specs=[
            pl.BlockSpec((BLOCK, D_MODEL), lambda bb, i, d: (bb * nb + i + 1, 0)),
            pl.BlockSpec((BLOCK, 8), lambda bb, i, d: (bb * nb + i + 1, 0)),
            pl.BlockSpec(memory_space=pl.ANY),
            pl.BlockSpec((1, D_MODEL), lambda bb, i, d: (0, 0)),
            pl.BlockSpec((1, D_MODEL), lambda bb, i, d: (0, 0)),
        ],
        out_specs=pl.BlockSpec((None, BLOCK, D_MODEL), lambda bb, i, d: (bb, i, 0)),
        scratch_shapes=[
            pltpu.VMEM((2, TOP_K, BLOCK, D_MODEL), F32),
            pltpu.SemaphoreType.DMA((2,)),
        ],
    )
    return pl.pallas_call(
        functools.partial(_combine_kernel, nb=nb),
        out_shape=jax.ShapeDtypeStruct((bsz, (nb - 1) * BLOCK, D_MODEL), F32),
        grid_spec=grid_spec,
        compiler_params=_cparams(("arbitrary", "arbitrary")),
        name="moe_combine_ln2",
    )(dest, h1, wts, yb, g, b)


def _dispatch_tables(ids, counts, n_blocks):
    m = ids.shape[0]
    counts = counts[0, MOE_GROUPS:MOE_GROUPS + N_EXPERTS]
    pcounts = (counts + MOE_ROWS - 1) // MOE_ROWS * MOE_ROWS
    pend = jnp.cumsum(pcounts)
    pstart = pend - pcounts
    expert = ids[:, 0:TOP_K]
    rank = ids[:, TOP_K:2 * TOP_K]
    dest = jnp.where(expert >= 0, pstart[jnp.maximum(expert, 0)] + rank, 0).astype(jnp.int32)
    n_used = jnp.maximum(pend[-1] // MOE_ROWS, 1).astype(jnp.int32)
    blk = jnp.arange(n_blocks, dtype=jnp.int32)
    blk_e = jnp.minimum(jnp.searchsorted(pend, blk * MOE_ROWS, side='right'), N_EXPERTS - 1)
    blk_e = jnp.where(blk < n_used, blk_e, blk_e[n_used - 1]).astype(jnp.int32)
    tok = jnp.broadcast_to(jnp.arange(m, dtype=jnp.int32)[:, None], (m, TOP_K))
    slot = jnp.where(expert >= 0, dest, n_blocks * MOE_ROWS)
    src_rows = jnp.zeros((n_blocks * MOE_ROWS,), jnp.int32).at[slot.reshape(-1)].set(
        tok.reshape(-1), mode='drop')
    return dest.reshape(-1), src_rows, blk_e, n_used.reshape(1)


def kernel(x, meta_tokens, ln_emb_g, ln_emb_b, w_in, conv_w, conv_b, dt_bias, a_log, d_skip, ssd_norm_g, sinks, w_br_attn, w_br_ssd, w_o, ln1_g, ln1_b, w_router_group, b_router_group, w_router_expert, b_router_expert, w_gate, w_up, w_down, ln2_g, ln2_b):
    bsz, seq, d = x.shape
    assert d == D_MODEL and seq % BLOCK == 0 and w_in.shape[0] == DEPTH
    nb = seq // BLOCK + 1
    m = bsz * nb * BLOCK
    row2 = lambda v: v.reshape(1, -1).astype(F32)

    meta_pad = jnp.concatenate([jnp.zeros((PAD, d), F32), meta_tokens.astype(F32)], axis=0)
    h0, h0b = _embed_ln(x, meta_pad, row2(ln_emb_g), row2(ln_emb_b))

    main = _inproj(h0b, w_in)
    attn = _attention(main, sinks[0].astype(F32), bsz, nb)

    w_dt = jnp.pad(w_in[0][:, DT_OFF:DT_OFF + SSD_HEADS], ((0, 0), (0, LANES - SSD_HEADS))).astype(BF16)
    pad_h = lambda v: jnp.pad(v.astype(F32), (0, LANES - SSD_HEADS)).reshape(1, LANES)
    y = _ssd(h0b, main, w_dt, conv_w[0].astype(F32), row2(conv_b[0]), pad_h(dt_bias[0]),
             pad_h(-jnp.exp(a_log[0].astype(F32))), row2(jnp.repeat(d_skip[0], SSD_HEADDIM)),
             row2(ssd_norm_g[0]), bsz, nb)

    w_gates = w_in[0][:, GATE_OFF:].astype(BF16)
    merged = _merge(h0b, attn, y, w_gates, w_br_attn[0].astype(BF16), w_br_ssd[0].astype(BF16))
    h1 = _outproj_ln(merged, h0, w_o[0].astype(BF16), row2(ln1_g[0]), row2(ln1_b[0]))

    w_r = jnp.pad(jnp.concatenate([w_router_group[0], w_router_expert[0]], axis=1).astype(F32),
                  ((0, 0), (0, LANES - MOE_GROUPS - N_EXPERTS)))
    b_r = jnp.pad(jnp.concatenate([b_router_group[0], b_router_expert[0]]).astype(F32),
                  (0, LANES - MOE_GROUPS - N_EXPERTS)).reshape(1, LANES)
    ids, wts, counts = _router(h1, w_r, b_r, nb)

    n_assign = bsz * (seq + N_META) * TOP_K
    n_blocks = -(-n_assign // MOE_ROWS) + N_EXPERTS
    dest, src_rows, blk_e, n_used = _dispatch_tables(ids, counts, n_blocks)
    yb = _experts(h1, w_gate, w_up, w_down, blk_e, src_rows, n_used, n_blocks)
    return _combine_ln(h1, wts, yb, dest, row2(ln2_g[0]), row2(ln2_b[0]), bsz, nb)
```

```python
import functools

import jax
import jax.numpy as jnp
from jax import lax
from jax.experimental import pallas as pl
from jax.experimental.pallas import tpu as pltpu

F32 = jnp.float32
BF16 = jnp.bfloat16
HIGHEST = lax.Precision.HIGHEST

D_MODEL = 2048
N_META = 16
BLOCK = 128
PAD = BLOCK - N_META
WINDOW = 128
HQ, HKV, HD = 16, 4, 64
Q_PER_KV = HQ // HKV
D_INNER = 2048
SSD_HEADDIM = 64
SSD_HEADS = D_INNER // SSD_HEADDIM
SSD_GROUPS = 4
HEADS_PER_GROUP = SSD_HEADS // SSD_GROUPS
D_STATE = 128
CONV_W = 4
CONV_DIM = D_INNER + 2 * SSD_GROUPS * D_STATE
MOE_GROUPS = 8
EXPERTS_PER_GROUP = 8
N_EXPERTS = MOE_GROUPS * EXPERTS_PER_GROUP
TOP_K = 2
D_EXPERT = 512
LN_EPS = 1e-5
RMS_EPS = 1e-5
NEG_INF = -1e30
DEPTH = 1
ALPHA = (2.0 * DEPTH) ** 0.25

Q_DIM = HQ * HD
KV_DIM = HKV * HD
MAIN_DIM = Q_DIM + 2 * KV_DIM + D_INNER + CONV_DIM
DT_OFF = MAIN_DIM
GATE_OFF = MAIN_DIM + SSD_HEADS
COL_TILE = 512
MAIN_TILES = MAIN_DIM // COL_TILE
Z_COL, XS_COL, Q_COL, K_COL, V_COL, B_COL, C_COL = 0, 2048, 4096, 5120, 5376, 5632, 6144

LANES = 128
MOE_ROWS = 256
VMEM_LIMIT = 56 * 1024 * 1024


def _cparams(sem, vmem=VMEM_LIMIT):
    return pltpu.CompilerParams(dimension_semantics=sem, vmem_limit_bytes=vmem)


def _ln_rows(x, g, b):
    mu = jnp.mean(x, axis=-1, keepdims=True)
    xc = x - mu
    var = jnp.mean(xc * xc, axis=-1, keepdims=True)
    return xc * lax.rsqrt(var + LN_EPS) * g + b


def _silu(x):
    return x * (1.0 / (1.0 + jnp.exp(-x)))


def _sigmoid(x):
    return 1.0 / (1.0 + jnp.exp(-x))


def _row_tile(n_blocks, max_blocks):
    best = 1
    for c in range(1, max_blocks + 1):
        if n_blocks % c == 0:
            best = c
    return best * BLOCK


def _embed_ln_kernel(x_ref, meta_ref, g_ref, b_ref, h_ref, hb_ref):
    i = pl.program_id(1)
    src = jnp.where(i == 0, meta_ref[...], x_ref[...])
    y = _ln_rows(src, g_ref[...], b_ref[...])
    row = lax.broadcasted_iota(jnp.int32, (BLOCK, 1), 0)
    y = jnp.where((i > 0) | (row >= PAD), y, 0.0)
    h_ref[...] = y
    hb_ref[...] = y.astype(BF16)


def _embed_ln(x, meta_pad, g, b):
    bsz, seq, d = x.shape
    nb = seq // BLOCK + 1
    m = bsz * nb * BLOCK
    return pl.pallas_call(
        _embed_ln_kernel,
        out_shape=(jax.ShapeDtypeStruct((m, d), F32), jax.ShapeDtypeStruct((m, d), BF16)),
        grid=(bsz, nb),
        in_specs=[
            pl.BlockSpec((None, BLOCK, d), lambda bb, i: (bb, jnp.maximum(i - 1, 0), 0)),
            pl.BlockSpec((BLOCK, d), lambda bb, i: (0, 0)),
            pl.BlockSpec((1, d), lambda bb, i: (0, 0)),
            pl.BlockSpec((1, d), lambda bb, i: (0, 0)),
        ],
        out_specs=(pl.BlockSpec((BLOCK, d), lambda bb, i: (bb * nb + i, 0)),
                   pl.BlockSpec((BLOCK, d), lambda bb, i: (bb * nb + i, 0))),
        compiler_params=_cparams(("parallel", "parallel")),
        name="embed_ln",
    )(x, meta_pad, g, b)


def _inproj_kernel(a_ref, w_ref, o_ref):
    o_ref[...] = jnp.dot(a_ref[...], w_ref[...].astype(BF16),
                         preferred_element_type=F32).astype(o_ref.dtype)


def _main_dest_tile(j):
    return jnp.where(j < 2, j + 8, jnp.where(j == 2, 10, jnp.where(j < 11, j - 3, j)))


def _inproj(hb, w_in):
    m, d = hb.shape
    tm = _row_tile(m // BLOCK, 13)
    return pl.pallas_call(
        _inproj_kernel,
        out_shape=jax.ShapeDtypeStruct((m, MAIN_DIM), BF16),
        grid=(m // tm, MAIN_TILES),
        in_specs=[
            pl.BlockSpec((tm, d), lambda i, j: (i, 0)),
            pl.BlockSpec((None, d, COL_TILE), lambda i, j: (0, 0, j)),
        ],
        out_specs=pl.BlockSpec((tm, COL_TILE), lambda i, j: (i, _main_dest_tile(j))),
        compiler_params=_cparams(("parallel", "arbitrary")),
        name="in_proj",
    )(hb, w_in)


def _attn_kernel(sinks_ref, q_ref, kc_ref, vc_ref, kp_ref, vp_ref, km_ref, vm_ref, o_ref):
    n = pl.program_id(1)
    r = lax.broadcasted_iota(jnp.int32, (BLOCK, BLOCK), 0)
    c = lax.broadcasted_iota(jnp.int32, (BLOCK, BLOCK), 1)
    is_cur = c <= r
    ok_band = (is_cur & (n >= 1)) | ((c > r) & (n >= 2))
    rm = lax.broadcasted_iota(jnp.int32, (BLOCK, N_META), 0)
    cm = lax.broadcasted_iota(jnp.int32, (BLOCK, N_META), 1)
    ok_meta = (n > 0) | (cm <= rm - PAD)
    scale = HD ** -0.5
    nt = (((1,), (1,)), ((), ()))
    for h in range(HKV):
        ks = slice(h * HD, (h + 1) * HD)
        kc, kp, km = kc_ref[:, ks], kp_ref[:, ks], km_ref[PAD:BLOCK, ks]
        vc, vp, vm = vc_ref[:, ks], vp_ref[:, ks], vm_ref[PAD:BLOCK, ks]
        q4 = jnp.concatenate(
            [q_ref[:, (h * Q_PER_KV + g) * HD:(h * Q_PER_KV + g + 1) * HD] for g in range(Q_PER_KV)],
            axis=0)
        s_c4 = lax.dot_general(q4, kc, nt, preferred_element_type=F32)
        s_p4 = lax.dot_general(q4, kp, nt, preferred_element_type=F32)
        s_m4 = lax.dot_general(q4, km, nt, preferred_element_type=F32)
        pc, pp, pm, dens = [], [], [], []
        for g in range(Q_PER_KV):
            rows = slice(g * BLOCK, (g + 1) * BLOCK)
            sink = sinks_ref[h * Q_PER_KV + g]
            s_b = jnp.where(ok_band, jnp.where(is_cur, s_c4[rows], s_p4[rows]) * scale, NEG_INF)
            s_m = jnp.where(ok_meta, s_m4[rows] * scale, NEG_INF)
            mx = jnp.maximum(jnp.maximum(jnp.max(s_b, axis=-1, keepdims=True),
                                         jnp.max(s_m, axis=-1, keepdims=True)), sink)
            p_b = jnp.exp(s_b - mx)
            p_m = jnp.exp(s_m - mx)
            dens.append(jnp.sum(p_b, axis=-1, keepdims=True) + jnp.sum(p_m, axis=-1, keepdims=True)
                        + jnp.exp(sink - mx))
            pc.append(jnp.where(is_cur, p_b, 0.0).astype(BF16))
            pp.append(jnp.where(is_cur, 0.0, p_b).astype(BF16))
            pm.append(p_m.astype(BF16))
        o4 = (jnp.dot(jnp.concatenate(pc, axis=0), vc, preferred_element_type=F32)
              + jnp.dot(jnp.concatenate(pp, axis=0), vp, preferred_element_type=F32)
              + jnp.dot(jnp.concatenate(pm, axis=0), vm, preferred_element_type=F32))
        for g in range(Q_PER_KV):
            hq = h * Q_PER_KV + g
            o_ref[:, hq * HD:(hq + 1) * HD] = (o4[g * BLOCK:(g + 1) * BLOCK] / dens[g]).astype(o_ref.dtype)


def _attention(main, sinks, bsz, nb):
    m = main.shape[0]
    kb, vb = K_COL // KV_DIM, V_COL // KV_DIM
    cur = lambda bb, i: bb * nb + i
    prev = lambda bb, i: bb * nb + jnp.maximum(i - 1, 0)
    first = lambda bb, i: bb * nb
    return pl.pallas_call(
        _attn_kernel,
        out_shape=jax.ShapeDtypeStruct((m, Q_DIM), BF16),
        grid=(bsz, nb),
        in_specs=[
            pl.BlockSpec(memory_space=pltpu.SMEM),
            pl.BlockSpec((BLOCK, Q_DIM), lambda bb, i: (cur(bb, i), Q_COL // Q_DIM)),
            pl.BlockSpec((BLOCK, KV_DIM), lambda bb, i: (cur(bb, i), kb)),
            pl.BlockSpec((BLOCK, KV_DIM), lambda bb, i: (cur(bb, i), vb)),
            pl.BlockSpec((BLOCK, KV_DIM), lambda bb, i: (prev(bb, i), kb)),
            pl.BlockSpec((BLOCK, KV_DIM), lambda bb, i: (prev(bb, i), vb)),
            pl.BlockSpec((BLOCK, KV_DIM), lambda bb, i: (first(bb, i), kb)),
            pl.BlockSpec((BLOCK, KV_DIM), lambda bb, i: (first(bb, i), vb)),
        ],
        out_specs=pl.BlockSpec((BLOCK, Q_DIM), lambda bb, i: (cur(bb, i), 0)),
        compiler_params=_cparams(("parallel", "parallel")),
        name="swa_attention",
    )(sinks, main, main, main, main, main, main, main)


CIN_ROWS = BLOCK + 8


def _ssd_kernel(hb_ref, wdt_ref, z_ref, xs_ref, bm_ref, cm_ref,
                cw_ref, cbias_ref, dtb_ref, a_ref, dsk_ref, ng_ref,
                y_ref,
                cin_ref, xc_ref, state_ref, yacc_ref, acs_ref, acst_ref, dtt_ref, wt_ref, ea_ref):
    i = pl.program_id(1)
    row = lax.broadcasted_iota(jnp.int32, (BLOCK, 1), 0)
    live = jnp.where((i > 0) | (row >= PAD), 1.0, 0.0)

    @pl.when(i == 0)
    def _():
        state_ref[...] = jnp.zeros_like(state_ref)
        cin_ref[0:8, :] = jnp.zeros((8, CONV_DIM), F32)

    @pl.when(i > 0)
    def _():
        cin_ref[0:8, :] = cin_ref[BLOCK:BLOCK + 8, :]

    cin_ref[8:CIN_ROWS, 0:D_INNER] = xs_ref[...].astype(F32)
    cin_ref[8:CIN_ROWS, D_INNER:D_INNER + 512] = bm_ref[...].astype(F32)
    cin_ref[8:CIN_ROWS, D_INNER + 512:CONV_DIM] = cm_ref[...].astype(F32)

    for cc in range(CONV_DIM // COL_TILE):
        cs = slice(cc * COL_TILE, (cc + 1) * COL_TILE)
        acc = jnp.broadcast_to(cbias_ref[:, cs], (BLOCK, COL_TILE))
        for j in range(CONV_W):
            acc = acc + cw_ref[j:j + 1, cs] * cin_ref[8 - (CONV_W - 1) + j:8 - (CONV_W - 1) + j + BLOCK, cs]
        xc_ref[:, cs] = _silu(acc) * live

    dt_raw = jnp.dot(hb_ref[...], wdt_ref[...], preferred_element_type=F32) + dtb_ref[...]
    dt = (jnp.maximum(dt_raw, 0.0) + jnp.log1p(jnp.exp(-jnp.abs(dt_raw)))) * live
    adt = dt * a_ref[...]
    rr = lax.broadcasted_iota(jnp.int32, (BLOCK, BLOCK), 0)
    cl = lax.broadcasted_iota(jnp.int32, (BLOCK, BLOCK), 1)
    causal = rr >= cl
    a_cs = jnp.dot(jnp.where(causal, 1.0, 0.0), adt, precision=HIGHEST, preferred_element_type=F32)
    a_cs_t = a_cs.T
    dt_t = dt.T
    last = a_cs_t[:, BLOCK - 1:BLOCK]
    acs_ref[...] = a_cs
    acst_ref[...] = a_cs_t
    dtt_ref[...] = dt_t
    wt_ref[...] = dt_t * jnp.exp(last - a_cs_t)
    ea_ref[...] = jnp.exp(a_cs)
    cd_t = jnp.exp(last)

    lane = lax.broadcasted_iota(jnp.int32, (1, LANES), 1)
    lo = lane < SSD_HEADDIM
    nn = (((1,), (1,)), ((), ()))
    for g in range(SSD_GROUPS):
        bg = xc_ref[:, D_INNER + g * D_STATE:D_INNER + (g + 1) * D_STATE]
        cg = xc_ref[:, D_INNER + 512 + g * D_STATE:D_INNER + 512 + (g + 1) * D_STATE]
        cb = lax.dot_general(cg.astype(BF16), bg.astype(BF16), nn, preferred_element_type=F32)
        bt = bg.T
        for pp in range(HEADS_PER_GROUP // 2):
            pr = g * (HEADS_PER_GROUP // 2) + pp
            ps = slice(pr * LANES, (pr + 1) * LANES)
            xs_pair = xc_ref[:, ps]
            st_pair = state_ref[:, ps]
            lhs, lhs_s = [], []
            for hd in (2 * pr, 2 * pr + 1):
                col = jnp.broadcast_to(acs_ref[:, hd:hd + 1], (BLOCK, BLOCK))
                seg = col - acst_ref[hd:hd + 1, :]
                dec = jnp.exp(jnp.where(causal, seg, NEG_INF))
                lhs.append((cb * dec * dtt_ref[hd:hd + 1, :]).astype(BF16))
                lhs.append((cg * jnp.broadcast_to(ea_ref[:, hd:hd + 1], (BLOCK, BLOCK))).astype(BF16))
                lhs_s.append((bt * wt_ref[hd:hd + 1, :]).astype(BF16))
            xs_lo = jnp.where(lo, xs_pair, 0.0).astype(BF16)
            xs_hi = jnp.where(lo, 0.0, xs_pair).astype(BF16)
            st_lo = jnp.where(lo, st_pair, 0.0).astype(BF16)
            st_hi = jnp.where(lo, 0.0, st_pair).astype(BF16)
            y_pair = jnp.dot(jnp.concatenate(lhs, axis=1),
                             jnp.concatenate([xs_lo, st_lo, xs_hi, st_hi], axis=0),
                             preferred_element_type=F32)
            yacc_ref[:, ps] = y_pair + dsk_ref[:, ps] * xs_pair
            st_new = jnp.dot(jnp.concatenate(lhs_s, axis=1),
                             jnp.concatenate([xs_lo, xs_hi], axis=0), preferred_element_type=F32)
            cd_row = jnp.where(lo, jnp.broadcast_to(cd_t[2 * pr:2 * pr + 1, :], (1, LANES)),
                               jnp.broadcast_to(cd_t[2 * pr + 1:2 * pr + 2, :], (1, LANES)))
            state_ref[:, ps] = st_pair * cd_row + st_new

    gw = D_INNER // SSD_GROUPS
    for g in range(SSD_GROUPS):
        gs = slice(g * gw, (g + 1) * gw)
        yz = yacc_ref[:, gs] * _silu(z_ref[:, gs].astype(F32))
        ms = jnp.mean(yz * yz, axis=-1, keepdims=True)
        y_ref[:, gs] = (yz * lax.rsqrt(ms + RMS_EPS) * ng_ref[:, gs]).astype(y_ref.dtype)


def _ssd(hb, main, w_dt, conv_w, conv_b, dt_bias, a_neg, d_skip_x, norm_g, bsz, nb):
    m = hb.shape[0]
    cur = lambda bb, i: bb * nb + i
    full = lambda shape: pl.BlockSpec(shape, lambda bb, i: (0, 0))
    return pl.pallas_call(
        _ssd_kernel,
        out_shape=jax.ShapeDtypeStruct((m, D_INNER), BF16),
        grid=(bsz, nb),
        in_specs=[
            pl.BlockSpec((BLOCK, D_MODEL), lambda bb, i: (cur(bb, i), 0)),
            full((D_MODEL, LANES)),
            pl.BlockSpec((BLOCK, D_INNER), lambda bb, i: (cur(bb, i), Z_COL // D_INNER)),
            pl.BlockSpec((BLOCK, D_INNER), lambda bb, i: (cur(bb, i), XS_COL // D_INNER)),
            pl.BlockSpec((BLOCK, 512), lambda bb, i: (cur(bb, i), B_COL // 512)),
            pl.BlockSpec((BLOCK, 512), lambda bb, i: (cur(bb, i), C_COL // 512)),
            full((CONV_W, CONV_DIM)), full((1, CONV_DIM)), full((1, LANES)), full((1, LANES)),
            full((1, D_INNER)), full((1, D_INNER)),
        ],
        out_specs=pl.BlockSpec((BLOCK, D_INNER), lambda bb, i: (cur(bb, i), 0)),
        scratch_shapes=[
            pltpu.VMEM((CIN_ROWS, CONV_DIM), F32),
            pltpu.VMEM((BLOCK, CONV_DIM), F32),
            pltpu.VMEM((D_STATE, D_INNER), F32),
            pltpu.VMEM((BLOCK, D_INNER), F32),
            pltpu.VMEM((BLOCK, LANES), F32),
            pltpu.VMEM((LANES, BLOCK), F32),
            pltpu.VMEM((LANES, BLOCK), F32),
            pltpu.VMEM((LANES, BLOCK), F32),
            pltpu.VMEM((BLOCK, LANES), F32),
        ],
        compiler_params=_cparams(("arbitrary", "arbitrary")),
        name="ssd",
    )(hb, w_dt, main, main, main, main, conv_w, conv_b, dt_bias, a_neg, d_skip_x, norm_g)


def _merge_kernel(hb_ref, at_ref, y_ref, wga_ref, wgs_ref, wa_ref, ws_ref, o_ref):
    hb = hb_ref[...]
    ga = jnp.dot(hb, wga_ref[...], preferred_element_type=F32)
    gs = jnp.dot(hb, wgs_ref[...], preferred_element_type=F32)
    pa = jnp.dot(at_ref[...], wa_ref[...], preferred_element_type=F32)
    ps = jnp.dot(y_ref[...], ws_ref[...], preferred_element_type=F32)
    o_ref[...] = (_sigmoid(ga) * pa + _sigmoid(gs) * ps).astype(o_ref.dtype)


def _merge(hb, attn, y, w_gates, w_a, w_s):
    m = hb.shape[0]
    tm = _row_tile(m // BLOCK, 5)
    nt = D_MODEL // COL_TILE
    return pl.pallas_call(
        _merge_kernel,
        out_shape=jax.ShapeDtypeStruct((m, D_MODEL), BF16),
        grid=(m // tm, nt),
        in_specs=[
            pl.BlockSpec((tm, D_MODEL), lambda i, j: (i, 0)),
            pl.BlockSpec((tm, Q_DIM), lambda i, j: (i, 0)),
            pl.BlockSpec((tm, D_INNER), lambda i, j: (i, 0)),
            pl.BlockSpec((D_MODEL, COL_TILE), lambda i, j: (0, j)),
            pl.BlockSpec((D_MODEL, COL_TILE), lambda i, j: (0, j + nt)),
            pl.BlockSpec((Q_DIM, COL_TILE), lambda i, j: (0, j)),
            pl.BlockSpec((D_INNER, COL_TILE), lambda i, j: (0, j)),
        ],
        out_specs=pl.BlockSpec((tm, COL_TILE), lambda i, j: (i, j)),
        compiler_params=_cparams(("parallel", "arbitrary")),
        name="branch_merge",
    )(hb, attn, y, w_gates, w_gates, w_a, w_s)


def _outproj_router_kernel(mg_ref, h_ref, wo_ref, g_ref, b_ref, wrh_ref, wrl_ref, br_ref,
                           o_ref, ids_ref, wts_ref, cnt_ref, carry_ref, *, nb, tm):
    step = pl.program_id(0)

    @pl.when(step == 0)
    def _():
        carry_ref[...] = jnp.zeros_like(carry_ref)

    mix = jnp.dot(mg_ref[...], wo_ref[...], preferred_element_type=F32)
    h1 = _ln_rows(ALPHA * h_ref[...] + mix, g_ref[...], b_ref[...])
    o_ref[...] = h1

    h_hi = h1.astype(BF16)
    h_lo = (h1 - h_hi.astype(F32)).astype(BF16)
    logits = (jnp.dot(h_hi, wrh_ref[...], preferred_element_type=F32)
              + (jnp.dot(h_lo, wrh_ref[...], preferred_element_type=F32)
                 + jnp.dot(h_hi, wrl_ref[...], preferred_element_type=F32))) + br_ref[...]
    lane = lax.broadcasted_iota(jnp.int32, (tm, LANES), 1)
    row = lax.broadcasted_iota(jnp.int32, (BLOCK, 1), 0)
    blocks = tm // BLOCK
    real = jnp.concatenate(
        [(((step * blocks + s) % nb) > 0) | (row >= PAD) for s in range(blocks)], axis=0)
    big = jnp.int32(LANES)

    is_g = lane < MOE_GROUPS
    gl = jnp.where(is_g, logits, -jnp.inf)
    ge = jnp.exp(gl - jnp.max(gl, axis=-1, keepdims=True))
    pg = ge / jnp.sum(ge, axis=-1, keepdims=True)
    p_top = jnp.max(pg, axis=-1, keepdims=True)
    g_idx = jnp.min(jnp.where(is_g & (pg == p_top), lane, big), axis=-1, keepdims=True)

    base = MOE_GROUPS + g_idx * EXPERTS_PER_GROUP
    sel = (lane >= base) & (lane < base + EXPERTS_PER_GROUP)
    el = jnp.where(sel, logits, -jnp.inf)
    ee = jnp.exp(el - jnp.max(el, axis=-1, keepdims=True))
    pe = ee / jnp.sum(ee, axis=-1, keepdims=True)
    v1 = jnp.max(jnp.where(sel, pe, -1.0), axis=-1, keepdims=True)
    i1 = jnp.min(jnp.where(sel & (pe == v1), lane, big), axis=-1, keepdims=True)
    sel2 = sel & (lane != i1)
    v2 = jnp.max(jnp.where(sel2, pe, -1.0), axis=-1, keepdims=True)
    i2 = jnp.min(jnp.where(sel2 & (pe == v2), lane, big), axis=-1, keepdims=True)
    vs = v1 + v2
    w1 = p_top * (v1 / vs)
    w2 = p_top * (v2 / vs)

    hit1 = (lane == i1) & real
    hit2 = (lane == i2) & real
    onehot = jnp.where(hit1 | hit2, 1.0, 0.0)
    rr = lax.broadcasted_iota(jnp.int32, (tm, tm), 0)
    cl = lax.broadcasted_iota(jnp.int32, (tm, tm), 1)
    before = jnp.where(rr > cl, 1.0, 0.0).astype(BF16)
    cum = jnp.dot(before, onehot.astype(BF16), preferred_element_type=F32) + carry_ref[...]
    r1 = jnp.sum(jnp.where(hit1, cum, 0.0), axis=-1, keepdims=True)
    r2 = jnp.sum(jnp.where(hit2, cum, 0.0), axis=-1, keepdims=True)
    carry_ref[...] = carry_ref[...] + jnp.sum(onehot, axis=0, keepdims=True)
    cnt_ref[...] = carry_ref[...].astype(jnp.int32)

    e1 = jnp.where(real, i1 - MOE_GROUPS, -1)
    e2 = jnp.where(real, i2 - MOE_GROUPS, -1)
    l8 = lax.broadcasted_iota(jnp.int32, (tm, 8), 1)
    ids_ref[...] = jnp.where(l8 == 0, e1, jnp.where(l8 == 1, e2, jnp.where(
        l8 == 2, r1.astype(jnp.int32), jnp.where(l8 == 3, r2.astype(jnp.int32), 0))))
    wts_ref[...] = jnp.where(l8 == 0, w1, jnp.where(l8 == 1, w2, 0.0))


def _outproj_router(merged, h0, w_o, g, b, wr_hi, wr_lo, b_r, nb):
    m = h0.shape[0]
    tm = _row_tile(m // BLOCK, 5)
    row_blk = lambda width: pl.BlockSpec((tm, width), lambda i: (i, 0))
    full = lambda shape: pl.BlockSpec(shape, lambda i: (0, 0))
    return pl.pallas_call(
        functools.partial(_outproj_router_kernel, nb=nb, tm=tm),
        out_shape=(jax.ShapeDtypeStruct((m, D_MODEL), F32),
                   jax.ShapeDtypeStruct((m, 8), jnp.int32), jax.ShapeDtypeStruct((m, 8), F32),
                   jax.ShapeDtypeStruct((1, LANES), jnp.int32)),
        grid=(m // tm,),
        in_specs=[row_blk(D_MODEL), row_blk(D_MODEL), full((D_MODEL, D_MODEL)),
                  full((1, D_MODEL)), full((1, D_MODEL)),
                  full((D_MODEL, LANES)), full((D_MODEL, LANES)), full((1, LANES))],
        out_specs=(row_blk(D_MODEL), row_blk(8), row_blk(8), full((1, LANES))),
        scratch_shapes=[pltpu.VMEM((1, LANES), F32)],
        compiler_params=_cparams(("arbitrary",)),
        name="out_proj_ln1_router",
    )(merged, h0, w_o, g, b, wr_hi, wr_lo, b_r)


def _expert_kernel(blk_e_ref, src_ref, nused_ref, h_hbm, wg_ref, wu_ref, wd_ref, o_ref,
                   xbuf, wgb, wub, wdb, sems):
    i = pl.program_id(0)
    n_used = nused_ref[0]

    def start_gather(blk, slot):
        base = blk * MOE_ROWS

        def body(r, carry):
            tok = src_ref[base + r]
            pltpu.make_async_copy(h_hbm.at[pl.ds(tok, 1), :], xbuf.at[slot, pl.ds(r, 1), :],
                                  sems.at[slot]).start()
            return carry
        lax.fori_loop(0, MOE_ROWS, body, 0, unroll=8)

    def wait_gather(slot):
        pltpu.make_async_copy(h_hbm.at[pl.ds(0, MOE_ROWS), :], xbuf.at[slot], sems.at[slot]).wait()

    @pl.when(i == 0)
    def _():
        start_gather(0, 0)

    @pl.when(i + 1 < n_used)
    def _():
        start_gather(i + 1, (i + 1) % 2)

    @pl.when(i < n_used)
    def _():
        fresh = (i == 0) | (blk_e_ref[i] != blk_e_ref[jnp.maximum(i - 1, 0)])

        @pl.when(fresh)
        def _():
            wgb[...] = wg_ref[...].astype(BF16)
            wub[...] = wu_ref[...].astype(BF16)
            wdb[...] = wd_ref[...].astype(BF16)

        slot = i % 2
        wait_gather(slot)
        xb = xbuf[slot].astype(BF16)
        gate = jnp.dot(xb, wgb[...], preferred_element_type=F32)
        up = jnp.dot(xb, wub[...], preferred_element_type=F32)
        hid = (_silu(gate) * up).astype(BF16)
        o_ref[...] = jnp.dot(hid, wdb[...], preferred_element_type=F32)

    @pl.when(i >= n_used)
    def _():
        o_ref[...] = jnp.zeros_like(o_ref)


def _experts(h1, w_gate, w_up, w_down, blk_e, src_rows, n_used, n_blocks):
    grid_spec = pltpu.PrefetchScalarGridSpec(
        num_scalar_prefetch=3,
        grid=(n_blocks,),
        in_specs=[
            pl.BlockSpec(memory_space=pl.ANY),
            pl.BlockSpec((None, None, D_MODEL, D_EXPERT), lambda i, be, sr, nu: (0, be[i], 0, 0)),
            pl.BlockSpec((None, None, D_MODEL, D_EXPERT), lambda i, be, sr, nu: (0, be[i], 0, 0)),
            pl.BlockSpec((None, None, D_EXPERT, D_MODEL), lambda i, be, sr, nu: (0, be[i], 0, 0)),
        ],
        out_specs=pl.BlockSpec((MOE_ROWS, D_MODEL), lambda i, be, sr, nu: (i, 0)),
        scratch_shapes=[
            pltpu.VMEM((2, MOE_ROWS, D_MODEL), F32),
            pltpu.VMEM((D_MODEL, D_EXPERT), BF16),
            pltpu.VMEM((D_MODEL, D_EXPERT), BF16),
            pltpu.VMEM((D_EXPERT, D_MODEL), BF16),
            pltpu.SemaphoreType.DMA((2,)),
        ],
    )
    return pl.pallas_call(
        _expert_kernel,
        out_shape=jax.ShapeDtypeStruct((n_blocks * MOE_ROWS, D_MODEL), F32),
        grid_spec=grid_spec,
        compiler_params=_cparams(("arbitrary",)),
        name="moe_experts",
    )(blk_e, src_rows, n_used, h1, w_gate, w_up, w_down)


def _combine_kernel(dest_ref, h_ref, wts_ref, yb_hbm, g_ref, b_ref, o_ref, ybuf, sems, *, nb):
    bb = pl.program_id(0)
    i = pl.program_id(1)
    n_i = pl.num_programs(1)
    step = bb * n_i + i
    n_steps = pl.num_programs(0) * n_i

    def assign_base(b_, i_):
        return (b_ * nb + i_ + 1) * (BLOCK * TOP_K)

    def start_gather(base, slot):
        def body(t, carry):
            for k in range(TOP_K):
                d = dest_ref[base + t * TOP_K + k]
                pltpu.make_async_copy(yb_hbm.at[pl.ds(d, 1), :], ybuf.at[slot, k, pl.ds(t, 1), :],
                                      sems.at[slot]).start()
            return carry
        lax.fori_loop(0, BLOCK, body, 0, unroll=4)

    def wait_gather(slot):
        for k in range(TOP_K):
            pltpu.make_async_copy(yb_hbm.at[pl.ds(0, BLOCK), :], ybuf.at[slot, k], sems.at[slot]).wait()

    @pl.when(step == 0)
    def _():
        start_gather(assign_base(0, 0), 0)

    @pl.when(step + 1 < n_steps)
    def _():
        nxt = jnp.where(i + 1 < n_i, assign_base(bb, i + 1), assign_base(bb + 1, 0))
        start_gather(nxt, (step + 1) % 2)

    slot = step % 2
    wait_gather(slot)
    wts = wts_ref[...]
    ffn = ybuf[slot, 0] * wts[:, 0:1] + ybuf[slot, 1] * wts[:, 1:2]
    o_ref[...] = _ln_rows(ALPHA * h_ref[...] + ffn, g_ref[...], b_ref[...])


def _combine_ln(h1, wts, yb, dest, g, b, bsz, nb):
    grid_spec = pltpu.PrefetchScalarGridSpec(
        num_scalar_prefetch=1,
        grid=(bsz, nb - 1),
        in_specs=[
            pl.BlockSpec((BLOCK, D_MODEL), lambda bb, i, d: (bb * nb + i + 1, 0)),
            pl.BlockSpec((BLOCK, 8), lambda bb, i, d: (bb * nb + i + 1, 0)),
            pl.BlockSpec(memory_space=pl.ANY),
            pl.BlockSpec((1, D_MODEL), lambda bb, i, d: (0, 0)),
            pl.BlockSpec((1, D_MODEL), lambda bb, i, d: (0, 0)),
        ],
        out_specs=pl.BlockSpec((None, BLOCK, D_MODEL), lambda bb, i, d: (bb, i, 0)),
        scratch_shapes=[
            pltpu.VMEM((2, TOP_K, BLOCK, D_MODEL), F32),
            pltpu.SemaphoreType.DMA((2,)),
        ],
    )
    return pl.pallas_call(
        functools.partial(_combine_kernel, nb=nb),
        out_shape=jax.ShapeDtypeStruct((bsz, (nb - 1) * BLOCK, D_MODEL), F32),
        grid_spec=grid_spec,
        compiler_params=_cparams(("arbitrary", "arbitrary")),
        name="moe_combine_ln2",
    )(dest, h1, wts, yb, g, b)


def _dispatch_tables(ids, counts, n_blocks):
    m = ids.shape[0]
    counts = counts[0, MOE_GROUPS:MOE_GROUPS + N_EXPERTS]
    pcounts = (counts + MOE_ROWS - 1) // MOE_ROWS * MOE_ROWS
    pend = jnp.cumsum(pcounts)
    pstart = pend - pcounts
    expert = ids[:, 0:TOP_K]
    rank = ids[:, TOP_K:2 * TOP_K]
    dest = jnp.where(expert >= 0, pstart[jnp.maximum(expert, 0)] + rank, 0).astype(jnp.int32)
    n_used = jnp.maximum(pend[-1] // MOE_ROWS, 1).astype(jnp.int32)
    blk = jnp.arange(n_blocks, dtype=jnp.int32)
    blk_e = jnp.minimum(jnp.searchsorted(pend, blk * MOE_ROWS, side='right'), N_EXPERTS - 1)
    blk_e = jnp.where(blk < n_used, blk_e, blk_e[n_used - 1]).astype(jnp.int32)
    tok = jnp.broadcast_to(jnp.arange(m, dtype=jnp.int32)[:, None], (m, TOP_K))
    slot = jnp.where(expert >= 0, dest, n_blocks * MOE_ROWS)
    src_rows = jnp.zeros((n_blocks * MOE_ROWS,), jnp.int32).at[slot.reshape(-1)].set(
        tok.reshape(-1), mode='drop')
    return dest.reshape(-1), src_rows, blk_e, n_used.reshape(1)


def kernel(x, meta_tokens, ln_emb_g, ln_emb_b, w_in, conv_w, conv_b, dt_bias, a_log, d_skip, ssd_norm_g, sinks, w_br_attn, w_br_ssd, w_o, ln1_g, ln1_b, w_router_group, b_router_group, w_router_expert, b_router_expert, w_gate, w_up, w_down, ln2_g, ln2_b):
    bsz, seq, d = x.shape
    assert d == D_MODEL and seq % BLOCK == 0 and w_in.shape[0] == DEPTH
    nb = seq // BLOCK + 1
    m = bsz * nb * BLOCK
    row2 = lambda v: v.reshape(1, -1).astype(F32)

    meta_pad = jnp.concatenate([jnp.zeros((PAD, d), F32), meta_tokens.astype(F32)], axis=0)
    h0, h0b = _embed_ln(x, meta_pad, row2(ln_emb_g), row2(ln_emb_b))

    main = _inproj(h0b, w_in)
    attn = _attention(main, sinks[0].astype(F32), bsz, nb)

    w_dt = jnp.pad(w_in[0][:, DT_OFF:DT_OFF + SSD_HEADS], ((0, 0), (0, LANES - SSD_HEADS))).astype(BF16)
    pad_h = lambda v: jnp.pad(v.astype(F32), (0, LANES - SSD_HEADS)).reshape(1, LANES)
    y = _ssd(h0b, main, w_dt, conv_w[0].astype(F32), row2(conv_b[0]), pad_h(dt_bias[0]),
             pad_h(-jnp.exp(a_log[0].astype(F32))), row2(jnp.repeat(d_skip[0], SSD_HEADDIM)),
             row2(ssd_norm_g[0]), bsz, nb)

    w_gates = w_in[0][:, GATE_OFF:].astype(BF16)
    merged = _merge(h0b, attn, y, w_gates, w_br_attn[0].astype(BF16), w_br_ssd[0].astype(BF16))
    w_r = jnp.pad(jnp.concatenate([w_router_group[0], w_router_expert[0]], axis=1).astype(F32),
                  ((0, 0), (0, LANES - MOE_GROUPS - N_EXPERTS)))
    wr_hi = w_r.astype(BF16)
    wr_lo = (w_r - wr_hi.astype(F32)).astype(BF16)
    b_r = jnp.pad(jnp.concatenate([b_router_group[0], b_router_expert[0]]).astype(F32),
                  (0, LANES - MOE_GROUPS - N_EXPERTS)).reshape(1, LANES)
    h1, ids, wts, counts = _outproj_router(merged, h0, w_o[0].astype(BF16), row2(ln1_g[0]),
                                           row2(ln1_b[0]), wr_hi, wr_lo, b_r, nb)

    n_assign = bsz * (seq + N_META) * TOP_K
    n_blocks = -(-n_assign // MOE_ROWS) + N_EXPERTS
    dest, src_rows, blk_e, n_used = _dispatch_tables(ids, counts, n_blocks)
    yb = _experts(h1, w_gate, w_up, w_down, blk_e, src_rows, n_used, n_blocks)
    return _combine_ln(h1, wts, yb, dest, row2(ln2_g[0]), row2(ln2_b[0]), bsz, nb)
```

```python
import functools

import jax
import jax.numpy as jnp
from jax import lax
from jax.experimental import pallas as pl
from jax.experimental.pallas import tpu as pltpu

F32 = jnp.float32
BF16 = jnp.bfloat16
HIGHEST = lax.Precision.HIGHEST

D_MODEL = 2048
N_META = 16
BLOCK = 128
PAD = BLOCK - N_META
WINDOW = 128
HQ, HKV, HD = 16, 4, 64
Q_PER_KV = HQ // HKV
D_INNER = 2048
SSD_HEADDIM = 64
SSD_HEADS = D_INNER // SSD_HEADDIM
SSD_GROUPS = 4
HEADS_PER_GROUP = SSD_HEADS // SSD_GROUPS
D_STATE = 128
CONV_W = 4
CONV_DIM = D_INNER + 2 * SSD_GROUPS * D_STATE
MOE_GROUPS = 8
EXPERTS_PER_GROUP = 8
N_EXPERTS = MOE_GROUPS * EXPERTS_PER_GROUP
TOP_K = 2
D_EXPERT = 512
LN_EPS = 1e-5
RMS_EPS = 1e-5
NEG_INF = -1e30
DEPTH = 1
ALPHA = (2.0 * DEPTH) ** 0.25

Q_DIM = HQ * HD
KV_DIM = HKV * HD
MAIN_DIM = Q_DIM + 2 * KV_DIM + D_INNER + CONV_DIM
DT_OFF = MAIN_DIM
GATE_OFF = MAIN_DIM + SSD_HEADS
COL_TILE = 512
MAIN_TILES = MAIN_DIM // COL_TILE
Z_COL, XS_COL, Q_COL, K_COL, V_COL, B_COL, C_COL = 0, 2048, 4096, 5120, 5376, 5632, 6144

LANES = 128
MOE_ROWS = 288
VMEM_LIMIT = 56 * 1024 * 1024


def _cparams(sem, vmem=VMEM_LIMIT):
    return pltpu.CompilerParams(dimension_semantics=sem, vmem_limit_bytes=vmem)


def _ln_rows(x, g, b):
    mu = jnp.mean(x, axis=-1, keepdims=True)
    xc = x - mu
    var = jnp.mean(xc * xc, axis=-1, keepdims=True)
    return xc * lax.rsqrt(var + LN_EPS) * g + b


def _silu(x):
    return x * (1.0 / (1.0 + jnp.exp(-x)))


def _sigmoid(x):
    return 1.0 / (1.0 + jnp.exp(-x))


def _row_tile(n_blocks, max_blocks):
    best = 1
    for c in range(1, max_blocks + 1):
        if n_blocks % c == 0:
            best = c
    return best * BLOCK


def _embed_ln_kernel(x_ref, meta_ref, g_ref, b_ref, h_ref, hb_ref):
    i = pl.program_id(1)
    src = jnp.where(i == 0, meta_ref[...], x_ref[...])
    y = _ln_rows(src, g_ref[...], b_ref[...])
    row = lax.broadcasted_iota(jnp.int32, (BLOCK, 1), 0)
    y = jnp.where((i > 0) | (row >= PAD), y, 0.0)
    h_ref[...] = y
    hb_ref[...] = y.astype(BF16)


def _embed_ln(x, meta_pad, g, b):
    bsz, seq, d = x.shape
    nb = seq // BLOCK + 1
    m = bsz * nb * BLOCK
    return pl.pallas_call(
        _embed_ln_kernel,
        out_shape=(jax.ShapeDtypeStruct((m, d), F32), jax.ShapeDtypeStruct((m, d), BF16)),
        grid=(bsz, nb),
        in_specs=[
            pl.BlockSpec((None, BLOCK, d), lambda bb, i: (bb, jnp.maximum(i - 1, 0), 0)),
            pl.BlockSpec((BLOCK, d), lambda bb, i: (0, 0)),
            pl.BlockSpec((1, d), lambda bb, i: (0, 0)),
            pl.BlockSpec((1, d), lambda bb, i: (0, 0)),
        ],
        out_specs=(pl.BlockSpec((BLOCK, d), lambda bb, i: (bb * nb + i, 0)),
                   pl.BlockSpec((BLOCK, d), lambda bb, i: (bb * nb + i, 0))),
        compiler_params=_cparams(("parallel", "parallel")),
        name="embed_ln",
    )(x, meta_pad, g, b)


_NT = (((1,), (1,)), ((), ()))


def _inproj_kernel(a_ref, wt_ref, o_ref):
    o_ref[...] = lax.dot_general(a_ref[...], wt_ref[...].astype(BF16), _NT,
                                 preferred_element_type=F32).astype(o_ref.dtype)


def _main_dest_tile(j):
    return jnp.where(j < 2, j + 8, jnp.where(j == 2, 10, jnp.where(j < 11, j - 3, j)))


def _inproj(hb, w_in_t):
    m, d = hb.shape
    tm = _row_tile(m // BLOCK, 13)
    return pl.pallas_call(
        _inproj_kernel,
        out_shape=jax.ShapeDtypeStruct((m, MAIN_DIM), BF16),
        grid=(m // tm, MAIN_TILES),
        in_specs=[
            pl.BlockSpec((tm, d), lambda i, j: (i, 0)),
            pl.BlockSpec((None, COL_TILE, d), lambda i, j: (0, j, 0)),
        ],
        out_specs=pl.BlockSpec((tm, COL_TILE), lambda i, j: (i, _main_dest_tile(j))),
        compiler_params=_cparams(("parallel", "arbitrary")),
        name="in_proj",
    )(hb, w_in_t)


def _attn_kernel(sinks_ref, q_ref, kc_ref, vc_ref, kp_ref, vp_ref, km_ref, vm_ref, o_ref):
    n = pl.program_id(1)
    r = lax.broadcasted_iota(jnp.int32, (BLOCK, BLOCK), 0)
    c = lax.broadcasted_iota(jnp.int32, (BLOCK, BLOCK), 1)
    is_cur = c <= r
    ok_band = (is_cur & (n >= 1)) | ((c > r) & (n >= 2))
    rm = lax.broadcasted_iota(jnp.int32, (BLOCK, N_META), 0)
    cm = lax.broadcasted_iota(jnp.int32, (BLOCK, N_META), 1)
    ok_meta = (n > 0) | (cm <= rm - PAD)
    scale = HD ** -0.5
    nt = (((1,), (1,)), ((), ()))
    for h in range(HKV):
        ks = slice(h * HD, (h + 1) * HD)
        kc, kp, km = kc_ref[:, ks], kp_ref[:, ks], km_ref[PAD:BLOCK, ks]
        vc, vp, vm = vc_ref[:, ks], vp_ref[:, ks], vm_ref[PAD:BLOCK, ks]
        q4 = jnp.concatenate(
            [q_ref[:, (h * Q_PER_KV + g) * HD:(h * Q_PER_KV + g + 1) * HD] for g in range(Q_PER_KV)],
            axis=0)
        s_c4 = lax.dot_general(q4, kc, nt, preferred_element_type=F32)
        s_p4 = lax.dot_general(q4, kp, nt, preferred_element_type=F32)
        s_m4 = lax.dot_general(q4, km, nt, preferred_element_type=F32)
        pc, pp, pm, dens = [], [], [], []
        for g in range(Q_PER_KV):
            rows = slice(g * BLOCK, (g + 1) * BLOCK)
            sink = sinks_ref[h * Q_PER_KV + g]
            s_b = jnp.where(ok_band, jnp.where(is_cur, s_c4[rows], s_p4[rows]) * scale, NEG_INF)
            s_m = jnp.where(ok_meta, s_m4[rows] * scale, NEG_INF)
            mx = jnp.maximum(jnp.maximum(jnp.max(s_b, axis=-1, keepdims=True),
                                         jnp.max(s_m, axis=-1, keepdims=True)), sink)
            p_b = jnp.exp(s_b - mx)
            p_m = jnp.exp(s_m - mx)
            dens.append(jnp.sum(p_b, axis=-1, keepdims=True) + jnp.sum(p_m, axis=-1, keepdims=True)
                        + jnp.exp(sink - mx))
            pc.append(jnp.where(is_cur, p_b, 0.0).astype(BF16))
            pp.append(jnp.where(is_cur, 0.0, p_b).astype(BF16))
            pm.append(p_m.astype(BF16))
        o4 = (jnp.dot(jnp.concatenate(pc, axis=0), vc, preferred_element_type=F32)
              + jnp.dot(jnp.concatenate(pp, axis=0), vp, preferred_element_type=F32)
              + jnp.dot(jnp.concatenate(pm, axis=0), vm, preferred_element_type=F32))
        for g in range(Q_PER_KV):
            hq = h * Q_PER_KV + g
            o_ref[:, hq * HD:(hq + 1) * HD] = (o4[g * BLOCK:(g + 1) * BLOCK] / dens[g]).astype(o_ref.dtype)


def _attention(main, sinks, bsz, nb):
    m = main.shape[0]
    kb, vb = K_COL // KV_DIM, V_COL // KV_DIM
    cur = lambda bb, i: bb * nb + i
    prev = lambda bb, i: bb * nb + jnp.maximum(i - 1, 0)
    first = lambda bb, i: bb * nb
    return pl.pallas_call(
        _attn_kernel,
        out_shape=jax.ShapeDtypeStruct((m, Q_DIM), BF16),
        grid=(bsz, nb),
        in_specs=[
            pl.BlockSpec(memory_space=pltpu.SMEM),
            pl.BlockSpec((BLOCK, Q_DIM), lambda bb, i: (cur(bb, i), Q_COL // Q_DIM)),
            pl.BlockSpec((BLOCK, KV_DIM), lambda bb, i: (cur(bb, i), kb)),
            pl.BlockSpec((BLOCK, KV_DIM), lambda bb, i: (cur(bb, i), vb)),
            pl.BlockSpec((BLOCK, KV_DIM), lambda bb, i: (prev(bb, i), kb)),
            pl.BlockSpec((BLOCK, KV_DIM), lambda bb, i: (prev(bb, i), vb)),
            pl.BlockSpec((BLOCK, KV_DIM), lambda bb, i: (first(bb, i), kb)),
            pl.BlockSpec((BLOCK, KV_DIM), lambda bb, i: (first(bb, i), vb)),
        ],
        out_specs=pl.BlockSpec((BLOCK, Q_DIM), lambda bb, i: (cur(bb, i), 0)),
        compiler_params=_cparams(("parallel", "parallel")),
        name="swa_attention",
    )(sinks, main, main, main, main, main, main, main)


CIN_ROWS = BLOCK + 8


def _ssd_kernel(hb_ref, wdt_ref, z_ref, xs_ref, bm_ref, cm_ref,
                cw_ref, cbias_ref, dtb_ref, a_ref, dsk_ref, ng_ref,
                y_ref,
                cin_ref, xc_ref, state_ref, yacc_ref, acs_ref, acst_ref, dtt_ref, wt_ref, ea_ref):
    i = pl.program_id(1)
    row = lax.broadcasted_iota(jnp.int32, (BLOCK, 1), 0)
    live = jnp.where((i > 0) | (row >= PAD), 1.0, 0.0)

    @pl.when(i == 0)
    def _():
        state_ref[...] = jnp.zeros_like(state_ref)
        cin_ref[0:8, :] = jnp.zeros((8, CONV_DIM), F32)

    @pl.when(i > 0)
    def _():
        cin_ref[0:8, :] = cin_ref[BLOCK:BLOCK + 8, :]

    cin_ref[8:CIN_ROWS, 0:D_INNER] = xs_ref[...].astype(F32)
    cin_ref[8:CIN_ROWS, D_INNER:D_INNER + 512] = bm_ref[...].astype(F32)
    cin_ref[8:CIN_ROWS, D_INNER + 512:CONV_DIM] = cm_ref[...].astype(F32)

    for cc in range(CONV_DIM // COL_TILE):
        cs = slice(cc * COL_TILE, (cc + 1) * COL_TILE)
        acc = jnp.broadcast_to(cbias_ref[:, cs], (BLOCK, COL_TILE))
        for j in range(CONV_W):
            acc = acc + cw_ref[j:j + 1, cs] * cin_ref[8 - (CONV_W - 1) + j:8 - (CONV_W - 1) + j + BLOCK, cs]
        xc_ref[:, cs] = _silu(acc) * live

    dt_raw = lax.dot_general(hb_ref[...], wdt_ref[...], _NT, preferred_element_type=F32) + dtb_ref[...]
    dt = (jnp.maximum(dt_raw, 0.0) + jnp.log1p(jnp.exp(-jnp.abs(dt_raw)))) * live
    adt = dt * a_ref[...]
    rr = lax.broadcasted_iota(jnp.int32, (BLOCK, BLOCK), 0)
    cl = lax.broadcasted_iota(jnp.int32, (BLOCK, BLOCK), 1)
    causal = rr >= cl
    a_cs = jnp.dot(jnp.where(causal, 1.0, 0.0), adt, precision=HIGHEST, preferred_element_type=F32)
    a_cs_t = a_cs.T
    dt_t = dt.T
    last = a_cs_t[:, BLOCK - 1:BLOCK]
    acs_ref[...] = a_cs
    acst_ref[...] = a_cs_t
    dtt_ref[...] = dt_t
    wt_ref[...] = dt_t * jnp.exp(last - a_cs_t)
    ea_ref[...] = jnp.exp(a_cs)
    cd_t = jnp.exp(last)

    lane = lax.broadcasted_iota(jnp.int32, (1, LANES), 1)
    lo = lane < SSD_HEADDIM
    nn = (((1,), (1,)), ((), ()))
    for g in range(SSD_GROUPS):
        bg = xc_ref[:, D_INNER + g * D_STATE:D_INNER + (g + 1) * D_STATE]
        cg = xc_ref[:, D_INNER + 512 + g * D_STATE:D_INNER + 512 + (g + 1) * D_STATE]
        cb = lax.dot_general(cg.astype(BF16), bg.astype(BF16), nn, preferred_element_type=F32)
        bt = bg.T
        for pp in range(HEADS_PER_GROUP // 2):
            pr = g * (HEADS_PER_GROUP // 2) + pp
            ps = slice(pr * LANES, (pr + 1) * LANES)
            xs_pair = xc_ref[:, ps]
            st_pair = state_ref[:, ps]
            lhs, lhs_s = [], []
            for hd in (2 * pr, 2 * pr + 1):
                col = jnp.broadcast_to(acs_ref[:, hd:hd + 1], (BLOCK, BLOCK))
                seg = col - acst_ref[hd:hd + 1, :]
                dec = jnp.exp(jnp.where(causal, seg, NEG_INF))
                lhs.append((cb * dec * dtt_ref[hd:hd + 1, :]).astype(BF16))
                lhs.append((cg * jnp.broadcast_to(ea_ref[:, hd:hd + 1], (BLOCK, BLOCK))).astype(BF16))
                lhs_s.append((bt * wt_ref[hd:hd + 1, :]).astype(BF16))
            xs_lo = jnp.where(lo, xs_pair, 0.0).astype(BF16)
            xs_hi = jnp.where(lo, 0.0, xs_pair).astype(BF16)
            st_lo = jnp.where(lo, st_pair, 0.0).astype(BF16)
            st_hi = jnp.where(lo, 0.0, st_pair).astype(BF16)
            y_pair = jnp.dot(jnp.concatenate(lhs, axis=1),
                             jnp.concatenate([xs_lo, st_lo, xs_hi, st_hi], axis=0),
                             preferred_element_type=F32)
            yacc_ref[:, ps] = y_pair + dsk_ref[:, ps] * xs_pair
            st_new = jnp.dot(jnp.concatenate(lhs_s, axis=1),
                             jnp.concatenate([xs_lo, xs_hi], axis=0), preferred_element_type=F32)
            cd_row = jnp.where(lo, jnp.broadcast_to(cd_t[2 * pr:2 * pr + 1, :], (1, LANES)),
                               jnp.broadcast_to(cd_t[2 * pr + 1:2 * pr + 2, :], (1, LANES)))
            state_ref[:, ps] = st_pair * cd_row + st_new

    gw = D_INNER // SSD_GROUPS
    for g in range(SSD_GROUPS):
        gs = slice(g * gw, (g + 1) * gw)
        yz = yacc_ref[:, gs] * _silu(z_ref[:, gs].astype(F32))
        ms = jnp.mean(yz * yz, axis=-1, keepdims=True)
        y_ref[:, gs] = (yz * lax.rsqrt(ms + RMS_EPS) * ng_ref[:, gs]).astype(y_ref.dtype)


def _ssd(hb, main, w_dt, conv_w, conv_b, dt_bias, a_neg, d_skip_x, norm_g, bsz, nb):
    m = hb.shape[0]
    cur = lambda bb, i: bb * nb + i
    full = lambda shape: pl.BlockSpec(shape, lambda bb, i: (0, 0))
    return pl.pallas_call(
        _ssd_kernel,
        out_shape=jax.ShapeDtypeStruct((m, D_INNER), BF16),
        grid=(bsz, nb),
        in_specs=[
            pl.BlockSpec((BLOCK, D_MODEL), lambda bb, i: (cur(bb, i), 0)),
            full((LANES, D_MODEL)),
            pl.BlockSpec((BLOCK, D_INNER), lambda bb, i: (cur(bb, i), Z_COL // D_INNER)),
            pl.BlockSpec((BLOCK, D_INNER), lambda bb, i: (cur(bb, i), XS_COL // D_INNER)),
            pl.BlockSpec((BLOCK, 512), lambda bb, i: (cur(bb, i), B_COL // 512)),
            pl.BlockSpec((BLOCK, 512), lambda bb, i: (cur(bb, i), C_COL // 512)),
            full((CONV_W, CONV_DIM)), full((1, CONV_DIM)), full((1, LANES)), full((1, LANES)),
            full((1, D_INNER)), full((1, D_INNER)),
        ],
        out_specs=pl.BlockSpec((BLOCK, D_INNER), lambda bb, i: (cur(bb, i), 0)),
        scratch_shapes=[
            pltpu.VMEM((CIN_ROWS, CONV_DIM), F32),
            pltpu.VMEM((BLOCK, CONV_DIM), F32),
            pltpu.VMEM((D_STATE, D_INNER), F32),
            pltpu.VMEM((BLOCK, D_INNER), F32),
            pltpu.VMEM((BLOCK, LANES), F32),
            pltpu.VMEM((LANES, BLOCK), F32),
            pltpu.VMEM((LANES, BLOCK), F32),
            pltpu.VMEM((LANES, BLOCK), F32),
            pltpu.VMEM((BLOCK, LANES), F32),
        ],
        compiler_params=_cparams(("arbitrary", "arbitrary")),
        name="ssd",
    )(hb, w_dt, main, main, main, main, conv_w, conv_b, dt_bias, a_neg, d_skip_x, norm_g)


def _merge_kernel(hb_ref, at_ref, y_ref, wga_ref, wgs_ref, wa_ref, ws_ref, o_ref,
                  wga_b, wgs_b, wa_b, ws_b):
    @pl.when(pl.program_id(1) == 0)
    def _():
        wga_b[...] = wga_ref[...].astype(BF16)
        wgs_b[...] = wgs_ref[...].astype(BF16)
        wa_b[...] = wa_ref[...].astype(BF16)
        ws_b[...] = ws_ref[...].astype(BF16)

    hb = hb_ref[...]
    ga = lax.dot_general(hb, wga_b[...], _NT, preferred_element_type=F32)
    gs = lax.dot_general(hb, wgs_b[...], _NT, preferred_element_type=F32)
    pa = jnp.dot(at_ref[...], wa_b[...], preferred_element_type=F32)
    ps = jnp.dot(y_ref[...], ws_b[...], preferred_element_type=F32)
    o_ref[...] = (_sigmoid(ga) * pa + _sigmoid(gs) * ps).astype(o_ref.dtype)


def _merge(hb, attn, y, w_gates, w_a, w_s):
    m = hb.shape[0]
    tm = _row_tile(m // BLOCK, 5)
    nt = D_MODEL // COL_TILE
    once = pl.Buffered(1)
    return pl.pallas_call(
        _merge_kernel,
        out_shape=jax.ShapeDtypeStruct((m, D_MODEL), BF16),
        grid=(nt, m // tm),
        in_specs=[
            pl.BlockSpec((tm, D_MODEL), lambda j, i: (i, 0)),
            pl.BlockSpec((tm, Q_DIM), lambda j, i: (i, 0)),
            pl.BlockSpec((tm, D_INNER), lambda j, i: (i, 0)),
            pl.BlockSpec((COL_TILE, D_MODEL), lambda j, i: (j, 0), pipeline_mode=once),
            pl.BlockSpec((COL_TILE, D_MODEL), lambda j, i: (j + nt, 0), pipeline_mode=once),
            pl.BlockSpec((Q_DIM, COL_TILE), lambda j, i: (0, j), pipeline_mode=once),
            pl.BlockSpec((D_INNER, COL_TILE), lambda j, i: (0, j), pipeline_mode=once),
        ],
        out_specs=pl.BlockSpec((tm, COL_TILE), lambda j, i: (i, j)),
        scratch_shapes=[
            pltpu.VMEM((COL_TILE, D_MODEL), BF16), pltpu.VMEM((COL_TILE, D_MODEL), BF16),
            pltpu.VMEM((Q_DIM, COL_TILE), BF16), pltpu.VMEM((D_INNER, COL_TILE), BF16),
        ],
        compiler_params=_cparams(("arbitrary", "arbitrary")),
        name="branch_merge",
    )(hb, attn, y, w_gates, w_gates, w_a, w_s)


def _outproj_router_kernel(mg_ref, h_ref, wo_ref, g_ref, b_ref, wrh_ref, wrl_ref, br_ref,
                           o_ref, ids_ref, wts_ref, cnt_ref, carry_ref, *, nb, tm):
    step = pl.program_id(0)

    @pl.when(step == 0)
    def _():
        carry_ref[...] = jnp.zeros_like(carry_ref)

    mix = jnp.dot(mg_ref[...], wo_ref[...], preferred_element_type=F32)
    h1 = _ln_rows(ALPHA * h_ref[...] + mix, g_ref[...], b_ref[...])
    o_ref[...] = h1

    h_hi = h1.astype(BF16)
    h_lo = (h1 - h_hi.astype(F32)).astype(BF16)
    logits = (lax.dot_general(h_hi, wrh_ref[...], _NT, preferred_element_type=F32)
              + (lax.dot_general(h_lo, wrh_ref[...], _NT, preferred_element_type=F32)
                 + lax.dot_general(h_hi, wrl_ref[...], _NT, preferred_element_type=F32))) + br_ref[...]
    lane = lax.broadcasted_iota(jnp.int32, (tm, LANES), 1)
    row = lax.broadcasted_iota(jnp.int32, (BLOCK, 1), 0)
    blocks = tm // BLOCK
    real = jnp.concatenate(
        [(((step * blocks + s) % nb) > 0) | (row >= PAD) for s in range(blocks)], axis=0)
    big = jnp.int32(LANES)

    is_g = lane < MOE_GROUPS
    gl = jnp.where(is_g, logits, -jnp.inf)
    ge = jnp.exp(gl - jnp.max(gl, axis=-1, keepdims=True))
    pg = ge / jnp.sum(ge, axis=-1, keepdims=True)
    p_top = jnp.max(pg, axis=-1, keepdims=True)
    g_idx = jnp.min(jnp.where(is_g & (pg == p_top), lane, big), axis=-1, keepdims=True)

    base = MOE_GROUPS + g_idx * EXPERTS_PER_GROUP
    sel = (lane >= base) & (lane < base + EXPERTS_PER_GROUP)
    el = jnp.where(sel, logits, -jnp.inf)
    ee = jnp.exp(el - jnp.max(el, axis=-1, keepdims=True))
    pe = ee / jnp.sum(ee, axis=-1, keepdims=True)
    v1 = jnp.max(jnp.where(sel, pe, -1.0), axis=-1, keepdims=True)
    i1 = jnp.min(jnp.where(sel & (pe == v1), lane, big), axis=-1, keepdims=True)
    sel2 = sel & (lane != i1)
    v2 = jnp.max(jnp.where(sel2, pe, -1.0), axis=-1, keepdims=True)
    i2 = jnp.min(jnp.where(sel2 & (pe == v2), lane, big), axis=-1, keepdims=True)
    vs = v1 + v2
    w1 = p_top * (v1 / vs)
    w2 = p_top * (v2 / vs)

    hit1 = (lane == i1) & real
    hit2 = (lane == i2) & real
    onehot = jnp.where(hit1 | hit2, 1.0, 0.0)
    rr = lax.broadcasted_iota(jnp.int32, (tm, tm), 0)
    cl = lax.broadcasted_iota(jnp.int32, (tm, tm), 1)
    before = jnp.where(rr > cl, 1.0, 0.0).astype(BF16)
    cum = jnp.dot(before, onehot.astype(BF16), preferred_element_type=F32) + carry_ref[...]
    r1 = jnp.sum(jnp.where(hit1, cum, 0.0), axis=-1, keepdims=True)
    r2 = jnp.sum(jnp.where(hit2, cum, 0.0), axis=-1, keepdims=True)
    carry_ref[...] = carry_ref[...] + jnp.sum(onehot, axis=0, keepdims=True)
    cnt_ref[...] = carry_ref[...].astype(jnp.int32)

    e1 = jnp.where(real, i1 - MOE_GROUPS, -1)
    e2 = jnp.where(real, i2 - MOE_GROUPS, -1)
    ids = jnp.where(lane == 0, e1, jnp.where(lane == 1, e2, jnp.where(
        lane == 2, r1.astype(jnp.int32), jnp.where(lane == 3, r2.astype(jnp.int32), 0))))
    ids_ref[...] = ids.T[0:8, :]
    l8 = lax.broadcasted_iota(jnp.int32, (tm, 8), 1)
    wts_ref[...] = jnp.where(l8 == 0, w1, jnp.where(l8 == 1, w2, 0.0))


def _outproj_router(merged, h0, w_o, g, b, wr_hi, wr_lo, b_r, nb):
    m = h0.shape[0]
    tm = _row_tile(m // BLOCK, 5)
    row_blk = lambda width: pl.BlockSpec((tm, width), lambda i: (i, 0))
    full = lambda shape: pl.BlockSpec(shape, lambda i: (0, 0))
    return pl.pallas_call(
        functools.partial(_outproj_router_kernel, nb=nb, tm=tm),
        out_shape=(jax.ShapeDtypeStruct((m, D_MODEL), F32),
                   jax.ShapeDtypeStruct((8, m), jnp.int32), jax.ShapeDtypeStruct((m, 8), F32),
                   jax.ShapeDtypeStruct((1, LANES), jnp.int32)),
        grid=(m // tm,),
        in_specs=[row_blk(D_MODEL), row_blk(D_MODEL), full((D_MODEL, D_MODEL)),
                  full((1, D_MODEL)), full((1, D_MODEL)),
                  full((LANES, D_MODEL)), full((LANES, D_MODEL)), full((1, LANES))],
        out_specs=(row_blk(D_MODEL), pl.BlockSpec((8, tm), lambda i: (0, i)), row_blk(8),
                   full((1, LANES))),
        scratch_shapes=[pltpu.VMEM((1, LANES), F32)],
        compiler_params=_cparams(("arbitrary",)),
        name="out_proj_ln1_router",
    )(merged, h0, w_o, g, b, wr_hi, wr_lo, b_r)


def _expert_kernel(blk_e_ref, src_ref, nused_ref, h_hbm, wg_ref, wu_ref, wd_ref, o_ref,
                   xbuf, wgb, wub, wdb, sems):
    i = pl.program_id(0)
    n_used = nused_ref[0]

    def start_gather(blk, slot):
        base = blk * MOE_ROWS

        def body(r, carry):
            tok = src_ref[base + r]
            pltpu.make_async_copy(h_hbm.at[pl.ds(tok, 1), :], xbuf.at[slot, pl.ds(r, 1), :],
                                  sems.at[slot]).start()
            return carry
        lax.fori_loop(0, MOE_ROWS, body, 0, unroll=8)

    def wait_gather(slot):
        pltpu.make_async_copy(h_hbm.at[pl.ds(0, MOE_ROWS), :], xbuf.at[slot], sems.at[slot]).wait()

    @pl.when(i == 0)
    def _():
        start_gather(0, 0)

    @pl.when(i + 1 < n_used)
    def _():
        start_gather(i + 1, (i + 1) % 2)

    @pl.when(i < n_used)
    def _():
        fresh = (i == 0) | (blk_e_ref[i] != blk_e_ref[jnp.maximum(i - 1, 0)])

        @pl.when(fresh)
        def _():
            wgb[...] = wg_ref[...].astype(BF16)
            wub[...] = wu_ref[...].astype(BF16)
            wdb[...] = wd_ref[...].astype(BF16)

        slot = i % 2
        wait_gather(slot)
        xb = xbuf[slot].astype(BF16)
        gate = jnp.dot(xb, wgb[...], preferred_element_type=F32)
        up = jnp.dot(xb, wub[...], preferred_element_type=F32)
        hid = (_silu(gate) * up).astype(BF16)
        o_ref[...] = jnp.dot(hid, wdb[...], preferred_element_type=F32)

    @pl.when(i >= n_used)
    def _():
        o_ref[...] = jnp.zeros_like(o_ref)


def _experts(h1, w_gate, w_up, w_down, blk_e, src_rows, n_used, n_blocks):
    grid_spec = pltpu.PrefetchScalarGridSpec(
        num_scalar_prefetch=3,
        grid=(n_blocks,),
        in_specs=[
            pl.BlockSpec(memory_space=pl.ANY),
            pl.BlockSpec((None, None, D_MODEL, D_EXPERT), lambda i, be, sr, nu: (0, be[i], 0, 0)),
            pl.BlockSpec((None, None, D_MODEL, D_EXPERT), lambda i, be, sr, nu: (0, be[i], 0, 0)),
            pl.BlockSpec((None, None, D_EXPERT, D_MODEL), lambda i, be, sr, nu: (0, be[i], 0, 0)),
        ],
        out_specs=pl.BlockSpec((MOE_ROWS, D_MODEL), lambda i, be, sr, nu: (i, 0)),
        scratch_shapes=[
            pltpu.VMEM((2, MOE_ROWS, D_MODEL), F32),
            pltpu.VMEM((D_MODEL, D_EXPERT), BF16),
            pltpu.VMEM((D_MODEL, D_EXPERT), BF16),
            pltpu.VMEM((D_EXPERT, D_MODEL), BF16),
            pltpu.SemaphoreType.DMA((2,)),
        ],
    )
    return pl.pallas_call(
        _expert_kernel,
        out_shape=jax.ShapeDtypeStruct((n_blocks * MOE_ROWS, D_MODEL), F32),
        grid_spec=grid_spec,
        compiler_params=_cparams(("arbitrary",)),
        name="moe_experts",
    )(blk_e, src_rows, n_used, h1, w_gate, w_up, w_down)


def _combine_kernel(dest_ref, h_ref, wts_ref, yb_hbm, g_ref, b_ref, o_ref, ybuf, sems, *, nb):
    bb = pl.program_id(0)
    i = pl.program_id(1)
    n_i = pl.num_programs(1)
    step = bb * n_i + i
    n_steps = pl.num_programs(0) * n_i

    n_rows = pl.num_programs(0) * nb * BLOCK

    def assign_base(b_, i_):
        return (b_ * nb + i_ + 1) * BLOCK

    def start_gather(base, slot):
        def body(t, carry):
            for k in range(TOP_K):
                d = dest_ref[k * n_rows + base + t]
                pltpu.make_async_copy(yb_hbm.at[pl.ds(d, 1), :], ybuf.at[slot, k, pl.ds(t, 1), :],
                                      sems.at[slot]).start()
            return carry
        lax.fori_loop(0, BLOCK, body, 0, unroll=4)

    def wait_gather(slot):
        for k in range(TOP_K):
            pltpu.make_async_copy(yb_hbm.at[pl.ds(0, BLOCK), :], ybuf.at[slot, k], sems.at[slot]).wait()

    @pl.when(step == 0)
    def _():
        start_gather(assign_base(0, 0), 0)

    @pl.when(step + 1 < n_steps)
    def _():
        nxt = jnp.where(i + 1 < n_i, assign_base(bb, i + 1), assign_base(bb + 1, 0))
        start_gather(nxt, (step + 1) % 2)

    slot = step % 2
    wait_gather(slot)
    wts = wts_ref[...]
    ffn = ybuf[slot, 0] * wts[:, 0:1] + ybuf[slot, 1] * wts[:, 1:2]
    o_ref[...] = _ln_rows(ALPHA * h_ref[...] + ffn, g_ref[...], b_ref[...])


def _combine_ln(h1, wts, yb, dest, g, b, bsz, nb):
    grid_spec = pltpu.PrefetchScalarGridSpec(
        num_scalar_prefetch=1,
        grid=(bsz, nb - 1),
        in_specs=[
            pl.BlockSpec((BLOCK, D_MODEL), lambda bb, i, d: (bb * nb + i + 1, 0)),
            pl.BlockSpec((BLOCK, 8), lambda bb, i, d: (bb * nb + i + 1, 0)),
            pl.BlockSpec(memory_space=pl.ANY),
            pl.BlockSpec((1, D_MODEL), lambda bb, i, d: (0, 0)),
            pl.BlockSpec((1, D_MODEL), lambda bb, i, d: (0, 0)),
        ],
        out_specs=pl.BlockSpec((None, BLOCK, D_MODEL), lambda bb, i, d: (bb, i, 0)),
        scratch_shapes=[
            pltpu.VMEM((2, TOP_K, BLOCK, D_MODEL), F32),
            pltpu.SemaphoreType.DMA((2,)),
        ],
    )
    return pl.pallas_call(
        functools.partial(_combine_kernel, nb=nb),
        out_shape=jax.ShapeDtypeStruct((bsz, (nb - 1) * BLOCK, D_MODEL), F32),
        grid_spec=grid_spec,
        compiler_params=_cparams(("arbitrary", "arbitrary")),
        name="moe_combine_ln2",
    )(dest, h1, wts, yb, g, b)


def _dispatch_tables(ids, counts, n_blocks):
    m = ids.shape[1]
    counts = counts[0, MOE_GROUPS:MOE_GROUPS + N_EXPERTS]
    pcounts = (counts + MOE_ROWS - 1) // MOE_ROWS * MOE_ROWS
    pend = jnp.cumsum(pcounts)
    pstart = pend - pcounts
    expert = ids[0:TOP_K]
    rank = ids[TOP_K:2 * TOP_K]
    dest = jnp.where(expert >= 0, pstart[jnp.maximum(expert, 0)] + rank, 0).astype(jnp.int32)
    n_used = jnp.maximum(pend[-1] // MOE_ROWS, 1).astype(jnp.int32)
    blk = jnp.arange(n_blocks, dtype=jnp.int32)
    blk_e = jnp.minimum(jnp.sum(pend[None, :] <= (blk * MOE_ROWS)[:, None], axis=1), N_EXPERTS - 1)
    blk_e = jnp.where(blk < n_used, blk_e, blk_e[n_used - 1]).astype(jnp.int32)
    tok = jnp.broadcast_to(jnp.arange(m, dtype=jnp.int32)[None, :], (TOP_K, m))
    slot = jnp.where(expert >= 0, dest, n_blocks * MOE_ROWS)
    src_rows = jnp.zeros((n_blocks * MOE_ROWS,), jnp.int32).at[slot.reshape(-1)].set(
        tok.reshape(-1), mode='drop')
    return dest.reshape(-1), src_rows, blk_e, n_used.reshape(1)


def kernel(x, meta_tokens, ln_emb_g, ln_emb_b, w_in, conv_w, conv_b, dt_bias, a_log, d_skip, ssd_norm_g, sinks, w_br_attn, w_br_ssd, w_o, ln1_g, ln1_b, w_router_group, b_router_group, w_router_expert, b_router_expert, w_gate, w_up, w_down, ln2_g, ln2_b):
    bsz, seq, d = x.shape
    assert d == D_MODEL and seq % BLOCK == 0 and w_in.shape[0] == DEPTH
    nb = seq // BLOCK + 1
    m = bsz * nb * BLOCK
    row2 = lambda v: v.reshape(1, -1).astype(F32)

    meta_pad = jnp.concatenate([jnp.zeros((PAD, d), F32), meta_tokens.astype(F32)], axis=0)
    h0, h0b = _embed_ln(x, meta_pad, row2(ln_emb_g), row2(ln_emb_b))

    w_in_t = jnp.swapaxes(w_in, 1, 2)
    main = _inproj(h0b, w_in_t)
    attn = _attention(main, sinks[0].astype(F32), bsz, nb)

    w_dt = jnp.pad(w_in_t[0, DT_OFF:DT_OFF + SSD_HEADS], ((0, LANES - SSD_HEADS), (0, 0))).astype(BF16)
    pad_h = lambda v: jnp.pad(v.astype(F32), (0, LANES - SSD_HEADS)).reshape(1, LANES)
    y = _ssd(h0b, main, w_dt, conv_w[0].astype(F32), row2(conv_b[0]), pad_h(dt_bias[0]),
             pad_h(-jnp.exp(a_log[0].astype(F32))), row2(jnp.repeat(d_skip[0], SSD_HEADDIM)),
             row2(ssd_norm_g[0]), bsz, nb)

    merged = _merge(h0b, attn, y, w_in_t[0, GATE_OFF:], w_br_attn[0], w_br_ssd[0])
    w_r = jnp.pad(jnp.concatenate([w_router_group[0].T, w_router_expert[0].T], axis=0).astype(F32),
                  ((0, LANES - MOE_GROUPS - N_EXPERTS), (0, 0)))
    wr_hi = w_r.astype(BF16)
    wr_lo = (w_r - wr_hi.astype(F32)).astype(BF16)
    b_r = jnp.pad(jnp.concatenate([b_router_group[0], b_router_expert[0]]).astype(F32),
                  (0, LANES - MOE_GROUPS - N_EXPERTS)).reshape(1, LANES)
    h1, ids, wts, counts = _outproj_router(merged, h0, w_o[0].astype(BF16), row2(ln1_g[0]),
                                           row2(ln1_b[0]), wr_hi, wr_lo, b_r, nb)

    n_assign = bsz * (seq + N_META) * TOP_K
    n_blocks = -(-n_assign // MOE_ROWS) + N_EXPERTS
    dest, src_rows, blk_e, n_used = _dispatch_tables(ids, counts, n_blocks)
    yb = _experts(h1, w_gate, w_up, w_down, blk_e, src_rows, n_used, n_blocks)
    return _combine_ln(h1, wts, yb, dest, row2(ln2_g[0]), row2(ln2_b[0]), bsz, nb)
```

```python
import functools

import jax
import jax.numpy as jnp
from jax import lax
from jax.experimental import pallas as pl
from jax.experimental.pallas import tpu as pltpu

F32 = jnp.float32
BF16 = jnp.bfloat16
HIGHEST = lax.Precision.HIGHEST

D_MODEL = 2048
N_META = 16
BLOCK = 128
PAD = BLOCK - N_META
WINDOW = 128
HQ, HKV, HD = 16, 4, 64
Q_PER_KV = HQ // HKV
D_INNER = 2048
SSD_HEADDIM = 64
SSD_HEADS = D_INNER // SSD_HEADDIM
SSD_GROUPS = 4
HEADS_PER_GROUP = SSD_HEADS // SSD_GROUPS
D_STATE = 128
CONV_W = 4
CONV_DIM = D_INNER + 2 * SSD_GROUPS * D_STATE
MOE_GROUPS = 8
EXPERTS_PER_GROUP = 8
N_EXPERTS = MOE_GROUPS * EXPERTS_PER_GROUP
TOP_K = 2
D_EXPERT = 512
LN_EPS = 1e-5
RMS_EPS = 1e-5
NEG_INF = -1e30
DEPTH = 1
ALPHA = (2.0 * DEPTH) ** 0.25

Q_DIM = HQ * HD
KV_DIM = HKV * HD
MAIN_DIM = Q_DIM + 2 * KV_DIM + D_INNER + CONV_DIM
DT_OFF = MAIN_DIM
GATE_OFF = MAIN_DIM + SSD_HEADS
COL_TILE = 512
MAIN_TILES = MAIN_DIM // COL_TILE
Z_COL, XS_COL, Q_COL, K_COL, V_COL, B_COL, C_COL = 0, 2048, 4096, 5120, 5376, 5632, 6144

LANES = 128
MOE_ROWS = 288
VMEM_LIMIT = 56 * 1024 * 1024


def _cparams(sem, vmem=VMEM_LIMIT):
    return pltpu.CompilerParams(dimension_semantics=sem, vmem_limit_bytes=vmem)


def _ln_rows(x, g, b):
    mu = jnp.mean(x, axis=-1, keepdims=True)
    xc = x - mu
    var = jnp.mean(xc * xc, axis=-1, keepdims=True)
    return xc * lax.rsqrt(var + LN_EPS) * g + b


def _silu(x):
    return x * (1.0 / (1.0 + jnp.exp(-x)))


def _sigmoid(x):
    return 1.0 / (1.0 + jnp.exp(-x))


def _row_tile(n_blocks, max_blocks):
    best = 1
    for c in range(1, max_blocks + 1):
        if n_blocks % c == 0:
            best = c
    return best * BLOCK


def _embed_ln_kernel(x_ref, meta_ref, g_ref, b_ref, h_ref, hb_ref):
    i = pl.program_id(1)
    src = jnp.where(i == 0, meta_ref[...], x_ref[...])
    y = _ln_rows(src, g_ref[...], b_ref[...])
    row = lax.broadcasted_iota(jnp.int32, (BLOCK, 1), 0)
    y = jnp.where((i > 0) | (row >= PAD), y, 0.0)
    h_ref[...] = y
    hb_ref[...] = y.astype(BF16)


def _embed_ln(x, meta_pad, g, b):
    bsz, seq, d = x.shape
    nb = seq // BLOCK + 1
    m = bsz * nb * BLOCK
    return pl.pallas_call(
        _embed_ln_kernel,
        out_shape=(jax.ShapeDtypeStruct((m, d), F32), jax.ShapeDtypeStruct((m, d), BF16)),
        grid=(bsz, nb),
        in_specs=[
            pl.BlockSpec((None, BLOCK, d), lambda bb, i: (bb, jnp.maximum(i - 1, 0), 0)),
            pl.BlockSpec((BLOCK, d), lambda bb, i: (0, 0)),
            pl.BlockSpec((1, d), lambda bb, i: (0, 0)),
            pl.BlockSpec((1, d), lambda bb, i: (0, 0)),
        ],
        out_specs=(pl.BlockSpec((BLOCK, d), lambda bb, i: (bb * nb + i, 0)),
                   pl.BlockSpec((BLOCK, d), lambda bb, i: (bb * nb + i, 0))),
        compiler_params=_cparams(("parallel", "parallel")),
        name="embed_ln",
    )(x, meta_pad, g, b)


_NT = (((1,), (1,)), ((), ()))


def _inproj_kernel(a_ref, wt_ref, o_ref):
    o_ref[...] = lax.dot_general(a_ref[...], wt_ref[...].astype(BF16), _NT,
                                 preferred_element_type=F32).astype(o_ref.dtype)


def _main_dest_tile(j):
    return jnp.where(j < 2, j + 8, jnp.where(j == 2, 10, jnp.where(j < 11, j - 3, j)))


def _inproj(hb, w_in_t):
    m, d = hb.shape
    tm = _row_tile(m // BLOCK, 13)
    return pl.pallas_call(
        _inproj_kernel,
        out_shape=jax.ShapeDtypeStruct((m, MAIN_DIM), BF16),
        grid=(m // tm, MAIN_TILES),
        in_specs=[
            pl.BlockSpec((tm, d), lambda i, j: (i, 0)),
            pl.BlockSpec((None, COL_TILE, d), lambda i, j: (0, j, 0)),
        ],
        out_specs=pl.BlockSpec((tm, COL_TILE), lambda i, j: (i, _main_dest_tile(j))),
        compiler_params=_cparams(("parallel", "arbitrary")),
        name="in_proj",
    )(hb, w_in_t)


def _attn_kernel(sinks_ref, q_ref, kc_ref, vc_ref, kp_ref, vp_ref, km_ref, vm_ref, o_ref):
    n = pl.program_id(1)
    r = lax.broadcasted_iota(jnp.int32, (BLOCK, BLOCK), 0)
    c = lax.broadcasted_iota(jnp.int32, (BLOCK, BLOCK), 1)
    is_cur = c <= r
    ok_band = (is_cur & (n >= 1)) | ((c > r) & (n >= 2))
    rm = lax.broadcasted_iota(jnp.int32, (BLOCK, N_META), 0)
    cm = lax.broadcasted_iota(jnp.int32, (BLOCK, N_META), 1)
    ok_meta = (n > 0) | (cm <= rm - PAD)
    scale = HD ** -0.5
    nt = (((1,), (1,)), ((), ()))
    for h in range(HKV):
        ks = slice(h * HD, (h + 1) * HD)
        kc, kp, km = kc_ref[:, ks], kp_ref[:, ks], km_ref[PAD:BLOCK, ks]
        vc, vp, vm = vc_ref[:, ks], vp_ref[:, ks], vm_ref[PAD:BLOCK, ks]
        q4 = jnp.concatenate(
            [q_ref[:, (h * Q_PER_KV + g) * HD:(h * Q_PER_KV + g + 1) * HD] for g in range(Q_PER_KV)],
            axis=0)
        s_c4 = lax.dot_general(q4, kc, nt, preferred_element_type=F32)
        s_p4 = lax.dot_general(q4, kp, nt, preferred_element_type=F32)
        s_m4 = lax.dot_general(q4, km, nt, preferred_element_type=F32)
        pc, pp, pm, dens = [], [], [], []
        for g in range(Q_PER_KV):
            rows = slice(g * BLOCK, (g + 1) * BLOCK)
            sink = sinks_ref[h * Q_PER_KV + g]
            s_b = jnp.where(ok_band, jnp.where(is_cur, s_c4[rows], s_p4[rows]) * scale, NEG_INF)
            s_m = jnp.where(ok_meta, s_m4[rows] * scale, NEG_INF)
            mx = jnp.maximum(jnp.maximum(jnp.max(s_b, axis=-1, keepdims=True),
                                         jnp.max(s_m, axis=-1, keepdims=True)), sink)
            p_b = jnp.exp(s_b - mx)
            p_m = jnp.exp(s_m - mx)
            dens.append(jnp.sum(p_b, axis=-1, keepdims=True) + jnp.sum(p_m, axis=-1, keepdims=True)
                        + jnp.exp(sink - mx))
            pc.append(jnp.where(is_cur, p_b, 0.0).astype(BF16))
            pp.append(jnp.where(is_cur, 0.0, p_b).astype(BF16))
            pm.append(p_m.astype(BF16))
        o4 = (jnp.dot(jnp.concatenate(pc, axis=0), vc, preferred_element_type=F32)
              + jnp.dot(jnp.concatenate(pp, axis=0), vp, preferred_element_type=F32)
              + jnp.dot(jnp.concatenate(pm, axis=0), vm, preferred_element_type=F32))
        for g in range(Q_PER_KV):
            hq = h * Q_PER_KV + g
            o_ref[:, hq * HD:(hq + 1) * HD] = (o4[g * BLOCK:(g + 1) * BLOCK] / dens[g]).astype(o_ref.dtype)


def _attention(main, sinks, bsz, nb):
    m = main.shape[0]
    kb, vb = K_COL // KV_DIM, V_COL // KV_DIM
    cur = lambda bb, i: bb * nb + i
    prev = lambda bb, i: bb * nb + jnp.maximum(i - 1, 0)
    first = lambda bb, i: bb * nb
    return pl.pallas_call(
        _attn_kernel,
        out_shape=jax.ShapeDtypeStruct((m, Q_DIM), BF16),
        grid=(bsz, nb),
        in_specs=[
            pl.BlockSpec(memory_space=pltpu.SMEM),
            pl.BlockSpec((BLOCK, Q_DIM), lambda bb, i: (cur(bb, i), Q_COL // Q_DIM)),
            pl.BlockSpec((BLOCK, KV_DIM), lambda bb, i: (cur(bb, i), kb)),
            pl.BlockSpec((BLOCK, KV_DIM), lambda bb, i: (cur(bb, i), vb)),
            pl.BlockSpec((BLOCK, KV_DIM), lambda bb, i: (prev(bb, i), kb)),
            pl.BlockSpec((BLOCK, KV_DIM), lambda bb, i: (prev(bb, i), vb)),
            pl.BlockSpec((BLOCK, KV_DIM), lambda bb, i: (first(bb, i), kb)),
            pl.BlockSpec((BLOCK, KV_DIM), lambda bb, i: (first(bb, i), vb)),
        ],
        out_specs=pl.BlockSpec((BLOCK, Q_DIM), lambda bb, i: (cur(bb, i), 0)),
        compiler_params=_cparams(("parallel", "parallel")),
        name="swa_attention",
    )(sinks, main, main, main, main, main, main, main)


CIN_ROWS = BLOCK + 8


def _ssd_kernel(hb_ref, wdt_ref, z_ref, xs_ref, bm_ref, cm_ref,
                cw_ref, cbias_ref, dtb_ref, a_ref, dsk_ref, ng_ref,
                y_ref,
                cin_ref, xc_ref, state_ref, yacc_ref, acs_ref, acst_ref, dtt_ref, wt_ref, ea_ref):
    i = pl.program_id(1)
    row = lax.broadcasted_iota(jnp.int32, (BLOCK, 1), 0)
    live = jnp.where((i > 0) | (row >= PAD), 1.0, 0.0)

    @pl.when(i == 0)
    def _():
        state_ref[...] = jnp.zeros_like(state_ref)
        cin_ref[0:8, :] = jnp.zeros((8, CONV_DIM), F32)

    @pl.when(i > 0)
    def _():
        cin_ref[0:8, :] = cin_ref[BLOCK:BLOCK + 8, :]

    cin_ref[8:CIN_ROWS, 0:D_INNER] = xs_ref[...].astype(F32)
    cin_ref[8:CIN_ROWS, D_INNER:D_INNER + 512] = bm_ref[...].astype(F32)
    cin_ref[8:CIN_ROWS, D_INNER + 512:CONV_DIM] = cm_ref[...].astype(F32)

    for cc in range(CONV_DIM // COL_TILE):
        cs = slice(cc * COL_TILE, (cc + 1) * COL_TILE)
        acc = jnp.broadcast_to(cbias_ref[:, cs], (BLOCK, COL_TILE))
        for j in range(CONV_W):
            acc = acc + cw_ref[j:j + 1, cs] * cin_ref[8 - (CONV_W - 1) + j:8 - (CONV_W - 1) + j + BLOCK, cs]
        xc_ref[:, cs] = _silu(acc) * live

    dt_raw = lax.dot_general(hb_ref[...], wdt_ref[...], _NT, preferred_element_type=F32) + dtb_ref[...]
    dt = (jnp.maximum(dt_raw, 0.0) + jnp.log1p(jnp.exp(-jnp.abs(dt_raw)))) * live
    adt = dt * a_ref[...]
    rr = lax.broadcasted_iota(jnp.int32, (BLOCK, BLOCK), 0)
    cl = lax.broadcasted_iota(jnp.int32, (BLOCK, BLOCK), 1)
    causal = rr >= cl
    a_cs = jnp.dot(jnp.where(causal, 1.0, 0.0), adt, precision=HIGHEST, preferred_element_type=F32)
    a_cs_t = a_cs.T
    dt_t = dt.T
    last = a_cs_t[:, BLOCK - 1:BLOCK]
    acs_ref[...] = a_cs
    acst_ref[...] = a_cs_t
    dtt_ref[...] = dt_t
    wt_ref[...] = dt_t * jnp.exp(last - a_cs_t)
    ea_ref[...] = jnp.exp(a_cs)
    cd_t = jnp.exp(last)

    lane = lax.broadcasted_iota(jnp.int32, (1, LANES), 1)
    lo = lane < SSD_HEADDIM
    nn = (((1,), (1,)), ((), ()))
    for g in range(SSD_GROUPS):
        bg = xc_ref[:, D_INNER + g * D_STATE:D_INNER + (g + 1) * D_STATE]
        cg = xc_ref[:, D_INNER + 512 + g * D_STATE:D_INNER + 512 + (g + 1) * D_STATE]
        cb = lax.dot_general(cg.astype(BF16), bg.astype(BF16), nn, preferred_element_type=F32)
        bt = bg.T
        for pp in range(HEADS_PER_GROUP // 2):
            pr = g * (HEADS_PER_GROUP // 2) + pp
            ps = slice(pr * LANES, (pr + 1) * LANES)
            xs_pair = xc_ref[:, ps]
            st_pair = state_ref[:, ps]
            lhs, lhs_s = [], []
            for hd in (2 * pr, 2 * pr + 1):
                col = jnp.broadcast_to(acs_ref[:, hd:hd + 1], (BLOCK, BLOCK))
                seg = col - acst_ref[hd:hd + 1, :]
                dec = jnp.exp(jnp.where(causal, seg, NEG_INF))
                lhs.append((cb * dec * dtt_ref[hd:hd + 1, :]).astype(BF16))
                lhs.append((cg * jnp.broadcast_to(ea_ref[:, hd:hd + 1], (BLOCK, BLOCK))).astype(BF16))
                lhs_s.append((bt * wt_ref[hd:hd + 1, :]).astype(BF16))
            xs_lo = jnp.where(lo, xs_pair, 0.0).astype(BF16)
            xs_hi = jnp.where(lo, 0.0, xs_pair).astype(BF16)
            st_lo = jnp.where(lo, st_pair, 0.0).astype(BF16)
            st_hi = jnp.where(lo, 0.0, st_pair).astype(BF16)
            y_pair = jnp.dot(jnp.concatenate(lhs, axis=1),
                             jnp.concatenate([xs_lo, st_lo, xs_hi, st_hi], axis=0),
                             preferred_element_type=F32)
            yacc_ref[:, ps] = y_pair + dsk_ref[:, ps] * xs_pair
            st_new = jnp.dot(jnp.concatenate(lhs_s, axis=1),
                             jnp.concatenate([xs_lo, xs_hi], axis=0), preferred_element_type=F32)
            cd_row = jnp.where(lo, jnp.broadcast_to(cd_t[2 * pr:2 * pr + 1, :], (1, LANES)),
                               jnp.broadcast_to(cd_t[2 * pr + 1:2 * pr + 2, :], (1, LANES)))
            state_ref[:, ps] = st_pair * cd_row + st_new

    gw = D_INNER // SSD_GROUPS
    for g in range(SSD_GROUPS):
        gs = slice(g * gw, (g + 1) * gw)
        yz = yacc_ref[:, gs] * _silu(z_ref[:, gs].astype(F32))
        ms = jnp.mean(yz * yz, axis=-1, keepdims=True)
        y_ref[:, gs] = (yz * lax.rsqrt(ms + RMS_EPS) * ng_ref[:, gs]).astype(y_ref.dtype)


def _ssd(hb, main, w_dt, conv_w, conv_b, dt_bias, a_neg, d_skip_x, norm_g, bsz, nb):
    m = hb.shape[0]
    cur = lambda bb, i: bb * nb + i
    full = lambda shape: pl.BlockSpec(shape, lambda bb, i: (0, 0))
    return pl.pallas_call(
        _ssd_kernel,
        out_shape=jax.ShapeDtypeStruct((m, D_INNER), BF16),
        grid=(bsz, nb),
        in_specs=[
            pl.BlockSpec((BLOCK, D_MODEL), lambda bb, i: (cur(bb, i), 0)),
            full((LANES, D_MODEL)),
            pl.BlockSpec((BLOCK, D_INNER), lambda bb, i: (cur(bb, i), Z_COL // D_INNER)),
            pl.BlockSpec((BLOCK, D_INNER), lambda bb, i: (cur(bb, i), XS_COL // D_INNER)),
            pl.BlockSpec((BLOCK, 512), lambda bb, i: (cur(bb, i), B_COL // 512)),
            pl.BlockSpec((BLOCK, 512), lambda bb, i: (cur(bb, i), C_COL // 512)),
            full((CONV_W, CONV_DIM)), full((1, CONV_DIM)), full((1, LANES)), full((1, LANES)),
            full((1, D_INNER)), full((1, D_INNER)),
        ],
        out_specs=pl.BlockSpec((BLOCK, D_INNER), lambda bb, i: (cur(bb, i), 0)),
        scratch_shapes=[
            pltpu.VMEM((CIN_ROWS, CONV_DIM), F32),
            pltpu.VMEM((BLOCK, CONV_DIM), F32),
            pltpu.VMEM((D_STATE, D_INNER), F32),
            pltpu.VMEM((BLOCK, D_INNER), F32),
            pltpu.VMEM((BLOCK, LANES), F32),
            pltpu.VMEM((LANES, BLOCK), F32),
            pltpu.VMEM((LANES, BLOCK), F32),
            pltpu.VMEM((LANES, BLOCK), F32),
            pltpu.VMEM((BLOCK, LANES), F32),
        ],
        compiler_params=_cparams(("arbitrary", "arbitrary")),
        name="ssd",
    )(hb, w_dt, main, main, main, main, conv_w, conv_b, dt_bias, a_neg, d_skip_x, norm_g)


def _merge_kernel(hb_ref, at_ref, y_ref, wga_ref, wgs_ref, wa_ref, ws_ref, o_ref,
                  wga_b, wgs_b, wa_b, ws_b):
    @pl.when(pl.program_id(1) == 0)
    def _():
        wga_b[...] = wga_ref[...].astype(BF16)
        wgs_b[...] = wgs_ref[...].astype(BF16)
        wa_b[...] = wa_ref[...].astype(BF16)
        ws_b[...] = ws_ref[...].astype(BF16)

    hb = hb_ref[...]
    ga = lax.dot_general(hb, wga_b[...], _NT, preferred_element_type=F32)
    gs = lax.dot_general(hb, wgs_b[...], _NT, preferred_element_type=F32)
    pa = jnp.dot(at_ref[...], wa_b[...], preferred_element_type=F32)
    ps = jnp.dot(y_ref[...], ws_b[...], preferred_element_type=F32)
    o_ref[...] = (_sigmoid(ga) * pa + _sigmoid(gs) * ps).astype(o_ref.dtype)


def _merge(hb, attn, y, w_gates, w_a, w_s):
    m = hb.shape[0]
    tm = _row_tile(m // BLOCK, 5)
    nt = D_MODEL // COL_TILE
    once = pl.Buffered(1)
    return pl.pallas_call(
        _merge_kernel,
        out_shape=jax.ShapeDtypeStruct((m, D_MODEL), BF16),
        grid=(nt, m // tm),
        in_specs=[
            pl.BlockSpec((tm, D_MODEL), lambda j, i: (i, 0)),
            pl.BlockSpec((tm, Q_DIM), lambda j, i: (i, 0)),
            pl.BlockSpec((tm, D_INNER), lambda j, i: (i, 0)),
            pl.BlockSpec((COL_TILE, D_MODEL), lambda j, i: (j, 0), pipeline_mode=once),
            pl.BlockSpec((COL_TILE, D_MODEL), lambda j, i: (j + nt, 0), pipeline_mode=once),
            pl.BlockSpec((Q_DIM, COL_TILE), lambda j, i: (0, j), pipeline_mode=once),
            pl.BlockSpec((D_INNER, COL_TILE), lambda j, i: (0, j), pipeline_mode=once),
        ],
        out_specs=pl.BlockSpec((tm, COL_TILE), lambda j, i: (i, j)),
        scratch_shapes=[
            pltpu.VMEM((COL_TILE, D_MODEL), BF16), pltpu.VMEM((COL_TILE, D_MODEL), BF16),
            pltpu.VMEM((Q_DIM, COL_TILE), BF16), pltpu.VMEM((D_INNER, COL_TILE), BF16),
        ],
        compiler_params=_cparams(("arbitrary", "arbitrary")),
        name="branch_merge",
    )(hb, attn, y, w_gates, w_gates, w_a, w_s)


def _outproj_router_kernel(mg_ref, h_ref, wo_ref, g_ref, b_ref, wrh_ref, wrl_ref, br_ref,
                           o_ref, ids_ref, wts_ref, cnt_ref, carry_ref, *, nb, tm):
    step = pl.program_id(0)

    @pl.when(step == 0)
    def _():
        carry_ref[...] = jnp.zeros_like(carry_ref)

    mix = jnp.dot(mg_ref[...], wo_ref[...], preferred_element_type=F32)
    h1 = _ln_rows(ALPHA * h_ref[...] + mix, g_ref[...], b_ref[...])
    o_ref[...] = h1

    h_hi = h1.astype(BF16)
    h_lo = (h1 - h_hi.astype(F32)).astype(BF16)
    logits = (lax.dot_general(h_hi, wrh_ref[...], _NT, preferred_element_type=F32)
              + (lax.dot_general(h_lo, wrh_ref[...], _NT, preferred_element_type=F32)
                 + lax.dot_general(h_hi, wrl_ref[...], _NT, preferred_element_type=F32))) + br_ref[...]
    lane = lax.broadcasted_iota(jnp.int32, (tm, LANES), 1)
    row = lax.broadcasted_iota(jnp.int32, (BLOCK, 1), 0)
    blocks = tm // BLOCK
    real = jnp.concatenate(
        [(((step * blocks + s) % nb) > 0) | (row >= PAD) for s in range(blocks)], axis=0)
    big = jnp.int32(LANES)

    is_g = lane < MOE_GROUPS
    gl = jnp.where(is_g, logits, -jnp.inf)
    ge = jnp.exp(gl - jnp.max(gl, axis=-1, keepdims=True))
    pg = ge / jnp.sum(ge, axis=-1, keepdims=True)
    p_top = jnp.max(pg, axis=-1, keepdims=True)
    g_idx = jnp.min(jnp.where(is_g & (pg == p_top), lane, big), axis=-1, keepdims=True)

    base = MOE_GROUPS + g_idx * EXPERTS_PER_GROUP
    sel = (lane >= base) & (lane < base + EXPERTS_PER_GROUP)
    el = jnp.where(sel, logits, -jnp.inf)
    ee = jnp.exp(el - jnp.max(el, axis=-1, keepdims=True))
    pe = ee / jnp.sum(ee, axis=-1, keepdims=True)
    v1 = jnp.max(jnp.where(sel, pe, -1.0), axis=-1, keepdims=True)
    i1 = jnp.min(jnp.where(sel & (pe == v1), lane, big), axis=-1, keepdims=True)
    sel2 = sel & (lane != i1)
    v2 = jnp.max(jnp.where(sel2, pe, -1.0), axis=-1, keepdims=True)
    i2 = jnp.min(jnp.where(sel2 & (pe == v2), lane, big), axis=-1, keepdims=True)
    vs = v1 + v2
    w1 = p_top * (v1 / vs)
    w2 = p_top * (v2 / vs)

    hit1 = (lane == i1) & real
    hit2 = (lane == i2) & real
    onehot = jnp.where(hit1 | hit2, 1.0, 0.0)
    rr = lax.broadcasted_iota(jnp.int32, (tm, tm), 0)
    cl = lax.broadcasted_iota(jnp.int32, (tm, tm), 1)
    before = jnp.where(rr > cl, 1.0, 0.0).astype(BF16)
    cum = jnp.dot(before, onehot.astype(BF16), preferred_element_type=F32) + carry_ref[...]
    r1 = jnp.sum(jnp.where(hit1, cum, 0.0), axis=-1, keepdims=True)
    r2 = jnp.sum(jnp.where(hit2, cum, 0.0), axis=-1, keepdims=True)
    carry_ref[...] = carry_ref[...] + jnp.sum(onehot, axis=0, keepdims=True)
    cnt_ref[...] = carry_ref[...].astype(jnp.int32)

    e1 = jnp.where(real, i1 - MOE_GROUPS, -1)
    e2 = jnp.where(real, i2 - MOE_GROUPS, -1)
    ids = jnp.where(lane == 0, e1, jnp.where(lane == 1, e2, jnp.where(
        lane == 2, r1.astype(jnp.int32), jnp.where(lane == 3, r2.astype(jnp.int32), 0))))
    ids_ref[...] = ids.T[0:8, :]
    l8 = lax.broadcasted_iota(jnp.int32, (tm, 8), 1)
    wts_ref[...] = jnp.where(l8 == 0, w1, jnp.where(l8 == 1, w2, 0.0))


def _outproj_router(merged, h0, w_o, g, b, wr_hi, wr_lo, b_r, nb):
    m = h0.shape[0]
    tm = _row_tile(m // BLOCK, 5)
    row_blk = lambda width: pl.BlockSpec((tm, width), lambda i: (i, 0))
    full = lambda shape: pl.BlockSpec(shape, lambda i: (0, 0))
    return pl.pallas_call(
        functools.partial(_outproj_router_kernel, nb=nb, tm=tm),
        out_shape=(jax.ShapeDtypeStruct((m, D_MODEL), F32),
                   jax.ShapeDtypeStruct((8, m), jnp.int32), jax.ShapeDtypeStruct((m, 8), F32),
                   jax.ShapeDtypeStruct((1, LANES), jnp.int32)),
        grid=(m // tm,),
        in_specs=[row_blk(D_MODEL), row_blk(D_MODEL), full((D_MODEL, D_MODEL)),
                  full((1, D_MODEL)), full((1, D_MODEL)),
                  full((LANES, D_MODEL)), full((LANES, D_MODEL)), full((1, LANES))],
        out_specs=(row_blk(D_MODEL), pl.BlockSpec((8, tm), lambda i: (0, i)), row_blk(8),
                   full((1, LANES))),
        scratch_shapes=[pltpu.VMEM((1, LANES), F32)],
        compiler_params=_cparams(("arbitrary",)),
        name="out_proj_ln1_router",
    )(merged, h0, w_o, g, b, wr_hi, wr_lo, b_r)


HALF = D_MODEL // 2
HI_MASK = 0xFFFF0000


def _pack_rows(x):
    lo = lax.bitcast_convert_type(x[:, :HALF].astype(BF16).astype(F32), jnp.uint32) >> 16
    hi = lax.bitcast_convert_type(x[:, HALF:].astype(BF16).astype(F32), jnp.uint32) & jnp.uint32(HI_MASK)
    return hi | lo


def _unpack_rows(w):
    lo = lax.bitcast_convert_type(w << 16, F32)
    hi = lax.bitcast_convert_type(w & jnp.uint32(HI_MASK), F32)
    return lo, hi


def _dest_kernel(pstart_ref, ids_ref, dest_ref, *, n_slots):
    e = ids_ref[0:TOP_K, :]
    acc = jnp.zeros_like(e)
    for ex in range(N_EXPERTS):
        acc = jnp.where(e == ex, pstart_ref[ex], acc)
    k = lax.broadcasted_iota(jnp.int32, e.shape, 0)
    t = lax.broadcasted_iota(jnp.int32, e.shape, 1)
    dump = n_slots + k * PAD + (t & (BLOCK - 1))
    dest_ref[...] = jnp.where(e >= 0, acc + ids_ref[TOP_K:2 * TOP_K, :], dump)


def _dest_rows(pstart, ids, n_slots):
    m = ids.shape[1]
    return pl.pallas_call(
        functools.partial(_dest_kernel, n_slots=n_slots),
        out_shape=jax.ShapeDtypeStruct((TOP_K, m), jnp.int32),
        in_specs=[pl.BlockSpec(memory_space=pltpu.SMEM), pl.BlockSpec(memory_space=pltpu.VMEM)],
        out_specs=pl.BlockSpec(memory_space=pltpu.VMEM),
        name="moe_dest",
    )(pstart, ids)


def _dispatch_kernel(dest_ref, zlo_ref, zhi_ref, nused_ref, h_ref, xs_hbm, ring, zrow, zblk, sems, zsem,
                     *, n_blocks):
    step = pl.program_id(0)
    n_steps = pl.num_programs(0)
    n_rows = n_steps * BLOCK
    slot = step % 2

    def zero_rows(start):
        def per_expert(ex, carry):
            def per_row(p, c):
                cp = pltpu.make_async_copy(zrow, xs_hbm.at[pl.ds(p, 1), :], zsem)
                cp.start() if start else cp.wait()
                return c
            return lax.fori_loop(zlo_ref[ex], zhi_ref[ex], per_row, carry)
        lax.fori_loop(0, N_EXPERTS, per_expert, 0)

        def per_block(blk, c):
            cp = pltpu.make_async_copy(zblk, xs_hbm.at[pl.ds(blk * MOE_ROWS, MOE_ROWS), :], zsem)
            cp.start() if start else cp.wait()
            return c
        lax.fori_loop(nused_ref[0], n_blocks, per_block, 0)

    @pl.when(step == 0)
    def _():
        zrow[...] = jnp.zeros_like(zrow)
        zblk[...] = jnp.zeros_like(zblk)
        zero_rows(True)

    ring[slot] = _pack_rows(h_ref[...])
    base = step * BLOCK

    def body(t, carry):
        for k in range(TOP_K):
            d = dest_ref[k * n_rows + base + t]
            pltpu.make_async_copy(ring.at[slot, pl.ds(t, 1), :], xs_hbm.at[pl.ds(d, 1), :],
                                  sems.at[slot]).start()
        return carry
    lax.fori_loop(0, BLOCK, body, 0, unroll=4)

    def wait_slot(s):
        for k in range(TOP_K):
            pltpu.make_async_copy(ring.at[s], xs_hbm.at[pl.ds(0, BLOCK), :], sems.at[s]).wait()

    @pl.when(step > 0)
    def _():
        wait_slot(1 - slot)

    @pl.when(step == n_steps - 1)
    def _():
        wait_slot(slot)
        zero_rows(False)


def _dispatch(h1, dest, zlo, zhi, n_used, n_blocks):
    m = h1.shape[0]
    n_slots = n_blocks * MOE_ROWS
    grid_spec = pltpu.PrefetchScalarGridSpec(
        num_scalar_prefetch=4,
        grid=(m // BLOCK,),
        in_specs=[pl.BlockSpec((BLOCK, D_MODEL), lambda i, *_: (i, 0))],
        out_specs=pl.BlockSpec(memory_space=pl.ANY),
        scratch_shapes=[
            pltpu.VMEM((2, BLOCK, HALF), jnp.uint32),
            pltpu.VMEM((1, HALF), jnp.uint32),
            pltpu.VMEM((MOE_ROWS, HALF), jnp.uint32),
            pltpu.SemaphoreType.DMA((2,)),
            pltpu.SemaphoreType.DMA(()),
        ],
    )
    return pl.pallas_call(
        functools.partial(_dispatch_kernel, n_blocks=n_blocks),
        out_shape=jax.ShapeDtypeStruct((n_slots + TOP_K * PAD, HALF), jnp.uint32),
        grid_spec=grid_spec,
        compiler_params=_cparams(("arbitrary",)),
        name="moe_dispatch",
    )(dest, zlo, zhi, n_used, h1)


def _expert_kernel(blk_e_ref, nused_ref, xs_ref, wg_ref, wu_ref, wd_ref, o_ref, wgb, wub, wdb):
    i = pl.program_id(0)
    n_used = nused_ref[0]

    @pl.when(i < n_used)
    def _():
        fresh = (i == 0) | (blk_e_ref[i] != blk_e_ref[jnp.maximum(i - 1, 0)])

        @pl.when(fresh)
        def _():
            wgb[...] = wg_ref[...].astype(BF16)
            wub[...] = wu_ref[...].astype(BF16)
            wdb[...] = wd_ref[...].astype(BF16)

        lo, hi = _unpack_rows(xs_ref[...])
        x_lo, x_hi = lo.astype(BF16), hi.astype(BF16)
        gate = (jnp.dot(x_lo, wgb[0:HALF, :], preferred_element_type=F32)
                + jnp.dot(x_hi, wgb[HALF:D_MODEL, :], preferred_element_type=F32))
        up = (jnp.dot(x_lo, wub[0:HALF, :], preferred_element_type=F32)
              + jnp.dot(x_hi, wub[HALF:D_MODEL, :], preferred_element_type=F32))
        hid = (_silu(gate) * up).astype(BF16)
        o_ref[...] = _pack_rows(jnp.dot(hid, wdb[...], preferred_element_type=F32))

    @pl.when(i >= n_used)
    def _():
        o_ref[...] = jnp.zeros_like(o_ref)


def _experts(xs, w_gate, w_up, w_down, blk_e, n_used, n_blocks):
    used = lambda i, nu: jnp.minimum(i, nu[0] - 1)
    grid_spec = pltpu.PrefetchScalarGridSpec(
        num_scalar_prefetch=2,
        grid=(n_blocks,),
        in_specs=[
            pl.BlockSpec((MOE_ROWS, HALF), lambda i, be, nu: (used(i, nu), 0)),
            pl.BlockSpec((None, None, D_MODEL, D_EXPERT), lambda i, be, nu: (0, be[i], 0, 0)),
            pl.BlockSpec((None, None, D_MODEL, D_EXPERT), lambda i, be, nu: (0, be[i], 0, 0)),
            pl.BlockSpec((None, None, D_EXPERT, D_MODEL), lambda i, be, nu: (0, be[i], 0, 0)),
        ],
        out_specs=pl.BlockSpec((MOE_ROWS, HALF), lambda i, be, nu: (i, 0)),
        scratch_shapes=[
            pltpu.VMEM((D_MODEL, D_EXPERT), BF16),
            pltpu.VMEM((D_MODEL, D_EXPERT), BF16),
            pltpu.VMEM((D_EXPERT, D_MODEL), BF16),
        ],
    )
    return pl.pallas_call(
        _expert_kernel,
        out_shape=jax.ShapeDtypeStruct((n_blocks * MOE_ROWS, HALF), jnp.uint32),
        grid_spec=grid_spec,
        compiler_params=_cparams(("arbitrary",)),
        name="moe_experts",
    )(blk_e, n_used, xs, w_gate, w_up, w_down)


def _combine_kernel(dest_ref, h_ref, wts_ref, yb_hbm, g_ref, b_ref, o_ref, ybuf, sems, *, nb):
    bb = pl.program_id(0)
    i = pl.program_id(1)
    n_i = pl.num_programs(1)
    step = bb * n_i + i
    n_steps = pl.num_programs(0) * n_i

    n_rows = pl.num_programs(0) * nb * BLOCK

    def assign_base(b_, i_):
        return (b_ * nb + i_ + 1) * BLOCK

    def start_gather(base, slot):
        def body(t, carry):
            for k in range(TOP_K):
                d = dest_ref[k * n_rows + base + t]
                pltpu.make_async_copy(yb_hbm.at[pl.ds(d, 1), :], ybuf.at[slot, k, pl.ds(t, 1), :],
                                      sems.at[slot]).start()
            return carry
        lax.fori_loop(0, BLOCK, body, 0, unroll=4)

    def wait_gather(slot):
        for k in range(TOP_K):
            pltpu.make_async_copy(yb_hbm.at[pl.ds(0, BLOCK), :], ybuf.at[slot, k], sems.at[slot]).wait()

    @pl.when(step == 0)
    def _():
        start_gather(assign_base(0, 0), 0)

    @pl.when(step + 1 < n_steps)
    def _():
        nxt = jnp.where(i + 1 < n_i, assign_base(bb, i + 1), assign_base(bb + 1, 0))
        start_gather(nxt, (step + 1) % 2)

    slot = step % 2
    wait_gather(slot)
    wts = wts_ref[...]
    lo0, hi0 = _unpack_rows(ybuf[slot, 0])
    lo1, hi1 = _unpack_rows(ybuf[slot, 1])
    w0, w1 = wts[:, 0:1], wts[:, 1:2]
    r_lo = ALPHA * h_ref[:, 0:HALF] + (lo0 * w0 + lo1 * w1)
    r_hi = ALPHA * h_ref[:, HALF:D_MODEL] + (hi0 * w0 + hi1 * w1)
    mu = (jnp.sum(r_lo, axis=-1, keepdims=True) + jnp.sum(r_hi, axis=-1, keepdims=True)) / D_MODEL
    c_lo, c_hi = r_lo - mu, r_hi - mu
    var = (jnp.sum(c_lo * c_lo, axis=-1, keepdims=True)
           + jnp.sum(c_hi * c_hi, axis=-1, keepdims=True)) / D_MODEL
    inv = lax.rsqrt(var + LN_EPS)
    o_ref[:, 0:HALF] = c_lo * inv * g_ref[:, 0:HALF] + b_ref[:, 0:HALF]
    o_ref[:, HALF:D_MODEL] = c_hi * inv * g_ref[:, HALF:D_MODEL] + b_ref[:, HALF:D_MODEL]


def _combine_ln(h1, wts, yb, dest, g, b, bsz, nb):
    grid_spec = pltpu.PrefetchScalarGridSpec(
        num_scalar_prefetch=1,
        grid=(bsz, nb - 1),
        in_specs=[
            pl.BlockSpec((BLOCK, D_MODEL), lambda bb, i, d: (bb * nb + i + 1, 0)),
            pl.BlockSpec((BLOCK, 8), lambda bb, i, d: (bb * nb + i + 1, 0)),
            pl.BlockSpec(memory_space=pl.ANY),
            pl.BlockSpec((1, D_MODEL), lambda bb, i, d: (0, 0)),
            pl.BlockSpec((1, D_MODEL), lambda bb, i, d: (0, 0)),
        ],
        out_specs=pl.BlockSpec((None, BLOCK, D_MODEL), lambda bb, i, d: (bb, i, 0)),
        scratch_shapes=[
            pltpu.VMEM((2, TOP_K, BLOCK, HALF), jnp.uint32),
            pltpu.SemaphoreType.DMA((2,)),
        ],
    )
    return pl.pallas_call(
        functools.partial(_combine_kernel, nb=nb),
        out_shape=jax.ShapeDtypeStruct((bsz, (nb - 1) * BLOCK, D_MODEL), F32),
        grid_spec=grid_spec,
        compiler_params=_cparams(("arbitrary", "arbitrary")),
        name="moe_combine_ln2",
    )(dest, h1, wts, yb, g, b)


def _dispatch_tables(ids, counts, n_blocks):
    counts = counts[0, MOE_GROUPS:MOE_GROUPS + N_EXPERTS]
    pcounts = (counts + MOE_ROWS - 1) // MOE_ROWS * MOE_ROWS
    pend = jnp.cumsum(pcounts)
    pstart = (pend - pcounts).astype(jnp.int32)
    n_used = jnp.maximum(pend[-1] // MOE_ROWS, 1).astype(jnp.int32)
    blk = jnp.arange(n_blocks, dtype=jnp.int32)
    blk_e = jnp.minimum(jnp.sum(pend[None, :] <= (blk * MOE_ROWS)[:, None], axis=1), N_EXPERTS - 1)
    blk_e = jnp.where(blk < n_used, blk_e, blk_e[n_used - 1]).astype(jnp.int32)
    dest = _dest_rows(pstart, ids, n_blocks * MOE_ROWS).reshape(-1)
    return dest, pstart, (pstart + counts).astype(jnp.int32), pend.astype(jnp.int32), blk_e, n_used.reshape(1)


def kernel(x, meta_tokens, ln_emb_g, ln_emb_b, w_in, conv_w, conv_b, dt_bias, a_log, d_skip, ssd_norm_g, sinks, w_br_attn, w_br_ssd, w_o, ln1_g, ln1_b, w_router_group, b_router_group, w_router_expert, b_router_expert, w_gate, w_up, w_down, ln2_g, ln2_b):
    bsz, seq, d = x.shape
    assert d == D_MODEL and seq % BLOCK == 0 and w_in.shape[0] == DEPTH
    nb = seq // BLOCK + 1
    m = bsz * nb * BLOCK
    row2 = lambda v: v.reshape(1, -1).astype(F32)

    meta_pad = jnp.concatenate([jnp.zeros((PAD, d), F32), meta_tokens.astype(F32)], axis=0)
    h0, h0b = _embed_ln(x, meta_pad, row2(ln_emb_g), row2(ln_emb_b))

    w_in_t = jnp.swapaxes(w_in, 1, 2)
    main = _inproj(h0b, w_in_t)
    attn = _attention(main, sinks[0].astype(F32), bsz, nb)

    w_dt = jnp.pad(w_in_t[0, DT_OFF:DT_OFF + SSD_HEADS], ((0, LANES - SSD_HEADS), (0, 0))).astype(BF16)
    pad_h = lambda v: jnp.pad(v.astype(F32), (0, LANES - SSD_HEADS)).reshape(1, LANES)
    y = _ssd(h0b, main, w_dt, conv_w[0].astype(F32), row2(conv_b[0]), pad_h(dt_bias[0]),
             pad_h(-jnp.exp(a_log[0].astype(F32))), row2(jnp.repeat(d_skip[0], SSD_HEADDIM)),
             row2(ssd_norm_g[0]), bsz, nb)

    merged = _merge(h0b, attn, y, w_in_t[0, GATE_OFF:], w_br_attn[0], w_br_ssd[0])
    w_r = jnp.pad(jnp.concatenate([w_router_group[0].T, w_router_expert[0].T], axis=0).astype(F32),
                  ((0, LANES - MOE_GROUPS - N_EXPERTS), (0, 0)))
    wr_hi = w_r.astype(BF16)
    wr_lo = (w_r - wr_hi.astype(F32)).astype(BF16)
    b_r = jnp.pad(jnp.concatenate([b_router_group[0], b_router_expert[0]]).astype(F32),
                  (0, LANES - MOE_GROUPS - N_EXPERTS)).reshape(1, LANES)
    h1, ids, wts, counts = _outproj_router(merged, h0, w_o[0].astype(BF16), row2(ln1_g[0]),
                                           row2(ln1_b[0]), wr_hi, wr_lo, b_r, nb)

    n_assign = bsz * (seq + N_META) * TOP_K
    n_blocks = -(-n_assign // MOE_ROWS) + N_EXPERTS
    dest, _, zlo, zhi, blk_e, n_used = _dispatch_tables(ids, counts, n_blocks)
    xs = _dispatch(h1, dest, zlo, zhi, n_used, n_blocks)
    yb = _experts(xs, w_gate, w_up, w_down, blk_e, n_used, n_blocks)
    return _combine_ln(h1, wts, yb, dest, row2(ln2_g[0]), row2(ln2_b[0]), bsz, nb)
```

```python
import functools

import jax
import jax.numpy as jnp
from jax import lax
from jax.experimental import pallas as pl
from jax.experimental.pallas import tpu as pltpu

F32 = jnp.float32
BF16 = jnp.bfloat16
HIGHEST = lax.Precision.HIGHEST

D_MODEL = 2048
N_META = 16
BLOCK = 128
PAD = BLOCK - N_META
WINDOW = 128
HQ, HKV, HD = 16, 4, 64
Q_PER_KV = HQ // HKV
D_INNER = 2048
SSD_HEADDIM = 64
SSD_HEADS = D_INNER // SSD_HEADDIM
SSD_GROUPS = 4
HEADS_PER_GROUP = SSD_HEADS // SSD_GROUPS
D_STATE = 128
CONV_W = 4
CONV_DIM = D_INNER + 2 * SSD_GROUPS * D_STATE
MOE_GROUPS = 8
EXPERTS_PER_GROUP = 8
N_EXPERTS = MOE_GROUPS * EXPERTS_PER_GROUP
TOP_K = 2
D_EXPERT = 512
LN_EPS = 1e-5
RMS_EPS = 1e-5
NEG_INF = -1e30
DEPTH = 1
ALPHA = (2.0 * DEPTH) ** 0.25

Q_DIM = HQ * HD
KV_DIM = HKV * HD
MAIN_DIM = Q_DIM + 2 * KV_DIM + D_INNER + CONV_DIM
DT_OFF = MAIN_DIM
GATE_OFF = MAIN_DIM + SSD_HEADS
COL_TILE = 512
MAIN_TILES = MAIN_DIM // COL_TILE
Z_COL, XS_COL, Q_COL, K_COL, V_COL, B_COL, C_COL = 0, 2048, 4096, 5120, 5376, 5632, 6144

LANES = 128
MOE_ROWS = 576
VMEM_LIMIT = 56 * 1024 * 1024


def _cparams(sem, vmem=VMEM_LIMIT):
    return pltpu.CompilerParams(dimension_semantics=sem, vmem_limit_bytes=vmem)


def _ln_rows(x, g, b):
    mu = jnp.mean(x, axis=-1, keepdims=True)
    xc = x - mu
    var = jnp.mean(xc * xc, axis=-1, keepdims=True)
    return xc * lax.rsqrt(var + LN_EPS) * g + b


def _sigmoid(x):
    return 1.0 / (1.0 + jnp.exp(-x))


def _silu(x):
    return x * _sigmoid(x)


def _row_tile(n_blocks, max_blocks):
    best = 1
    for c in range(1, max_blocks + 1):
        if n_blocks % c == 0:
            best = c
    return best * BLOCK


def _embed_ln_kernel(x_ref, meta_ref, g_ref, b_ref, h_ref, hb_ref):
    i = pl.program_id(1)
    src = jnp.where(i == 0, meta_ref[...], x_ref[...])
    y = _ln_rows(src, g_ref[...], b_ref[...])
    row = lax.broadcasted_iota(jnp.int32, (BLOCK, 1), 0)
    y = jnp.where((i > 0) | (row >= PAD), y, 0.0)
    h_ref[...] = y
    hb_ref[...] = y.astype(BF16)


def _embed_ln(x, meta_pad, g, b):
    bsz, seq, d = x.shape
    nb = seq // BLOCK + 1
    m = bsz * nb * BLOCK
    return pl.pallas_call(
        _embed_ln_kernel,
        out_shape=(jax.ShapeDtypeStruct((m, d), F32), jax.ShapeDtypeStruct((m, d), BF16)),
        grid=(bsz, nb),
        in_specs=[
            pl.BlockSpec((None, BLOCK, d), lambda bb, i: (bb, jnp.maximum(i - 1, 0), 0)),
            pl.BlockSpec((BLOCK, d), lambda bb, i: (0, 0)),
            pl.BlockSpec((1, d), lambda bb, i: (0, 0)),
            pl.BlockSpec((1, d), lambda bb, i: (0, 0)),
        ],
        out_specs=(pl.BlockSpec((BLOCK, d), lambda bb, i: (bb * nb + i, 0)),
                   pl.BlockSpec((BLOCK, d), lambda bb, i: (bb * nb + i, 0))),
        compiler_params=_cparams(("parallel", "parallel")),
        name="embed_ln",
    )(x, meta_pad, g, b)


_NT = (((1,), (1,)), ((), ()))


def _inproj_kernel(a_ref, wt_ref, o_ref):
    o_ref[...] = lax.dot_general(a_ref[...], wt_ref[...].astype(BF16), _NT,
                                 preferred_element_type=F32).astype(o_ref.dtype)


def _main_dest_tile(j):
    return jnp.where(j < 2, j + 8, jnp.where(j == 2, 10, jnp.where(j < 11, j - 3, j)))


def _inproj(hb, w_in_t):
    m, d = hb.shape
    tm = _row_tile(m // BLOCK, 13)
    return pl.pallas_call(
        _inproj_kernel,
        out_shape=jax.ShapeDtypeStruct((m, MAIN_DIM), BF16),
        grid=(m // tm, MAIN_TILES),
        in_specs=[
            pl.BlockSpec((tm, d), lambda i, j: (i, 0)),
            pl.BlockSpec((None, COL_TILE, d), lambda i, j: (0, j, 0)),
        ],
        out_specs=pl.BlockSpec((tm, COL_TILE), lambda i, j: (i, _main_dest_tile(j))),
        compiler_params=_cparams(("parallel", "arbitrary")),
        name="in_proj",
    )(hb, w_in_t)


def _attn_kernel(sinks_ref, q_ref, kc_ref, vc_ref, kp_ref, vp_ref, km_ref, vm_ref, o_ref):
    n = pl.program_id(1)
    r = lax.broadcasted_iota(jnp.int32, (BLOCK, BLOCK), 0)
    c = lax.broadcasted_iota(jnp.int32, (BLOCK, BLOCK), 1)
    is_cur = c <= r
    ok_band = (is_cur & (n >= 1)) | ((c > r) & (n >= 2))
    rm = lax.broadcasted_iota(jnp.int32, (BLOCK, N_META), 0)
    cm = lax.broadcasted_iota(jnp.int32, (BLOCK, N_META), 1)
    ok_meta = (n > 0) | (cm <= rm - PAD)
    scale = HD ** -0.5
    nt = (((1,), (1,)), ((), ()))
    for h in range(HKV):
        ks = slice(h * HD, (h + 1) * HD)
        kc, kp, km = kc_ref[:, ks], kp_ref[:, ks], km_ref[PAD:BLOCK, ks]
        vc, vp, vm = vc_ref[:, ks], vp_ref[:, ks], vm_ref[PAD:BLOCK, ks]
        q4 = jnp.concatenate(
            [q_ref[:, (h * Q_PER_KV + g) * HD:(h * Q_PER_KV + g + 1) * HD] for g in range(Q_PER_KV)],
            axis=0)
        s_c4 = lax.dot_general(q4, kc, nt, preferred_element_type=F32)
        s_p4 = lax.dot_general(q4, kp, nt, preferred_element_type=F32)
        s_m4 = lax.dot_general(q4, km, nt, preferred_element_type=F32)
        pc, pp, pm, dens = [], [], [], []
        for g in range(Q_PER_KV):
            rows = slice(g * BLOCK, (g + 1) * BLOCK)
            sink = sinks_ref[h * Q_PER_KV + g]
            s_b = jnp.where(ok_band, jnp.where(is_cur, s_c4[rows], s_p4[rows]) * scale, NEG_INF)
            s_m = jnp.where(ok_meta, s_m4[rows] * scale, NEG_INF)
            mx = jnp.maximum(jnp.maximum(jnp.max(s_b, axis=-1, keepdims=True),
                                         jnp.max(s_m, axis=-1, keepdims=True)), sink)
            p_b = jnp.exp(s_b - mx)
            p_m = jnp.exp(s_m - mx)
            dens.append(jnp.sum(p_b, axis=-1, keepdims=True) + jnp.sum(p_m, axis=-1, keepdims=True)
                        + jnp.exp(sink - mx))
            pc.append(jnp.where(is_cur, p_b, 0.0).astype(BF16))
            pp.append(jnp.where(is_cur, 0.0, p_b).astype(BF16))
            pm.append(p_m.astype(BF16))
        o4 = (jnp.dot(jnp.concatenate(pc, axis=0), vc, preferred_element_type=F32)
              + jnp.dot(jnp.concatenate(pp, axis=0), vp, preferred_element_type=F32)
              + jnp.dot(jnp.concatenate(pm, axis=0), vm, preferred_element_type=F32))
        for g in range(Q_PER_KV):
            hq = h * Q_PER_KV + g
            o_ref[:, hq * HD:(hq + 1) * HD] = (o4[g * BLOCK:(g + 1) * BLOCK] / dens[g]).astype(o_ref.dtype)


def _attention(main, sinks, bsz, nb):
    m = main.shape[0]
    kb, vb = K_COL // KV_DIM, V_COL // KV_DIM
    cur = lambda bb, i: bb * nb + i
    prev = lambda bb, i: bb * nb + jnp.maximum(i - 1, 0)
    first = lambda bb, i: bb * nb
    return pl.pallas_call(
        _attn_kernel,
        out_shape=jax.ShapeDtypeStruct((m, Q_DIM), BF16),
        grid=(bsz, nb),
        in_specs=[
            pl.BlockSpec(memory_space=pltpu.SMEM),
            pl.BlockSpec((BLOCK, Q_DIM), lambda bb, i: (cur(bb, i), Q_COL // Q_DIM)),
            pl.BlockSpec((BLOCK, KV_DIM), lambda bb, i: (cur(bb, i), kb)),
            pl.BlockSpec((BLOCK, KV_DIM), lambda bb, i: (cur(bb, i), vb)),
            pl.BlockSpec((BLOCK, KV_DIM), lambda bb, i: (prev(bb, i), kb)),
            pl.BlockSpec((BLOCK, KV_DIM), lambda bb, i: (prev(bb, i), vb)),
            pl.BlockSpec((BLOCK, KV_DIM), lambda bb, i: (first(bb, i), kb)),
            pl.BlockSpec((BLOCK, KV_DIM), lambda bb, i: (first(bb, i), vb)),
        ],
        out_specs=pl.BlockSpec((BLOCK, Q_DIM), lambda bb, i: (cur(bb, i), 0)),
        compiler_params=_cparams(("parallel", "parallel")),
        name="swa_attention",
    )(sinks, main, main, main, main, main, main, main)


CIN_ROWS = BLOCK + 8
CONV_SLABS = CONV_DIM // LANES
B_SLAB = D_INNER // LANES


def _ssd_kernel(hb_ref, wdt_ref, z_ref, xs_ref, bm_ref, cm_ref,
                cw_ref, cbias_ref, dtb_ref, a_ref, dsk_ref, ng_ref,
                y_ref,
                cin_ref, xc_ref, state_ref, yacc_ref, acs_ref, acst_ref, dtt_ref, wt_ref, ea_ref):
    i = pl.program_id(1)
    row = lax.broadcasted_iota(jnp.int32, (BLOCK, 1), 0)
    live = jnp.where((i > 0) | (row >= PAD), 1.0, 0.0)

    @pl.when(i == 0)
    def _():
        state_ref[...] = jnp.zeros_like(state_ref)
        cin_ref[:, 0:8, :] = jnp.zeros((CONV_SLABS, 8, LANES), F32)

    @pl.when(i > 0)
    def _():
        cin_ref[:, 0:8, :] = cin_ref[:, BLOCK:BLOCK + 8, :]

    for sl in range(CONV_SLABS):
        cs = slice(sl * LANES, (sl + 1) * LANES)
        if sl < D_INNER // LANES:
            src = xs_ref[:, cs]
        elif sl < (D_INNER + 512) // LANES:
            src = bm_ref[:, sl * LANES - D_INNER:(sl + 1) * LANES - D_INNER]
        else:
            src = cm_ref[:, sl * LANES - D_INNER - 512:(sl + 1) * LANES - D_INNER - 512]
        cin_ref[sl, 8:CIN_ROWS, :] = src.astype(F32)
        acc = jnp.broadcast_to(cbias_ref[:, cs], (BLOCK, LANES))
        for j in range(CONV_W):
            acc = acc + cw_ref[j:j + 1, cs] * cin_ref[sl, pl.ds(8 - (CONV_W - 1) + j, BLOCK, stride=1), :]
        xc_ref[sl] = _silu(acc)

    @pl.when(i == 0)
    def _():
        xc_ref[:, 0:PAD, :] = jnp.zeros((CONV_SLABS, PAD, LANES), F32)

    dt_raw = lax.dot_general(hb_ref[...], wdt_ref[...], _NT, preferred_element_type=F32) + dtb_ref[...]
    dt = (jnp.maximum(dt_raw, 0.0) + jnp.log1p(jnp.exp(-jnp.abs(dt_raw)))) * live
    adt = dt * a_ref[...]
    rr = lax.broadcasted_iota(jnp.int32, (BLOCK, BLOCK), 0)
    cl = lax.broadcasted_iota(jnp.int32, (BLOCK, BLOCK), 1)
    causal = rr >= cl
    a_cs = jnp.dot(jnp.where(causal, 1.0, 0.0), adt, precision=HIGHEST, preferred_element_type=F32)
    a_cs_t = a_cs.T
    dt_t = dt.T
    last = a_cs_t[:, BLOCK - 1:BLOCK]
    acs_ref[...] = a_cs
    acst_ref[...] = a_cs_t
    dtt_ref[...] = dt_t
    wt_ref[...] = dt_t * jnp.exp(last - a_cs_t)
    ea_ref[...] = jnp.exp(a_cs)
    cd_t = jnp.exp(last)

    lane = lax.broadcasted_iota(jnp.int32, (1, LANES), 1)
    lo = lane < SSD_HEADDIM
    nn = (((1,), (1,)), ((), ()))
    for g in range(SSD_GROUPS):
        bg = xc_ref[B_SLAB + g]
        cg = xc_ref[B_SLAB + SSD_GROUPS + g]
        cb = lax.dot_general(cg.astype(BF16), bg.astype(BF16), nn, preferred_element_type=F32)
        bt = bg.T
        for pp in range(HEADS_PER_GROUP // 2):
            pr = g * (HEADS_PER_GROUP // 2) + pp
            ps = slice(pr * LANES, (pr + 1) * LANES)
            xs_pair = xc_ref[pr]
            st_pair = state_ref[:, ps]
            lhs, lhs_s = [], []
            for hd in (2 * pr, 2 * pr + 1):
                col = jnp.broadcast_to(acs_ref[:, hd:hd + 1], (BLOCK, BLOCK))
                seg = col - acst_ref[hd:hd + 1, :]
                dec = jnp.exp(jnp.where(causal, seg, NEG_INF))
                lhs.append((cb * dec * dtt_ref[hd:hd + 1, :]).astype(BF16))
                lhs.append((cg * jnp.broadcast_to(ea_ref[:, hd:hd + 1], (BLOCK, BLOCK))).astype(BF16))
                lhs_s.append((bt * wt_ref[hd:hd + 1, :]).astype(BF16))
            xs_lo = jnp.where(lo, xs_pair, 0.0).astype(BF16)
            xs_hi = jnp.where(lo, 0.0, xs_pair).astype(BF16)
            st_lo = jnp.where(lo, st_pair, 0.0).astype(BF16)
            st_hi = jnp.where(lo, 0.0, st_pair).astype(BF16)
            y_pair = jnp.dot(jnp.concatenate(lhs, axis=1),
                             jnp.concatenate([xs_lo, st_lo, xs_hi, st_hi], axis=0),
                             preferred_element_type=F32)
            yacc_ref[:, ps] = y_pair + dsk_ref[:, ps] * xs_pair
            st_new = jnp.dot(jnp.concatenate(lhs_s, axis=1),
                             jnp.concatenate([xs_lo, xs_hi], axis=0), preferred_element_type=F32)
            cd_row = jnp.where(lo, jnp.broadcast_to(cd_t[2 * pr:2 * pr + 1, :], (1, LANES)),
                               jnp.broadcast_to(cd_t[2 * pr + 1:2 * pr + 2, :], (1, LANES)))
            state_ref[:, ps] = st_pair * cd_row + st_new

    gw = D_INNER // SSD_GROUPS
    for g in range(SSD_GROUPS):
        gs = slice(g * gw, (g + 1) * gw)
        yz = yacc_ref[:, gs] * _silu(z_ref[:, gs].astype(F32))
        ms = jnp.mean(yz * yz, axis=-1, keepdims=True)
        y_ref[:, gs] = (yz * lax.rsqrt(ms + RMS_EPS) * ng_ref[:, gs]).astype(y_ref.dtype)


def _ssd(hb, main, w_dt, conv_w, conv_b, dt_bias, a_neg, d_skip_x, norm_g, bsz, nb):
    m = hb.shape[0]
    cur = lambda bb, i: bb * nb + i
    full = lambda shape: pl.BlockSpec(shape, lambda bb, i: (0, 0))
    return pl.pallas_call(
        _ssd_kernel,
        out_shape=jax.ShapeDtypeStruct((m, D_INNER), BF16),
        grid=(bsz, nb),
        in_specs=[
            pl.BlockSpec((BLOCK, D_MODEL), lambda bb, i: (cur(bb, i), 0)),
            full((LANES, D_MODEL)),
            pl.BlockSpec((BLOCK, D_INNER), lambda bb, i: (cur(bb, i), Z_COL // D_INNER)),
            pl.BlockSpec((BLOCK, D_INNER), lambda bb, i: (cur(bb, i), XS_COL // D_INNER)),
            pl.BlockSpec((BLOCK, 512), lambda bb, i: (cur(bb, i), B_COL // 512)),
            pl.BlockSpec((BLOCK, 512), lambda bb, i: (cur(bb, i), C_COL // 512)),
            full((CONV_W, CONV_DIM)), full((1, CONV_DIM)), full((1, LANES)), full((1, LANES)),
            full((1, D_INNER)), full((1, D_INNER)),
        ],
        out_specs=pl.BlockSpec((BLOCK, D_INNER), lambda bb, i: (cur(bb, i), 0)),
        scratch_shapes=[
            pltpu.VMEM((CONV_SLABS, CIN_ROWS, LANES), F32),
            pltpu.VMEM((CONV_SLABS, BLOCK, LANES), F32),
            pltpu.VMEM((D_STATE, D_INNER), F32),
            pltpu.VMEM((BLOCK, D_INNER), F32),
            pltpu.VMEM((BLOCK, LANES), F32),
            pltpu.VMEM((LANES, BLOCK), F32),
            pltpu.VMEM((LANES, BLOCK), F32),
            pltpu.VMEM((LANES, BLOCK), F32),
            pltpu.VMEM((BLOCK, LANES), F32),
        ],
        compiler_params=_cparams(("arbitrary", "arbitrary")),
        name="ssd",
    )(hb, w_dt, main, main, main, main, conv_w, conv_b, dt_bias, a_neg, d_skip_x, norm_g)


def _merge_kernel(hb_ref, at_ref, y_ref, wga_ref, wgs_ref, wa_ref, ws_ref, o_ref,
                  wga_b, wgs_b, wa_b, ws_b):
    @pl.when(pl.program_id(1) == 0)
    def _():
        wga_b[...] = wga_ref[...].astype(BF16)
        wgs_b[...] = wgs_ref[...].astype(BF16)
        wa_b[...] = wa_ref[...].astype(BF16)
        ws_b[...] = ws_ref[...].astype(BF16)

    hb = hb_ref[...]
    ga = lax.dot_general(hb, wga_b[...], _NT, preferred_element_type=F32)
    gs = lax.dot_general(hb, wgs_b[...], _NT, preferred_element_type=F32)
    pa = jnp.dot(at_ref[...], wa_b[...], preferred_element_type=F32)
    ps = jnp.dot(y_ref[...], ws_b[...], preferred_element_type=F32)
    o_ref[...] = (_sigmoid(ga) * pa + _sigmoid(gs) * ps).astype(o_ref.dtype)


def _merge(hb, attn, y, w_gates, w_a, w_s):
    m = hb.shape[0]
    tm = _row_tile(m // BLOCK, 5)
    nt = D_MODEL // COL_TILE
    once = pl.Buffered(1)
    return pl.pallas_call(
        _merge_kernel,
        out_shape=jax.ShapeDtypeStruct((m, D_MODEL), BF16),
        grid=(nt, m // tm),
        in_specs=[
            pl.BlockSpec((tm, D_MODEL), lambda j, i: (i, 0)),
            pl.BlockSpec((tm, Q_DIM), lambda j, i: (i, 0)),
            pl.BlockSpec((tm, D_INNER), lambda j, i: (i, 0)),
            pl.BlockSpec((COL_TILE, D_MODEL), lambda j, i: (j, 0), pipeline_mode=once),
            pl.BlockSpec((COL_TILE, D_MODEL), lambda j, i: (j + nt, 0), pipeline_mode=once),
            pl.BlockSpec((Q_DIM, COL_TILE), lambda j, i: (0, j), pipeline_mode=once),
            pl.BlockSpec((D_INNER, COL_TILE), lambda j, i: (0, j), pipeline_mode=once),
        ],
        out_specs=pl.BlockSpec((tm, COL_TILE), lambda j, i: (i, j)),
        scratch_shapes=[
            pltpu.VMEM((COL_TILE, D_MODEL), BF16), pltpu.VMEM((COL_TILE, D_MODEL), BF16),
            pltpu.VMEM((Q_DIM, COL_TILE), BF16), pltpu.VMEM((D_INNER, COL_TILE), BF16),
        ],
        compiler_params=_cparams(("arbitrary", "arbitrary")),
        name="branch_merge",
    )(hb, attn, y, w_gates, w_gates, w_a, w_s)


def _outproj_router_kernel(mg_ref, h_ref, wo_ref, g_ref, b_ref, wrh_ref, wrl_ref, br_ref,
                           o_ref, ids_ref, wts_ref, cnt_ref, carry_ref, *, nb, tm):
    step = pl.program_id(0)

    @pl.when(step == 0)
    def _():
        carry_ref[...] = jnp.zeros_like(carry_ref)

    mix = jnp.dot(mg_ref[...], wo_ref[...], preferred_element_type=F32)
    h1 = _ln_rows(ALPHA * h_ref[...] + mix, g_ref[...], b_ref[...])
    o_ref[...] = h1

    h_hi = h1.astype(BF16)
    h_lo = (h1 - h_hi.astype(F32)).astype(BF16)
    logits = (lax.dot_general(h_hi, wrh_ref[...], _NT, preferred_element_type=F32)
              + (lax.dot_general(h_lo, wrh_ref[...], _NT, preferred_element_type=F32)
                 + lax.dot_general(h_hi, wrl_ref[...], _NT, preferred_element_type=F32))) + br_ref[...]
    lane = lax.broadcasted_iota(jnp.int32, (tm, LANES), 1)
    row = lax.broadcasted_iota(jnp.int32, (BLOCK, 1), 0)
    blocks = tm // BLOCK
    real = jnp.concatenate(
        [(((step * blocks + s) % nb) > 0) | (row >= PAD) for s in range(blocks)], axis=0)
    big = jnp.int32(LANES)

    is_g = lane < MOE_GROUPS
    gl = jnp.where(is_g, logits, -jnp.inf)
    ge = jnp.exp(gl - jnp.max(gl, axis=-1, keepdims=True))
    pg = ge / jnp.sum(ge, axis=-1, keepdims=True)
    p_top = jnp.max(pg, axis=-1, keepdims=True)
    g_idx = jnp.min(jnp.where(is_g & (pg == p_top), lane, big), axis=-1, keepdims=True)

    base = MOE_GROUPS + g_idx * EXPERTS_PER_GROUP
    sel = (lane >= base) & (lane < base + EXPERTS_PER_GROUP)
    el = jnp.where(sel, logits, -jnp.inf)
    ee = jnp.exp(el - jnp.max(el, axis=-1, keepdims=True))
    pe = ee / jnp.sum(ee, axis=-1, keepdims=True)
    v1 = jnp.max(jnp.where(sel, pe, -1.0), axis=-1, keepdims=True)
    i1 = jnp.min(jnp.where(sel & (pe == v1), lane, big), axis=-1, keepdims=True)
    sel2 = sel & (lane != i1)
    v2 = jnp.max(jnp.where(sel2, pe, -1.0), axis=-1, keepdims=True)
    i2 = jnp.min(jnp.where(sel2 & (pe == v2), lane, big), axis=-1, keepdims=True)
    vs = v1 + v2
    w1 = p_top * (v1 / vs)
    w2 = p_top * (v2 / vs)

    hit1 = (lane == i1) & real
    hit2 = (lane == i2) & real
    onehot = jnp.where(hit1 | hit2, 1.0, 0.0)
    rr = lax.broadcasted_iota(jnp.int32, (tm, tm), 0)
    cl = lax.broadcasted_iota(jnp.int32, (tm, tm), 1)
    before = jnp.where(rr > cl, 1.0, 0.0).astype(BF16)
    cum = jnp.dot(before, onehot.astype(BF16), preferred_element_type=F32) + carry_ref[...]
    r1 = jnp.sum(jnp.where(hit1, cum, 0.0), axis=-1, keepdims=True)
    r2 = jnp.sum(jnp.where(hit2, cum, 0.0), axis=-1, keepdims=True)
    carry_ref[...] = carry_ref[...] + jnp.sum(onehot, axis=0, keepdims=True)
    cnt_ref[...] = carry_ref[...].astype(jnp.int32)

    e1 = jnp.where(real, i1 - MOE_GROUPS, -1)
    e2 = jnp.where(real, i2 - MOE_GROUPS, -1)
    ids = jnp.where(lane == 0, e1, jnp.where(lane == 1, e2, jnp.where(
        lane == 2, r1.astype(jnp.int32), jnp.where(lane == 3, r2.astype(jnp.int32), 0))))
    ids_ref[...] = ids.T[0:8, :]
    l8 = lax.broadcasted_iota(jnp.int32, (tm, 8), 1)
    wts_ref[...] = jnp.where(l8 == 0, w1, jnp.where(l8 == 1, w2, 0.0))


def _outproj_router(merged, h0, w_o, g, b, wr_hi, wr_lo, b_r, nb):
    m = h0.shape[0]
    tm = _row_tile(m // BLOCK, 5)
    row_blk = lambda width: pl.BlockSpec((tm, width), lambda i: (i, 0))
    full = lambda shape: pl.BlockSpec(shape, lambda i: (0, 0))
    return pl.pallas_call(
        functools.partial(_outproj_router_kernel, nb=nb, tm=tm),
        out_shape=(jax.ShapeDtypeStruct((m, D_MODEL), F32),
                   jax.ShapeDtypeStruct((8, m), jnp.int32), jax.ShapeDtypeStruct((m, 8), F32),
                   jax.ShapeDtypeStruct((1, LANES), jnp.int32)),
        grid=(m // tm,),
        in_specs=[row_blk(D_MODEL), row_blk(D_MODEL), full((D_MODEL, D_MODEL)),
                  full((1, D_MODEL)), full((1, D_MODEL)),
                  full((LANES, D_MODEL)), full((LANES, D_MODEL)), full((1, LANES))],
        out_specs=(row_blk(D_MODEL), pl.BlockSpec((8, tm), lambda i: (0, i)), row_blk(8),
                   full((1, LANES))),
        scratch_shapes=[pltpu.VMEM((1, LANES), F32)],
        compiler_params=_cparams(("arbitrary",)),
        name="out_proj_ln1_router",
    )(merged, h0, w_o, g, b, wr_hi, wr_lo, b_r)


HALF = D_MODEL // 2
HI_MASK = 0xFFFF0000


def _pack_rows(x):
    lo = lax.bitcast_convert_type(x[:, :HALF].astype(BF16).astype(F32), jnp.uint32) >> 16
    hi = lax.bitcast_convert_type(x[:, HALF:].astype(BF16).astype(F32), jnp.uint32) & jnp.uint32(HI_MASK)
    return hi | lo


def _unpack_rows(w):
    lo = lax.bitcast_convert_type(w << 16, F32)
    hi = lax.bitcast_convert_type(w & jnp.uint32(HI_MASK), F32)
    return lo, hi


ROW_SUB = HALF // LANES


def _put_tile_rows(ref, lead, x):
    for s in range(ROW_SUB):
        ref[lead + (pl.ds(s, x.shape[0], stride=ROW_SUB), slice(None))] = x[:, s * LANES:(s + 1) * LANES]


def _get_tile_rows(ref, lead, r):
    return jnp.concatenate(
        [ref[lead + (pl.ds(s, r, stride=ROW_SUB), slice(None))] for s in range(ROW_SUB)], axis=1)


def _tile_rows(ref, row, n=1):
    return ref.at[pl.ds(pl.multiple_of(row * ROW_SUB, ROW_SUB), n * ROW_SUB), :]


def _dest_kernel(pstart_ref, ids_ref, dest_ref, *, n_slots):
    e = ids_ref[0:TOP_K, :]
    acc = jnp.zeros_like(e)
    for ex in range(N_EXPERTS):
        acc = jnp.where(e == ex, pstart_ref[ex], acc)
    k = lax.broadcasted_iota(jnp.int32, e.shape, 0)
    t = lax.broadcasted_iota(jnp.int32, e.shape, 1)
    dump = n_slots + k * PAD + (t & (BLOCK - 1))
    dest_ref[...] = jnp.where(e >= 0, acc + ids_ref[TOP_K:2 * TOP_K, :], dump)


def _dest_rows(pstart, ids, n_slots):
    m = ids.shape[1]
    return pl.pallas_call(
        functools.partial(_dest_kernel, n_slots=n_slots),
        out_shape=jax.ShapeDtypeStruct((TOP_K, m), jnp.int32),
        in_specs=[pl.BlockSpec(memory_space=pltpu.SMEM), pl.BlockSpec(memory_space=pltpu.VMEM)],
        out_specs=pl.BlockSpec(memory_space=pltpu.VMEM),
        name="moe_dest",
    )(pstart, ids)


def _dispatch_kernel(dest_ref, zlo_ref, zhi_ref, nused_ref, h_ref, xs_hbm, ring, zrow, zblk, sems, zsem,
                     *, n_blocks):
    step = pl.program_id(0)
    n_steps = pl.num_programs(0)
    n_rows = n_steps * BLOCK
    slot = step % 2

    def zero_rows(start):
        def per_expert(ex, carry):
            def per_row(p, c):
                cp = pltpu.make_async_copy(zrow, _tile_rows(xs_hbm, p), zsem)
                cp.start() if start else cp.wait()
                return c
            return lax.fori_loop(zlo_ref[ex], zhi_ref[ex], per_row, carry)
        lax.fori_loop(0, N_EXPERTS, per_expert, 0)

        def per_block(blk, c):
            cp = pltpu.make_async_copy(zblk, _tile_rows(xs_hbm, blk * MOE_ROWS, MOE_ROWS), zsem)
            cp.start() if start else cp.wait()
            return c
        lax.fori_loop(nused_ref[0], n_blocks, per_block, 0)

    @pl.when(step == 0)
    def _():
        zrow[...] = jnp.zeros_like(zrow)
        zblk[...] = jnp.zeros_like(zblk)
        zero_rows(True)

    _put_tile_rows(ring, (slot,), _pack_rows(h_ref[...]))
    base = step * BLOCK

    def body(t, carry):
        for k in range(TOP_K):
            d = dest_ref[k * n_rows + base + t]
            pltpu.make_async_copy(_tile_rows(ring.at[slot], t), _tile_rows(xs_hbm, d),
                                  sems.at[slot]).start()
        return carry
    lax.fori_loop(0, BLOCK, body, 0, unroll=4)

    def wait_slot(s):
        for k in range(TOP_K):
            pltpu.make_async_copy(ring.at[s], _tile_rows(xs_hbm, 0, BLOCK), sems.at[s]).wait()

    @pl.when(step > 0)
    def _():
        wait_slot(1 - slot)

    @pl.when(step == n_steps - 1)
    def _():
        wait_slot(slot)
        zero_rows(False)


def _dispatch(h1, dest, zlo, zhi, n_used, n_blocks):
    m = h1.shape[0]
    n_slots = n_blocks * MOE_ROWS
    grid_spec = pltpu.PrefetchScalarGridSpec(
        num_scalar_prefetch=4,
        grid=(m // BLOCK,),
        in_specs=[pl.BlockSpec((BLOCK, D_MODEL), lambda i, *_: (i, 0))],
        out_specs=pl.BlockSpec(memory_space=pl.ANY),
        scratch_shapes=[
            pltpu.VMEM((2, BLOCK * ROW_SUB, LANES), jnp.uint32),
            pltpu.VMEM((ROW_SUB, LANES), jnp.uint32),
            pltpu.VMEM((MOE_ROWS * ROW_SUB, LANES), jnp.uint32),
            pltpu.SemaphoreType.DMA((2,)),
            pltpu.SemaphoreType.DMA(()),
        ],
    )
    return pl.pallas_call(
        functools.partial(_dispatch_kernel, n_blocks=n_blocks),
        out_shape=jax.ShapeDtypeStruct(((n_slots + TOP_K * PAD) * ROW_SUB, LANES), jnp.uint32),
        grid_spec=grid_spec,
        compiler_params=_cparams(("arbitrary",)),
        name="moe_dispatch",
    )(dest, zlo, zhi, n_used, h1)


def _expert_kernel(blk_e_ref, nused_ref, xs_ref, wg_ref, wu_ref, wd_ref, o_ref, wgb, wub, wdb):
    i = pl.program_id(0)
    n_used = nused_ref[0]

    @pl.when(i < n_used)
    def _():
        fresh = (i == 0) | (blk_e_ref[i] != blk_e_ref[jnp.maximum(i - 1, 0)])

        @pl.when(fresh)
        def _():
            wgb[...] = wg_ref[...].astype(BF16)
            wub[...] = wu_ref[...].astype(BF16)
            wdb[...] = wd_ref[...].astype(BF16)

        lo, hi = _unpack_rows(_get_tile_rows(xs_ref, (), MOE_ROWS))
        x_lo, x_hi = lo.astype(BF16), hi.astype(BF16)
        gate = (jnp.dot(x_lo, wgb[0:HALF, :], preferred_element_type=F32)
                + jnp.dot(x_hi, wgb[HALF:D_MODEL, :], preferred_element_type=F32))
        up = (jnp.dot(x_lo, wub[0:HALF, :], preferred_element_type=F32)
              + jnp.dot(x_hi, wub[HALF:D_MODEL, :], preferred_element_type=F32))
        hid = (_silu(gate) * up).astype(BF16)
        _put_tile_rows(o_ref, (), _pack_rows(jnp.dot(hid, wdb[...], preferred_element_type=F32)))

    @pl.when(i >= n_used)
    def _():
        o_ref[...] = jnp.zeros_like(o_ref)


def _experts(xs, w_gate, w_up, w_down, blk_e, n_used, n_blocks):
    used = lambda i, nu: jnp.minimum(i, nu[0] - 1)
    grid_spec = pltpu.PrefetchScalarGridSpec(
        num_scalar_prefetch=2,
        grid=(n_blocks,),
        in_specs=[
            pl.BlockSpec((MOE_ROWS * ROW_SUB, LANES), lambda i, be, nu: (used(i, nu), 0)),
            pl.BlockSpec((None, None, D_MODEL, D_EXPERT), lambda i, be, nu: (0, be[i], 0, 0)),
            pl.BlockSpec((None, None, D_MODEL, D_EXPERT), lambda i, be, nu: (0, be[i], 0, 0)),
            pl.BlockSpec((None, None, D_EXPERT, D_MODEL), lambda i, be, nu: (0, be[i], 0, 0)),
        ],
        out_specs=pl.BlockSpec((MOE_ROWS * ROW_SUB, LANES), lambda i, be, nu: (i, 0)),
        scratch_shapes=[
            pltpu.VMEM((D_MODEL, D_EXPERT), BF16),
            pltpu.VMEM((D_MODEL, D_EXPERT), BF16),
            pltpu.VMEM((D_EXPERT, D_MODEL), BF16),
        ],
    )
    return pl.pallas_call(
        _expert_kernel,
        out_shape=jax.ShapeDtypeStruct((n_blocks * MOE_ROWS * ROW_SUB, LANES), jnp.uint32),
        grid_spec=grid_spec,
        compiler_params=_cparams(("arbitrary",)),
        name="moe_experts",
    )(blk_e, n_used, xs, w_gate, w_up, w_down)


def _combine_kernel(dest_ref, h_ref, wts_ref, yb_hbm, g_ref, b_ref, o_ref, ybuf, sems, *, nb):
    bb = pl.program_id(0)
    i = pl.program_id(1)
    n_i = pl.num_programs(1)
    step = bb * n_i + i
    n_steps = pl.num_programs(0) * n_i

    n_rows = pl.num_programs(0) * nb * BLOCK

    def assign_base(b_, i_):
        return (b_ * nb + i_ + 1) * BLOCK

    def start_gather(base, slot):
        def body(t, carry):
            for k in range(TOP_K):
                d = dest_ref[k * n_rows + base + t]
                pltpu.make_async_copy(_tile_rows(yb_hbm, d), _tile_rows(ybuf.at[slot, k], t),
                                      sems.at[slot]).start()
            return carry
        lax.fori_loop(0, BLOCK, body, 0, unroll=4)

    def wait_gather(slot):
        for k in range(TOP_K):
            pltpu.make_async_copy(_tile_rows(yb_hbm, 0, BLOCK), ybuf.at[slot, k], sems.at[slot]).wait()

    @pl.when(step == 0)
    def _():
        start_gather(assign_base(0, 0), 0)

    @pl.when(step + 1 < n_steps)
    def _():
        nxt = jnp.where(i + 1 < n_i, assign_base(bb, i + 1), assign_base(bb + 1, 0))
        start_gather(nxt, (step + 1) % 2)

    slot = step % 2
    wait_gather(slot)
    wts = wts_ref[...]
    lo0, hi0 = _unpack_rows(_get_tile_rows(ybuf, (slot, 0), BLOCK))
    lo1, hi1 = _unpack_rows(_get_tile_rows(ybuf, (slot, 1), BLOCK))
    w0, w1 = wts[:, 0:1], wts[:, 1:2]
    r_lo = ALPHA * h_ref[:, 0:HALF] + (lo0 * w0 + lo1 * w1)
    r_hi = ALPHA * h_ref[:, HALF:D_MODEL] + (hi0 * w0 + hi1 * w1)
    mu = (jnp.sum(r_lo, axis=-1, keepdims=True) + jnp.sum(r_hi, axis=-1, keepdims=True)) / D_MODEL
    c_lo, c_hi = r_lo - mu, r_hi - mu
    var = (jnp.sum(c_lo * c_lo, axis=-1, keepdims=True)
           + jnp.sum(c_hi * c_hi, axis=-1, keepdims=True)) / D_MODEL
    inv = lax.rsqrt(var + LN_EPS)
    o_ref[:, 0:HALF] = c_lo * inv * g_ref[:, 0:HALF] + b_ref[:, 0:HALF]
    o_ref[:, HALF:D_MODEL] = c_hi * inv * g_ref[:, HALF:D_MODEL] + b_ref[:, HALF:D_MODEL]


def _combine_ln(h1, wts, yb, dest, g, b, bsz, nb):
    grid_spec = pltpu.PrefetchScalarGridSpec(
        num_scalar_prefetch=1,
        grid=(bsz, nb - 1),
        in_specs=[
            pl.BlockSpec((BLOCK, D_MODEL), lambda bb, i, d: (bb * nb + i + 1, 0)),
            pl.BlockSpec((BLOCK, 8), lambda bb, i, d: (bb * nb + i + 1, 0)),
            pl.BlockSpec(memory_space=pl.ANY),
            pl.BlockSpec((1, D_MODEL), lambda bb, i, d: (0, 0)),
            pl.BlockSpec((1, D_MODEL), lambda bb, i, d: (0, 0)),
        ],
        out_specs=pl.BlockSpec((None, BLOCK, D_MODEL), lambda bb, i, d: (bb, i, 0)),
        scratch_shapes=[
            pltpu.VMEM((2, TOP_K, BLOCK * ROW_SUB, LANES), jnp.uint32),
            pltpu.SemaphoreType.DMA((2,)),
        ],
    )
    return pl.pallas_call(
        functools.partial(_combine_kernel, nb=nb),
        out_shape=jax.ShapeDtypeStruct((bsz, (nb - 1) * BLOCK, D_MODEL), F32),
        grid_spec=grid_spec,
        compiler_params=_cparams(("arbitrary", "arbitrary")),
        name="moe_combine_ln2",
    )(dest, h1, wts, yb, g, b)


def _dispatch_tables(ids, counts, n_blocks):
    counts = counts[0, MOE_GROUPS:MOE_GROUPS + N_EXPERTS]
    pcounts = (counts + MOE_ROWS - 1) // MOE_ROWS * MOE_ROWS
    pend = jnp.cumsum(pcounts)
    pstart = (pend - pcounts).astype(jnp.int32)
    n_used = jnp.maximum(pend[-1] // MOE_ROWS, 1).astype(jnp.int32)
    blk = jnp.arange(n_blocks, dtype=jnp.int32)
    blk_e = jnp.minimum(jnp.sum(pend[None, :] <= (blk * MOE_ROWS)[:, None], axis=1), N_EXPERTS - 1)
    blk_e = jnp.where(blk < n_used, blk_e, blk_e[n_used - 1]).astype(jnp.int32)
    dest = _dest_rows(pstart, ids, n_blocks * MOE_ROWS).reshape(-1)
    return dest, pstart, (pstart + counts).astype(jnp.int32), pend.astype(jnp.int32), blk_e, n_used.reshape(1)


def kernel(x, meta_tokens, ln_emb_g, ln_emb_b, w_in, conv_w, conv_b, dt_bias, a_log, d_skip, ssd_norm_g, sinks, w_br_attn, w_br_ssd, w_o, ln1_g, ln1_b, w_router_group, b_router_group, w_router_expert, b_router_expert, w_gate, w_up, w_down, ln2_g, ln2_b):
    bsz, seq, d = x.shape
    assert d == D_MODEL and seq % BLOCK == 0 and w_in.shape[0] == DEPTH
    nb = seq // BLOCK + 1
    m = bsz * nb * BLOCK
    row2 = lambda v: v.reshape(1, -1).astype(F32)

    meta_pad = jnp.concatenate([jnp.zeros((PAD, d), F32), meta_tokens.astype(F32)], axis=0)
    h0, h0b = _embed_ln(x, meta_pad, row2(ln_emb_g), row2(ln_emb_b))

    w_in_t = jnp.swapaxes(w_in, 1, 2)
    main = _inproj(h0b, w_in_t)
    attn = _attention(main, sinks[0].astype(F32), bsz, nb)

    w_dt = jnp.pad(w_in_t[0, DT_OFF:DT_OFF + SSD_HEADS], ((0, LANES - SSD_HEADS), (0, 0))).astype(BF16)
    pad_h = lambda v: jnp.pad(v.astype(F32), (0, LANES - SSD_HEADS)).reshape(1, LANES)
    y = _ssd(h0b, main, w_dt, conv_w[0].astype(F32), row2(conv_b[0]), pad_h(dt_bias[0]),
             pad_h(-jnp.exp(a_log[0].astype(F32))), row2(jnp.repeat(d_skip[0], SSD_HEADDIM)),
             row2(ssd_norm_g[0]), bsz, nb)

    merged = _merge(h0b, attn, y, w_in_t[0, GATE_OFF:], w_br_attn[0], w_br_ssd[0])
    w_r = jnp.pad(jnp.concatenate([w_router_group[0].T, w_router_expert[0].T], axis=0).astype(F32),
                  ((0, LANES - MOE_GROUPS - N_EXPERTS), (0, 0)))
    wr_hi = w_r.astype(BF16)
    wr_lo = (w_r - wr_hi.astype(F32)).astype(BF16)
    b_r = jnp.pad(jnp.concatenate([b_router_group[0], b_router_expert[0]]).astype(F32),
                  (0, LANES - MOE_GROUPS - N_EXPERTS)).reshape(1, LANES)
    h1, ids, wts, counts = _outproj_router(merged, h0, w_o[0].astype(BF16), row2(ln1_g[0]),
                                           row2(ln1_b[0]), wr_hi, wr_lo, b_r, nb)

    n_assign = bsz * (seq + N_META) * TOP_K
    n_blocks = -(-n_assign // MOE_ROWS) + N_EXPERTS
    dest, _, zlo, zhi, blk_e, n_used = _dispatch_tables(ids, counts, n_blocks)
    xs = _dispatch(h1, dest, zlo, zhi, n_used, n_blocks)
    yb = _experts(xs, w_gate, w_up, w_down, blk_e, n_used, n_blocks)
    return _combine_ln(h1, wts, yb, dest, row2(ln2_g[0]), row2(ln2_b[0]), bsz, nb)
```

```python
import functools

import jax
import jax.numpy as jnp
from jax import lax
from jax.experimental import pallas as pl
from jax.experimental.pallas import tpu as pltpu

F32 = jnp.float32
BF16 = jnp.bfloat16
HIGHEST = lax.Precision.HIGHEST

D_MODEL = 2048
N_META = 16
BLOCK = 128
PAD = BLOCK - N_META
WINDOW = 128
HQ, HKV, HD = 16, 4, 64
Q_PER_KV = HQ // HKV
D_INNER = 2048
SSD_HEADDIM = 64
SSD_HEADS = D_INNER // SSD_HEADDIM
SSD_GROUPS = 4
HEADS_PER_GROUP = SSD_HEADS // SSD_GROUPS
D_STATE = 128
CONV_W = 4
CONV_DIM = D_INNER + 2 * SSD_GROUPS * D_STATE
MOE_GROUPS = 8
EXPERTS_PER_GROUP = 8
N_EXPERTS = MOE_GROUPS * EXPERTS_PER_GROUP
TOP_K = 2
D_EXPERT = 512
LN_EPS = 1e-5
RMS_EPS = 1e-5
NEG_INF = -1e30
DEPTH = 1
ALPHA = (2.0 * DEPTH) ** 0.25

Q_DIM = HQ * HD
KV_DIM = HKV * HD
MAIN_DIM = Q_DIM + 2 * KV_DIM + D_INNER + CONV_DIM
DT_OFF = MAIN_DIM
GATE_OFF = MAIN_DIM + SSD_HEADS
COL_TILE = 512
MAIN_TILES = MAIN_DIM // COL_TILE
Z_COL, XS_COL, Q_COL, K_COL, V_COL, B_COL, C_COL = 0, 2048, 4096, 5120, 5376, 5632, 6144

LANES = 128
MOE_ROWS = 576
VMEM_LIMIT = 56 * 1024 * 1024


def _cparams(sem, vmem=VMEM_LIMIT):
    return pltpu.CompilerParams(dimension_semantics=sem, vmem_limit_bytes=vmem)


def _ln_rows(x, g, b):
    mu = jnp.mean(x, axis=-1, keepdims=True)
    xc = x - mu
    var = jnp.mean(xc * xc, axis=-1, keepdims=True)
    return xc * lax.rsqrt(var + LN_EPS) * g + b


def _sigmoid(x):
    return 1.0 / (1.0 + jnp.exp(-x))


def _silu(x):
    return x * _sigmoid(x)


def _row_tile(n_blocks, max_blocks):
    best = 1
    for c in range(1, max_blocks + 1):
        if n_blocks % c == 0:
            best = c
    return best * BLOCK


def _embed_ln_kernel(x_ref, meta_ref, g_ref, b_ref, h_ref, hb_ref):
    i = pl.program_id(1)
    src = jnp.where(i == 0, meta_ref[...], x_ref[...])
    y = _ln_rows(src, g_ref[...], b_ref[...])
    row = lax.broadcasted_iota(jnp.int32, (BLOCK, 1), 0)
    y = jnp.where((i > 0) | (row >= PAD), y, 0.0)
    h_ref[...] = y
    hb_ref[...] = y.astype(BF16)


def _embed_ln(x, meta_pad, g, b):
    bsz, seq, d = x.shape
    nb = seq // BLOCK + 1
    m = bsz * nb * BLOCK
    return pl.pallas_call(
        _embed_ln_kernel,
        out_shape=(jax.ShapeDtypeStruct((m, d), F32), jax.ShapeDtypeStruct((m, d), BF16)),
        grid=(bsz, nb),
        in_specs=[
            pl.BlockSpec((None, BLOCK, d), lambda bb, i: (bb, jnp.maximum(i - 1, 0), 0)),
            pl.BlockSpec((BLOCK, d), lambda bb, i: (0, 0)),
            pl.BlockSpec((1, d), lambda bb, i: (0, 0)),
            pl.BlockSpec((1, d), lambda bb, i: (0, 0)),
        ],
        out_specs=(pl.BlockSpec((BLOCK, d), lambda bb, i: (bb * nb + i, 0)),
                   pl.BlockSpec((BLOCK, d), lambda bb, i: (bb * nb + i, 0))),
        compiler_params=_cparams(("parallel", "parallel")),
        name="embed_ln",
    )(x, meta_pad, g, b)


_NT = (((1,), (1,)), ((), ()))


def _inproj_kernel(a_ref, wt_ref, o_ref):
    o_ref[...] = lax.dot_general(a_ref[...], wt_ref[...].astype(BF16), _NT,
                                 preferred_element_type=F32).astype(o_ref.dtype)


def _main_dest_tile(j):
    return jnp.where(j < 2, j + 8, jnp.where(j == 2, 10, jnp.where(j < 11, j - 3, j)))


def _inproj(hb, w_in_t):
    m, d = hb.shape
    tm = _row_tile(m // BLOCK, 13)
    return pl.pallas_call(
        _inproj_kernel,
        out_shape=jax.ShapeDtypeStruct((m, MAIN_DIM), BF16),
        grid=(m // tm, MAIN_TILES),
        in_specs=[
            pl.BlockSpec((tm, d), lambda i, j: (i, 0)),
            pl.BlockSpec((None, COL_TILE, d), lambda i, j: (0, j, 0)),
        ],
        out_specs=pl.BlockSpec((tm, COL_TILE), lambda i, j: (i, _main_dest_tile(j))),
        compiler_params=_cparams(("parallel", "arbitrary")),
        name="in_proj",
    )(hb, w_in_t)


def _attn_body(n, sinks_ref, q_ref, kc_ref, vc_ref, kp_ref, vp_ref, km_ref, vm_ref, o_ref):
    r = lax.broadcasted_iota(jnp.int32, (BLOCK, BLOCK), 0)
    c = lax.broadcasted_iota(jnp.int32, (BLOCK, BLOCK), 1)
    is_cur = c <= r
    ok_band = (is_cur & (n >= 1)) | ((c > r) & (n >= 2))
    rm = lax.broadcasted_iota(jnp.int32, (BLOCK, N_META), 0)
    cm = lax.broadcasted_iota(jnp.int32, (BLOCK, N_META), 1)
    ok_meta = (n > 0) | (cm <= rm - PAD)
    scale = HD ** -0.5
    nt = (((1,), (1,)), ((), ()))
    for h in range(HKV):
        ks = slice(h * HD, (h + 1) * HD)
        kc, kp, km = kc_ref[:, ks], kp_ref[:, ks], km_ref[PAD:BLOCK, ks]
        vc, vp, vm = vc_ref[:, ks], vp_ref[:, ks], vm_ref[PAD:BLOCK, ks]
        q4 = jnp.concatenate(
            [q_ref[:, (h * Q_PER_KV + g) * HD:(h * Q_PER_KV + g + 1) * HD] for g in range(Q_PER_KV)],
            axis=0)
        s_c4 = lax.dot_general(q4, kc, nt, preferred_element_type=F32)
        s_p4 = lax.dot_general(q4, kp, nt, preferred_element_type=F32)
        s_m4 = lax.dot_general(q4, km, nt, preferred_element_type=F32)
        pc, pp, pm, dens = [], [], [], []
        for g in range(Q_PER_KV):
            rows = slice(g * BLOCK, (g + 1) * BLOCK)
            sink = sinks_ref[h * Q_PER_KV + g]
            s_b = jnp.where(ok_band, jnp.where(is_cur, s_c4[rows], s_p4[rows]) * scale, NEG_INF)
            s_m = jnp.where(ok_meta, s_m4[rows] * scale, NEG_INF)
            mx = jnp.maximum(jnp.maximum(jnp.max(s_b, axis=-1, keepdims=True),
                                         jnp.max(s_m, axis=-1, keepdims=True)), sink)
            p_b = jnp.exp(s_b - mx)
            p_m = jnp.exp(s_m - mx)
            dens.append(jnp.sum(p_b, axis=-1, keepdims=True) + jnp.sum(p_m, axis=-1, keepdims=True)
                        + jnp.exp(sink - mx))
            pc.append(jnp.where(is_cur, p_b, 0.0).astype(BF16))
            pp.append(jnp.where(is_cur, 0.0, p_b).astype(BF16))
            pm.append(p_m.astype(BF16))
        o4 = (jnp.dot(jnp.concatenate(pc, axis=0), vc, preferred_element_type=F32)
              + jnp.dot(jnp.concatenate(pp, axis=0), vp, preferred_element_type=F32)
              + jnp.dot(jnp.concatenate(pm, axis=0), vm, preferred_element_type=F32))
        for g in range(Q_PER_KV):
            hq = h * Q_PER_KV + g
            o_ref[:, hq * HD:(hq + 1) * HD] = (o4[g * BLOCK:(g + 1) * BLOCK] / dens[g]).astype(o_ref.dtype)


def _attn_kernel(*refs):
    _attn_body(pl.program_id(1), *refs)


def _attention(main, sinks, bsz, nb):
    m = main.shape[0]
    kb, vb = K_COL // KV_DIM, V_COL // KV_DIM
    cur = lambda bb, i: bb * nb + i
    prev = lambda bb, i: bb * nb + jnp.maximum(i - 1, 0)
    first = lambda bb, i: bb * nb
    return pl.pallas_call(
        _attn_kernel,
        out_shape=jax.ShapeDtypeStruct((m, Q_DIM), BF16),
        grid=(bsz, nb),
        in_specs=[
            pl.BlockSpec(memory_space=pltpu.SMEM),
            pl.BlockSpec((BLOCK, Q_DIM), lambda bb, i: (cur(bb, i), Q_COL // Q_DIM)),
            pl.BlockSpec((BLOCK, KV_DIM), lambda bb, i: (cur(bb, i), kb)),
            pl.BlockSpec((BLOCK, KV_DIM), lambda bb, i: (cur(bb, i), vb)),
            pl.BlockSpec((BLOCK, KV_DIM), lambda bb, i: (prev(bb, i), kb)),
            pl.BlockSpec((BLOCK, KV_DIM), lambda bb, i: (prev(bb, i), vb)),
            pl.BlockSpec((BLOCK, KV_DIM), lambda bb, i: (first(bb, i), kb)),
            pl.BlockSpec((BLOCK, KV_DIM), lambda bb, i: (first(bb, i), vb)),
        ],
        out_specs=pl.BlockSpec((BLOCK, Q_DIM), lambda bb, i: (cur(bb, i), 0)),
        compiler_params=_cparams(("parallel", "parallel")),
        name="swa_attention",
    )(sinks, main, main, main, main, main, main, main)


CIN_ROWS = BLOCK + 8
CONV_SLABS = CONV_DIM // LANES
B_SLAB = D_INNER // LANES


def _ssd_body(i, hb_ref, wdt_ref, z_ref, xs_ref, bm_ref, cm_ref,
              cw_ref, cbias_ref, dtb_ref, a_ref, dsk_ref, ng_ref,
              y_ref,
              cin_ref, xc_ref, state_ref, yacc_ref, acs_ref, acst_ref, dtt_ref, wt_ref, ea_ref):
    row = lax.broadcasted_iota(jnp.int32, (BLOCK, 1), 0)
    live = jnp.where((i > 0) | (row >= PAD), 1.0, 0.0)

    @pl.when(i == 0)
    def _():
        state_ref[...] = jnp.zeros_like(state_ref)
        cin_ref[:, 0:8, :] = jnp.zeros((CONV_SLABS, 8, LANES), F32)

    @pl.when(i > 0)
    def _():
        cin_ref[:, 0:8, :] = cin_ref[:, BLOCK:BLOCK + 8, :]

    for sl in range(CONV_SLABS):
        cs = slice(sl * LANES, (sl + 1) * LANES)
        if sl < D_INNER // LANES:
            src = xs_ref[:, cs]
        elif sl < (D_INNER + 512) // LANES:
            src = bm_ref[:, sl * LANES - D_INNER:(sl + 1) * LANES - D_INNER]
        else:
            src = cm_ref[:, sl * LANES - D_INNER - 512:(sl + 1) * LANES - D_INNER - 512]
        cin_ref[sl, 8:CIN_ROWS, :] = src.astype(F32)
        acc = jnp.broadcast_to(cbias_ref[:, cs], (BLOCK, LANES))
        for j in range(CONV_W):
            acc = acc + cw_ref[j:j + 1, cs] * cin_ref[sl, pl.ds(8 - (CONV_W - 1) + j, BLOCK, stride=1), :]
        xc_ref[sl] = _silu(acc)

    @pl.when(i == 0)
    def _():
        xc_ref[:, 0:PAD, :] = jnp.zeros((CONV_SLABS, PAD, LANES), F32)

    dt_raw = lax.dot_general(hb_ref[...], wdt_ref[...], _NT, preferred_element_type=F32) + dtb_ref[...]
    dt = (jnp.maximum(dt_raw, 0.0) + jnp.log1p(jnp.exp(-jnp.abs(dt_raw)))) * live
    adt = dt * a_ref[...]
    rr = lax.broadcasted_iota(jnp.int32, (BLOCK, BLOCK), 0)
    cl = lax.broadcasted_iota(jnp.int32, (BLOCK, BLOCK), 1)
    causal = rr >= cl
    a_cs = jnp.dot(jnp.where(causal, 1.0, 0.0), adt, precision=HIGHEST, preferred_element_type=F32)
    a_cs_t = a_cs.T
    dt_t = dt.T
    last = a_cs_t[:, BLOCK - 1:BLOCK]
    acs_ref[...] = a_cs
    acst_ref[...] = a_cs_t
    dtt_ref[...] = dt_t
    wt_ref[...] = dt_t * jnp.exp(last - a_cs_t)
    ea_ref[...] = jnp.exp(a_cs)
    cd_t = jnp.exp(last)

    lane = lax.broadcasted_iota(jnp.int32, (1, LANES), 1)
    lo = lane < SSD_HEADDIM
    nn = (((1,), (1,)), ((), ()))
    for g in range(SSD_GROUPS):
        bg = xc_ref[B_SLAB + g]
        cg = xc_ref[B_SLAB + SSD_GROUPS + g]
        cb = lax.dot_general(cg.astype(BF16), bg.astype(BF16), nn, preferred_element_type=F32)
        bt = bg.T
        for pp in range(HEADS_PER_GROUP // 2):
            pr = g * (HEADS_PER_GROUP // 2) + pp
            ps = slice(pr * LANES, (pr + 1) * LANES)
            xs_pair = xc_ref[pr]
            st_pair = state_ref[:, ps]
            lhs, lhs_s = [], []
            for hd in (2 * pr, 2 * pr + 1):
                col = jnp.broadcast_to(acs_ref[:, hd:hd + 1], (BLOCK, BLOCK))
                seg = col - acst_ref[hd:hd + 1, :]
                dec = jnp.exp(jnp.where(causal, seg, NEG_INF))
                lhs.append((cb * dec * dtt_ref[hd:hd + 1, :]).astype(BF16))
                lhs.append((cg * jnp.broadcast_to(ea_ref[:, hd:hd + 1], (BLOCK, BLOCK))).astype(BF16))
                lhs_s.append((bt * wt_ref[hd:hd + 1, :]).astype(BF16))
            xs_lo = jnp.where(lo, xs_pair, 0.0).astype(BF16)
            xs_hi = jnp.where(lo, 0.0, xs_pair).astype(BF16)
            st_lo = jnp.where(lo, st_pair, 0.0).astype(BF16)
            st_hi = jnp.where(lo, 0.0, st_pair).astype(BF16)
            y_pair = jnp.dot(jnp.concatenate(lhs, axis=1),
                             jnp.concatenate([xs_lo, st_lo, xs_hi, st_hi], axis=0),
                             preferred_element_type=F32)
            yacc_ref[:, ps] = y_pair + dsk_ref[:, ps] * xs_pair
            st_new = jnp.dot(jnp.concatenate(lhs_s, axis=1),
                             jnp.concatenate([xs_lo, xs_hi], axis=0), preferred_element_type=F32)
            cd_row = jnp.where(lo, jnp.broadcast_to(cd_t[2 * pr:2 * pr + 1, :], (1, LANES)),
                               jnp.broadcast_to(cd_t[2 * pr + 1:2 * pr + 2, :], (1, LANES)))
            state_ref[:, ps] = st_pair * cd_row + st_new

    gw = D_INNER // SSD_GROUPS
    for g in range(SSD_GROUPS):
        gs = slice(g * gw, (g + 1) * gw)
        yz = yacc_ref[:, gs] * _silu(z_ref[:, gs].astype(F32))
        ms = jnp.mean(yz * yz, axis=-1, keepdims=True)
        y_ref[:, gs] = (yz * lax.rsqrt(ms + RMS_EPS) * ng_ref[:, gs]).astype(y_ref.dtype)


def _ssd_kernel(*refs):
    _ssd_body(pl.program_id(1), *refs)


def _ssd(hb, main, w_dt, conv_w, conv_b, dt_bias, a_neg, d_skip_x, norm_g, bsz, nb):
    m = hb.shape[0]
    cur = lambda bb, i: bb * nb + i
    full = lambda shape: pl.BlockSpec(shape, lambda bb, i: (0, 0))
    return pl.pallas_call(
        _ssd_kernel,
        out_shape=jax.ShapeDtypeStruct((m, D_INNER), BF16),
        grid=(bsz, nb),
        in_specs=[
            pl.BlockSpec((BLOCK, D_MODEL), lambda bb, i: (cur(bb, i), 0)),
            full((LANES, D_MODEL)),
            pl.BlockSpec((BLOCK, D_INNER), lambda bb, i: (cur(bb, i), Z_COL // D_INNER)),
            pl.BlockSpec((BLOCK, D_INNER), lambda bb, i: (cur(bb, i), XS_COL // D_INNER)),
            pl.BlockSpec((BLOCK, 512), lambda bb, i: (cur(bb, i), B_COL // 512)),
            pl.BlockSpec((BLOCK, 512), lambda bb, i: (cur(bb, i), C_COL // 512)),
            full((CONV_W, CONV_DIM)), full((1, CONV_DIM)), full((1, LANES)), full((1, LANES)),
            full((1, D_INNER)), full((1, D_INNER)),
        ],
        out_specs=pl.BlockSpec((BLOCK, D_INNER), lambda bb, i: (cur(bb, i), 0)),
        scratch_shapes=[
            pltpu.VMEM((CONV_SLABS, CIN_ROWS, LANES), F32),
            pltpu.VMEM((CONV_SLABS, BLOCK, LANES), F32),
            pltpu.VMEM((D_STATE, D_INNER), F32),
            pltpu.VMEM((BLOCK, D_INNER), F32),
            pltpu.VMEM((BLOCK, LANES), F32),
            pltpu.VMEM((LANES, BLOCK), F32),
            pltpu.VMEM((LANES, BLOCK), F32),
            pltpu.VMEM((LANES, BLOCK), F32),
            pltpu.VMEM((BLOCK, LANES), F32),
        ],
        compiler_params=_cparams(("arbitrary", "arbitrary")),
        name="ssd",
    )(hb, w_dt, main, main, main, main, conv_w, conv_b, dt_bias, a_neg, d_skip_x, norm_g)


def _merge_kernel(hb_ref, at_ref, y_ref, wga_ref, wgs_ref, wa_ref, ws_ref, o_ref,
                  wga_b, wgs_b, wa_b, ws_b):
    @pl.when(pl.program_id(1) == 0)
    def _():
        wga_b[...] = wga_ref[...].astype(BF16)
        wgs_b[...] = wgs_ref[...].astype(BF16)
        wa_b[...] = wa_ref[...].astype(BF16)
        ws_b[...] = ws_ref[...].astype(BF16)

    hb = hb_ref[...]
    ga = lax.dot_general(hb, wga_b[...], _NT, preferred_element_type=F32)
    gs = lax.dot_general(hb, wgs_b[...], _NT, preferred_element_type=F32)
    pa = jnp.dot(at_ref[...], wa_b[...], preferred_element_type=F32)
    ps = jnp.dot(y_ref[...], ws_b[...], preferred_element_type=F32)
    o_ref[...] = (_sigmoid(ga) * pa + _sigmoid(gs) * ps).astype(o_ref.dtype)


def _merge(hb, attn, y, w_gates, w_a, w_s):
    m = hb.shape[0]
    tm = _row_tile(m // BLOCK, 5)
    nt = D_MODEL // COL_TILE
    once = pl.Buffered(1)
    return pl.pallas_call(
        _merge_kernel,
        out_shape=jax.ShapeDtypeStruct((m, D_MODEL), BF16),
        grid=(nt, m // tm),
        in_specs=[
            pl.BlockSpec((tm, D_MODEL), lambda j, i: (i, 0)),
            pl.BlockSpec((tm, Q_DIM), lambda j, i: (i, 0)),
            pl.BlockSpec((tm, D_INNER), lambda j, i: (i, 0)),
            pl.BlockSpec((COL_TILE, D_MODEL), lambda j, i: (j, 0), pipeline_mode=once),
            pl.BlockSpec((COL_TILE, D_MODEL), lambda j, i: (j + nt, 0), pipeline_mode=once),
            pl.BlockSpec((Q_DIM, COL_TILE), lambda j, i: (0, j), pipeline_mode=once),
            pl.BlockSpec((D_INNER, COL_TILE), lambda j, i: (0, j), pipeline_mode=once),
        ],
        out_specs=pl.BlockSpec((tm, COL_TILE), lambda j, i: (i, j)),
        scratch_shapes=[
            pltpu.VMEM((COL_TILE, D_MODEL), BF16), pltpu.VMEM((COL_TILE, D_MODEL), BF16),
            pltpu.VMEM((Q_DIM, COL_TILE), BF16), pltpu.VMEM((D_INNER, COL_TILE), BF16),
        ],
        compiler_params=_cparams(("arbitrary", "arbitrary")),
        name="branch_merge",
    )(hb, attn, y, w_gates, w_gates, w_a, w_s)


def _outproj_router_kernel(mg_ref, h_ref, wo_ref, g_ref, b_ref, wrh_ref, wrl_ref, br_ref,
                           o_ref, ids_ref, wts_ref, cnt_ref, carry_ref, *, nb, tm):
    step = pl.program_id(0)

    @pl.when(step == 0)
    def _():
        carry_ref[...] = jnp.zeros_like(carry_ref)

    mix = jnp.dot(mg_ref[...], wo_ref[...], preferred_element_type=F32)
    h1 = _ln_rows(ALPHA * h_ref[...] + mix, g_ref[...], b_ref[...])
    o_ref[...] = h1

    h_hi = h1.astype(BF16)
    h_lo = (h1 - h_hi.astype(F32)).astype(BF16)
    logits = (lax.dot_general(h_hi, wrh_ref[...], _NT, preferred_element_type=F32)
              + (lax.dot_general(h_lo, wrh_ref[...], _NT, preferred_element_type=F32)
                 + lax.dot_general(h_hi, wrl_ref[...], _NT, preferred_element_type=F32))) + br_ref[...]
    lane = lax.broadcasted_iota(jnp.int32, (tm, LANES), 1)
    row = lax.broadcasted_iota(jnp.int32, (BLOCK, 1), 0)
    blocks = tm // BLOCK
    real = jnp.concatenate(
        [(((step * blocks + s) % nb) > 0) | (row >= PAD) for s in range(blocks)], axis=0)
    big = jnp.int32(LANES)

    is_g = lane < MOE_GROUPS
    gl = jnp.where(is_g, logits, -jnp.inf)
    ge = jnp.exp(gl - jnp.max(gl, axis=-1, keepdims=True))
    pg = ge / jnp.sum(ge, axis=-1, keepdims=True)
    p_top = jnp.max(pg, axis=-1, keepdims=True)
    g_idx = jnp.min(jnp.where(is_g & (pg == p_top), lane, big), axis=-1, keepdims=True)

    base = MOE_GROUPS + g_idx * EXPERTS_PER_GROUP
    sel = (lane >= base) & (lane < base + EXPERTS_PER_GROUP)
    el = jnp.where(sel, logits, -jnp.inf)
    ee = jnp.exp(el - jnp.max(el, axis=-1, keepdims=True))
    pe = ee / jnp.sum(ee, axis=-1, keepdims=True)
    v1 = jnp.max(jnp.where(sel, pe, -1.0), axis=-1, keepdims=True)
    i1 = jnp.min(jnp.where(sel & (pe == v1), lane, big), axis=-1, keepdims=True)
    sel2 = sel & (lane != i1)
    v2 = jnp.max(jnp.where(sel2, pe, -1.0), axis=-1, keepdims=True)
    i2 = jnp.min(jnp.where(sel2 & (pe == v2), lane, big), axis=-1, keepdims=True)
    vs = v1 + v2
    w1 = p_top * (v1 / vs)
    w2 = p_top * (v2 / vs)

    hit1 = (lane == i1) & real
    hit2 = (lane == i2) & real
    onehot = jnp.where(hit1 | hit2, 1.0, 0.0)
    rr = lax.broadcasted_iota(jnp.int32, (tm, tm), 0)
    cl = lax.broadcasted_iota(jnp.int32, (tm, tm), 1)
    before = jnp.where(rr > cl, 1.0, 0.0).astype(BF16)
    cum = jnp.dot(before, onehot.astype(BF16), preferred_element_type=F32) + carry_ref[...]
    r1 = jnp.sum(jnp.where(hit1, cum, 0.0), axis=-1, keepdims=True)
    r2 = jnp.sum(jnp.where(hit2, cum, 0.0), axis=-1, keepdims=True)
    carry_ref[...] = carry_ref[...] + jnp.sum(onehot, axis=0, keepdims=True)
    cnt_ref[...] = carry_ref[...].astype(jnp.int32)

    e1 = jnp.where(real, i1 - MOE_GROUPS, -1)
    e2 = jnp.where(real, i2 - MOE_GROUPS, -1)
    ids = jnp.where(lane == 0, e1, jnp.where(lane == 1, e2, jnp.where(
        lane == 2, r1.astype(jnp.int32), jnp.where(lane == 3, r2.astype(jnp.int32), 0))))
    ids_ref[...] = ids.T[0:8, :]
    l8 = lax.broadcasted_iota(jnp.int32, (tm, 8), 1)
    wts_ref[...] = jnp.where(l8 == 0, w1, jnp.where(l8 == 1, w2, 0.0))


def _outproj_router(merged, h0, w_o, g, b, wr_hi, wr_lo, b_r, nb):
    m = h0.shape[0]
    tm = _row_tile(m // BLOCK, 5)
    row_blk = lambda width: pl.BlockSpec((tm, width), lambda i: (i, 0))
    full = lambda shape: pl.BlockSpec(shape, lambda i: (0, 0))
    return pl.pallas_call(
        functools.partial(_outproj_router_kernel, nb=nb, tm=tm),
        out_shape=(jax.ShapeDtypeStruct((m, D_MODEL), F32),
                   jax.ShapeDtypeStruct((8, m), jnp.int32), jax.ShapeDtypeStruct((m, 8), F32),
                   jax.ShapeDtypeStruct((1, LANES), jnp.int32)),
        grid=(m // tm,),
        in_specs=[row_blk(D_MODEL), row_blk(D_MODEL), full((D_MODEL, D_MODEL)),
                  full((1, D_MODEL)), full((1, D_MODEL)),
                  full((LANES, D_MODEL)), full((LANES, D_MODEL)), full((1, LANES))],
        out_specs=(row_blk(D_MODEL), pl.BlockSpec((8, tm), lambda i: (0, i)), row_blk(8),
                   full((1, LANES))),
        scratch_shapes=[pltpu.VMEM((1, LANES), F32)],
        compiler_params=_cparams(("arbitrary",)),
        name="out_proj_ln1_router",
    )(merged, h0, w_o, g, b, wr_hi, wr_lo, b_r)


HALF = D_MODEL // 2
HI_MASK = 0xFFFF0000


def _pack_rows(x):
    lo = lax.bitcast_convert_type(x[:, :HALF].astype(BF16).astype(F32), jnp.uint32) >> 16
    hi = lax.bitcast_convert_type(x[:, HALF:].astype(BF16).astype(F32), jnp.uint32) & jnp.uint32(HI_MASK)
    return hi | lo


def _unpack_rows(w):
    lo = lax.bitcast_convert_type(w << 16, F32)
    hi = lax.bitcast_convert_type(w & jnp.uint32(HI_MASK), F32)
    return lo, hi


ROW_SUB = HALF // LANES


def _put_tile_rows(ref, lead, x):
    for s in range(ROW_SUB):
        ref[lead + (pl.ds(s, x.shape[0], stride=ROW_SUB), slice(None))] = x[:, s * LANES:(s + 1) * LANES]


def _get_tile_rows(ref, lead, r):
    return jnp.concatenate(
        [ref[lead + (pl.ds(s, r, stride=ROW_SUB), slice(None))] for s in range(ROW_SUB)], axis=1)


def _tile_rows(ref, row, n=1):
    return ref.at[pl.ds(pl.multiple_of(row * ROW_SUB, ROW_SUB), n * ROW_SUB), :]


def _dest_kernel(pstart_ref, ids_ref, dest_ref, *, n_slots):
    e = ids_ref[0:TOP_K, :]
    acc = jnp.zeros_like(e)
    for ex in range(N_EXPERTS):
        acc = jnp.where(e == ex, pstart_ref[ex], acc)
    k = lax.broadcasted_iota(jnp.int32, e.shape, 0)
    t = lax.broadcasted_iota(jnp.int32, e.shape, 1)
    dump = n_slots + k * PAD + (t & (BLOCK - 1))
    dest_ref[...] = jnp.where(e >= 0, acc + ids_ref[TOP_K:2 * TOP_K, :], dump)


def _dest_rows(pstart, ids, n_slots):
    m = ids.shape[1]
    return pl.pallas_call(
        functools.partial(_dest_kernel, n_slots=n_slots),
        out_shape=jax.ShapeDtypeStruct((TOP_K, m), jnp.int32),
        in_specs=[pl.BlockSpec(memory_space=pltpu.SMEM), pl.BlockSpec(memory_space=pltpu.VMEM)],
        out_specs=pl.BlockSpec(memory_space=pltpu.VMEM),
        name="moe_dest",
    )(pstart, ids)


def _dispatch_kernel(dest_ref, zlo_ref, zhi_ref, nused_ref, h_ref, xs_hbm, ring, zrow, zblk, sems, zsem,
                     *, n_blocks):
    step = pl.program_id(0)
    n_steps = pl.num_programs(0)
    n_rows = n_steps * BLOCK
    slot = step % 2

    def zero_rows(start):
        def per_expert(ex, carry):
            def per_row(p, c):
                cp = pltpu.make_async_copy(zrow, _tile_rows(xs_hbm, p), zsem)
                cp.start() if start else cp.wait()
                return c
            return lax.fori_loop(zlo_ref[ex], zhi_ref[ex], per_row, carry)
        lax.fori_loop(0, N_EXPERTS, per_expert, 0)

        def per_block(blk, c):
            cp = pltpu.make_async_copy(zblk, _tile_rows(xs_hbm, blk * MOE_ROWS, MOE_ROWS), zsem)
            cp.start() if start else cp.wait()
            return c
        lax.fori_loop(nused_ref[0], n_blocks, per_block, 0)

    @pl.when(step == 0)
    def _():
        zrow[...] = jnp.zeros_like(zrow)
        zblk[...] = jnp.zeros_like(zblk)
        zero_rows(True)

    _put_tile_rows(ring, (slot,), _pack_rows(h_ref[...]))
    base = step * BLOCK

    def body(t, carry):
        for k in range(TOP_K):
            d = dest_ref[k * n_rows + base + t]
            pltpu.make_async_copy(_tile_rows(ring.at[slot], t), _tile_rows(xs_hbm, d),
                                  sems.at[slot]).start(priority=k)
        return carry
    lax.fori_loop(0, BLOCK, body, 0, unroll=4)

    def wait_slot(s):
        for k in range(TOP_K):
            pltpu.make_async_copy(ring.at[s], _tile_rows(xs_hbm, 0, BLOCK), sems.at[s]).wait()

    @pl.when(step > 0)
    def _():
        wait_slot(1 - slot)

    @pl.when(step == n_steps - 1)
    def _():
        wait_slot(slot)
        zero_rows(False)


def _dispatch(h1, dest, zlo, zhi, n_used, n_blocks):
    m = h1.shape[0]
    n_slots = n_blocks * MOE_ROWS
    grid_spec = pltpu.PrefetchScalarGridSpec(
        num_scalar_prefetch=4,
        grid=(m // BLOCK,),
        in_specs=[pl.BlockSpec((BLOCK, D_MODEL), lambda i, *_: (i, 0))],
        out_specs=pl.BlockSpec(memory_space=pl.ANY),
        scratch_shapes=[
            pltpu.VMEM((2, BLOCK * ROW_SUB, LANES), jnp.uint32),
            pltpu.VMEM((ROW_SUB, LANES), jnp.uint32),
            pltpu.VMEM((MOE_ROWS * ROW_SUB, LANES), jnp.uint32),
            pltpu.SemaphoreType.DMA((2,)),
            pltpu.SemaphoreType.DMA(()),
        ],
    )
    return pl.pallas_call(
        functools.partial(_dispatch_kernel, n_blocks=n_blocks),
        out_shape=jax.ShapeDtypeStruct(((n_slots + TOP_K * PAD) * ROW_SUB, LANES), jnp.uint32),
        grid_spec=grid_spec,
        compiler_params=_cparams(("arbitrary",)),
        name="moe_dispatch",
    )(dest, zlo, zhi, n_used, h1)


def _expert_kernel(blk_e_ref, nused_ref, xs_ref, wg_ref, wu_ref, wd_ref, o_ref):
    i = pl.program_id(0)
    n_used = nused_ref[0]

    @pl.when(i < n_used)
    def _():
        lo, hi = _unpack_rows(_get_tile_rows(xs_ref, (), MOE_ROWS))
        x_lo, x_hi = lo.astype(BF16), hi.astype(BF16)
        gate = (jnp.dot(x_lo, wg_ref[0:HALF, :].astype(BF16), preferred_element_type=F32)
                + jnp.dot(x_hi, wg_ref[HALF:D_MODEL, :].astype(BF16), preferred_element_type=F32))
        up = (jnp.dot(x_lo, wu_ref[0:HALF, :].astype(BF16), preferred_element_type=F32)
              + jnp.dot(x_hi, wu_ref[HALF:D_MODEL, :].astype(BF16), preferred_element_type=F32))
        hid = (_silu(gate) * up).astype(BF16)
        _put_tile_rows(o_ref, (), _pack_rows(jnp.dot(hid, wd_ref[...].astype(BF16),
                                                     preferred_element_type=F32)))

    @pl.when(i >= n_used)
    def _():
        o_ref[...] = jnp.zeros_like(o_ref)


def _experts(xs, w_gate, w_up, w_down, blk_e, n_used, n_blocks):
    used = lambda i, nu: jnp.minimum(i, nu[0] - 1)
    grid_spec = pltpu.PrefetchScalarGridSpec(
        num_scalar_prefetch=2,
        grid=(n_blocks,),
        in_specs=[
            pl.BlockSpec((MOE_ROWS * ROW_SUB, LANES), lambda i, be, nu: (used(i, nu), 0)),
            pl.BlockSpec((None, None, D_MODEL, D_EXPERT), lambda i, be, nu: (0, be[i], 0, 0)),
            pl.BlockSpec((None, None, D_MODEL, D_EXPERT), lambda i, be, nu: (0, be[i], 0, 0)),
            pl.BlockSpec((None, None, D_EXPERT, D_MODEL), lambda i, be, nu: (0, be[i], 0, 0)),
        ],
        out_specs=pl.BlockSpec((MOE_ROWS * ROW_SUB, LANES), lambda i, be, nu: (i, 0)),
    )
    return pl.pallas_call(
        _expert_kernel,
        out_shape=jax.ShapeDtypeStruct((n_blocks * MOE_ROWS * ROW_SUB, LANES), jnp.uint32),
        grid_spec=grid_spec,
        compiler_params=_cparams(("arbitrary",)),
        name="moe_experts",
    )(blk_e, n_used, xs, w_gate, w_up, w_down)


def _combine_kernel(dest_ref, h_ref, wts_ref, yb_hbm, g_ref, b_ref, o_ref, ybuf, sems, *, nb):
    bb = pl.program_id(0)
    i = pl.program_id(1)
    n_i = pl.num_programs(1)
    step = bb * n_i + i
    n_steps = pl.num_programs(0) * n_i

    n_rows = pl.num_programs(0) * nb * BLOCK

    def assign_base(b_, i_):
        return (b_ * nb + i_ + 1) * BLOCK

    def start_gather(base, slot):
        def body(t, carry):
            for k in range(TOP_K):
                d = dest_ref[k * n_rows + base + t]
                pltpu.make_async_copy(_tile_rows(yb_hbm, d), _tile_rows(ybuf.at[slot, k], t),
                                      sems.at[slot]).start(priority=k)
            return carry
        lax.fori_loop(0, BLOCK, body, 0, unroll=4)

    def wait_gather(slot):
        for k in range(TOP_K):
            pltpu.make_async_copy(_tile_rows(yb_hbm, 0, BLOCK), ybuf.at[slot, k], sems.at[slot]).wait()

    @pl.when(step == 0)
    def _():
        start_gather(assign_base(0, 0), 0)

    @pl.when(step + 1 < n_steps)
    def _():
        nxt = jnp.where(i + 1 < n_i, assign_base(bb, i + 1), assign_base(bb + 1, 0))
        start_gather(nxt, (step + 1) % 2)

    slot = step % 2
    wait_gather(slot)
    wts = wts_ref[...]
    lo0, hi0 = _unpack_rows(_get_tile_rows(ybuf, (slot, 0), BLOCK))
    lo1, hi1 = _unpack_rows(_get_tile_rows(ybuf, (slot, 1), BLOCK))
    w0, w1 = wts[:, 0:1], wts[:, 1:2]
    r_lo = ALPHA * h_ref[:, 0:HALF] + (lo0 * w0 + lo1 * w1)
    r_hi = ALPHA * h_ref[:, HALF:D_MODEL] + (hi0 * w0 + hi1 * w1)
    mu = (jnp.sum(r_lo, axis=-1, keepdims=True) + jnp.sum(r_hi, axis=-1, keepdims=True)) / D_MODEL
    c_lo, c_hi = r_lo - mu, r_hi - mu
    var = (jnp.sum(c_lo * c_lo, axis=-1, keepdims=True)
           + jnp.sum(c_hi * c_hi, axis=-1, keepdims=True)) / D_MODEL
    inv = lax.rsqrt(var + LN_EPS)
    o_ref[:, 0:HALF] = c_lo * inv * g_ref[:, 0:HALF] + b_ref[:, 0:HALF]
    o_ref[:, HALF:D_MODEL] = c_hi * inv * g_ref[:, HALF:D_MODEL] + b_ref[:, HALF:D_MODEL]


def _combine_ln(h1, wts, yb, dest, g, b, bsz, nb):
    grid_spec = pltpu.PrefetchScalarGridSpec(
        num_scalar_prefetch=1,
        grid=(bsz, nb - 1),
        in_specs=[
            pl.BlockSpec((BLOCK, D_MODEL), lambda bb, i, d: (bb * nb + i + 1, 0)),
            pl.BlockSpec((BLOCK, 8), lambda bb, i, d: (bb * nb + i + 1, 0)),
            pl.BlockSpec(memory_space=pl.ANY),
            pl.BlockSpec((1, D_MODEL), lambda bb, i, d: (0, 0)),
            pl.BlockSpec((1, D_MODEL), lambda bb, i, d: (0, 0)),
        ],
        out_specs=pl.BlockSpec((None, BLOCK, D_MODEL), lambda bb, i, d: (bb, i, 0)),
        scratch_shapes=[
            pltpu.VMEM((2, TOP_K, BLOCK * ROW_SUB, LANES), jnp.uint32),
            pltpu.SemaphoreType.DMA((2,)),
        ],
    )
    return pl.pallas_call(
        functools.partial(_combine_kernel, nb=nb),
        out_shape=jax.ShapeDtypeStruct((bsz, (nb - 1) * BLOCK, D_MODEL), F32),
        grid_spec=grid_spec,
        compiler_params=_cparams(("arbitrary", "arbitrary")),
        name="moe_combine_ln2",
    )(dest, h1, wts, yb, g, b)


def _dispatch_tables(ids, counts, n_blocks):
    counts = counts[0, MOE_GROUPS:MOE_GROUPS + N_EXPERTS]
    pcounts = (counts + MOE_ROWS - 1) // MOE_ROWS * MOE_ROWS
    pend = jnp.cumsum(pcounts)
    pstart = (pend - pcounts).astype(jnp.int32)
    n_used = jnp.maximum(pend[-1] // MOE_ROWS, 1).astype(jnp.int32)
    blk = jnp.arange(n_blocks, dtype=jnp.int32)
    blk_e = jnp.minimum(jnp.sum(pend[None, :] <= (blk * MOE_ROWS)[:, None], axis=1), N_EXPERTS - 1)
    blk_e = jnp.where(blk < n_used, blk_e, blk_e[n_used - 1]).astype(jnp.int32)
    dest = _dest_rows(pstart, ids, n_blocks * MOE_ROWS).reshape(-1)
    return dest, pstart, (pstart + counts).astype(jnp.int32), pend.astype(jnp.int32), blk_e, n_used.reshape(1)


def kernel(x, meta_tokens, ln_emb_g, ln_emb_b, w_in, conv_w, conv_b, dt_bias, a_log, d_skip, ssd_norm_g, sinks, w_br_attn, w_br_ssd, w_o, ln1_g, ln1_b, w_router_group, b_router_group, w_router_expert, b_router_expert, w_gate, w_up, w_down, ln2_g, ln2_b):
    bsz, seq, d = x.shape
    assert d == D_MODEL and seq % BLOCK == 0 and w_in.shape[0] == DEPTH
    nb = seq // BLOCK + 1
    m = bsz * nb * BLOCK
    row2 = lambda v: v.reshape(1, -1).astype(F32)

    meta_pad = jnp.concatenate([jnp.zeros((PAD, d), F32), meta_tokens.astype(F32)], axis=0)
    h0, h0b = _embed_ln(x, meta_pad, row2(ln_emb_g), row2(ln_emb_b))

    w_in_t = jnp.swapaxes(w_in, 1, 2)
    main = _inproj(h0b, w_in_t)
    w_dt = jnp.pad(w_in_t[0, DT_OFF:DT_OFF + SSD_HEADS], ((0, LANES - SSD_HEADS), (0, 0))).astype(BF16)
    pad_h = lambda v: jnp.pad(v.astype(F32), (0, LANES - SSD_HEADS)).reshape(1, LANES)
    attn = _attention(main, sinks[0].astype(F32), bsz, nb)
    y = _ssd(h0b, main, w_dt, conv_w[0].astype(F32), row2(conv_b[0]), pad_h(dt_bias[0]),
             pad_h(-jnp.exp(a_log[0].astype(F32))), row2(jnp.repeat(d_skip[0], SSD_HEADDIM)),
             row2(ssd_norm_g[0]), bsz, nb)
    merged = _merge(h0b, attn, y, w_in_t[0, GATE_OFF:], w_br_attn[0], w_br_ssd[0])
    w_r = jnp.pad(jnp.concatenate([w_router_group[0].T, w_router_expert[0].T], axis=0).astype(F32),
                  ((0, LANES - MOE_GROUPS - N_EXPERTS), (0, 0)))
    wr_hi = w_r.astype(BF16)
    wr_lo = (w_r - wr_hi.astype(F32)).astype(BF16)
    b_r = jnp.pad(jnp.concatenate([b_router_group[0], b_router_expert[0]]).astype(F32),
                  (0, LANES - MOE_GROUPS - N_EXPERTS)).reshape(1, LANES)
    h1, ids, wts, counts = _outproj_router(merged, h0, w_o[0].astype(BF16), row2(ln1_g[0]),
                                           row2(ln1_b[0]), wr_hi, wr_lo, b_r, nb)

    n_assign = bsz * (seq + N_META) * TOP_K
    n_blocks = -(-n_assign // MOE_ROWS) + N_EXPERTS
    dest, _, zlo, zhi, blk_e, n_used = _dispatch_tables(ids, counts, n_blocks)
    xs = _dispatch(h1, dest, zlo, zhi, n_used, n_blocks)
    yb = _experts(xs, w_gate, w_up, w_down, blk_e, n_used, n_blocks)
    return _combine_ln(h1, wts, yb, dest, row2(ln2_g[0]), row2(ln2_b[0]), bsz, nb)
```

```python
import functools

import jax
import jax.numpy as jnp
from jax import lax
from jax.experimental import pallas as pl
from jax.experimental.pallas import tpu as pltpu

F32 = jnp.float32
BF16 = jnp.bfloat16
HIGHEST = lax.Precision.HIGHEST

D_MODEL = 2048
N_META = 16
BLOCK = 128
PAD = BLOCK - N_META
WINDOW = 128
HQ, HKV, HD = 16, 4, 64
Q_PER_KV = HQ // HKV
D_INNER = 2048
SSD_HEADDIM = 64
SSD_HEADS = D_INNER // SSD_HEADDIM
SSD_GROUPS = 4
HEADS_PER_GROUP = SSD_HEADS // SSD_GROUPS
D_STATE = 128
CONV_W = 4
CONV_DIM = D_INNER + 2 * SSD_GROUPS * D_STATE
MOE_GROUPS = 8
EXPERTS_PER_GROUP = 8
N_EXPERTS = MOE_GROUPS * EXPERTS_PER_GROUP
TOP_K = 2
D_EXPERT = 512
LN_EPS = 1e-5
RMS_EPS = 1e-5
NEG_INF = -1e30
DEPTH = 1
ALPHA = (2.0 * DEPTH) ** 0.25

Q_DIM = HQ * HD
KV_DIM = HKV * HD
MAIN_DIM = Q_DIM + 2 * KV_DIM + D_INNER + CONV_DIM
DT_OFF = MAIN_DIM
GATE_OFF = MAIN_DIM + SSD_HEADS
COL_TILE = 512
MAIN_TILES = MAIN_DIM // COL_TILE
Z_COL, XS_COL, Q_COL, K_COL, V_COL, B_COL, C_COL = 0, 2048, 4096, 5120, 5376, 5632, 6144

LANES = 128
MOE_ROWS = 576
VMEM_LIMIT = 56 * 1024 * 1024


def _cparams(sem, vmem=VMEM_LIMIT):
    return pltpu.CompilerParams(dimension_semantics=sem, vmem_limit_bytes=vmem)


def _ln_rows(x, g, b):
    mu = jnp.mean(x, axis=-1, keepdims=True)
    xc = x - mu
    var = jnp.mean(xc * xc, axis=-1, keepdims=True)
    return xc * lax.rsqrt(var + LN_EPS) * g + b


def _sigmoid(x):
    return 1.0 / (1.0 + jnp.exp(-x))


def _silu(x):
    return x * _sigmoid(x)


def _row_tile(n_blocks, max_blocks):
    best = 1
    for c in range(1, max_blocks + 1):
        if n_blocks % c == 0:
            best = c
    return best * BLOCK


def _embed_ln_kernel(x_ref, meta_ref, g_ref, b_ref, h_ref, hb_ref):
    i = pl.program_id(1)
    src = jnp.where(i == 0, meta_ref[...], x_ref[...])
    y = _ln_rows(src, g_ref[...], b_ref[...])
    row = lax.broadcasted_iota(jnp.int32, (BLOCK, 1), 0)
    y = jnp.where((i > 0) | (row >= PAD), y, 0.0)
    h_ref[...] = y
    hb_ref[...] = y.astype(BF16)


def _embed_ln(x, meta_pad, g, b):
    bsz, seq, d = x.shape
    nb = seq // BLOCK + 1
    m = bsz * nb * BLOCK
    return pl.pallas_call(
        _embed_ln_kernel,
        out_shape=(jax.ShapeDtypeStruct((m, d), F32), jax.ShapeDtypeStruct((m, d), BF16)),
        grid=(bsz, nb),
        in_specs=[
            pl.BlockSpec((None, BLOCK, d), lambda bb, i: (bb, jnp.maximum(i - 1, 0), 0)),
            pl.BlockSpec((BLOCK, d), lambda bb, i: (0, 0)),
            pl.BlockSpec((1, d), lambda bb, i: (0, 0)),
            pl.BlockSpec((1, d), lambda bb, i: (0, 0)),
        ],
        out_specs=(pl.BlockSpec((BLOCK, d), lambda bb, i: (bb * nb + i, 0)),
                   pl.BlockSpec((BLOCK, d), lambda bb, i: (bb * nb + i, 0))),
        compiler_params=_cparams(("parallel", "parallel")),
        name="embed_ln",
    )(x, meta_pad, g, b)


_NT = (((1,), (1,)), ((), ()))


def _inproj_kernel(a_ref, wt_ref, o_ref):
    o_ref[...] = lax.dot_general(a_ref[...], wt_ref[...].astype(BF16), _NT,
                                 preferred_element_type=F32).astype(o_ref.dtype)


def _main_dest_tile(j):
    return jnp.where(j < 2, j + 8, jnp.where(j == 2, 10, jnp.where(j < 11, j - 3, j)))


def _inproj(hb, w_in_t):
    m, d = hb.shape
    tm = _row_tile(m // BLOCK, 13)
    return pl.pallas_call(
        _inproj_kernel,
        out_shape=jax.ShapeDtypeStruct((m, MAIN_DIM), BF16),
        grid=(m // tm, MAIN_TILES),
        in_specs=[
            pl.BlockSpec((tm, d), lambda i, j: (i, 0)),
            pl.BlockSpec((None, COL_TILE, d), lambda i, j: (0, j, 0)),
        ],
        out_specs=pl.BlockSpec((tm, COL_TILE), lambda i, j: (i, _main_dest_tile(j))),
        compiler_params=_cparams(("parallel", "arbitrary")),
        name="in_proj",
    )(hb, w_in_t)


def _attn_body(n, sinks_ref, q_ref, kc_ref, vc_ref, kp_ref, vp_ref, km_ref, vm_ref, o_ref, fill):
    r = lax.broadcasted_iota(jnp.int32, (BLOCK, BLOCK), 0)
    c = lax.broadcasted_iota(jnp.int32, (BLOCK, BLOCK), 1)
    is_cur = c <= r
    ok_band = (is_cur & (n >= 1)) | ((c > r) & (n >= 2))
    rm = lax.broadcasted_iota(jnp.int32, (BLOCK, N_META), 0)
    cm = lax.broadcasted_iota(jnp.int32, (BLOCK, N_META), 1)
    ok_meta = (n > 0) | (cm <= rm - PAD)
    scale = HD ** -0.5
    nt = (((1,), (1,)), ((), ()))
    for h in range(HKV):
        ks = slice(h * HD, (h + 1) * HD)
        kc, kp, km = kc_ref[:, ks], kp_ref[:, ks], km_ref[PAD:BLOCK, ks]
        vc, vp, vm = vc_ref[:, ks], vp_ref[:, ks], vm_ref[PAD:BLOCK, ks]
        q4 = jnp.concatenate(
            [q_ref[:, (h * Q_PER_KV + g) * HD:(h * Q_PER_KV + g + 1) * HD] for g in range(Q_PER_KV)],
            axis=0)
        s_c4 = lax.dot_general(q4, kc, nt, preferred_element_type=F32)
        s_p4 = lax.dot_general(q4, kp, nt, preferred_element_type=F32)
        s_m4 = lax.dot_general(q4, km, nt, preferred_element_type=F32)
        pc, pp, pm, dens = [], [], [], []
        for g in range(Q_PER_KV):
            rows = slice(g * BLOCK, (g + 1) * BLOCK)
            sink = sinks_ref[h * Q_PER_KV + g]
            s_b = jnp.where(ok_band, jnp.where(is_cur, s_c4[rows], s_p4[rows]) * scale, NEG_INF)
            s_m = jnp.where(ok_meta, s_m4[rows] * scale, NEG_INF)
            mx = jnp.maximum(jnp.maximum(jnp.max(s_b, axis=-1, keepdims=True),
                                         jnp.max(s_m, axis=-1, keepdims=True)), sink)
            p_b = jnp.exp(s_b - mx)
            p_m = jnp.exp(s_m - mx)
            dens.append(jnp.sum(p_b, axis=-1, keepdims=True) + jnp.sum(p_m, axis=-1, keepdims=True)
                        + jnp.exp(sink - mx))
            pc.append(jnp.where(is_cur, p_b, 0.0).astype(BF16))
            pp.append(jnp.where(is_cur, 0.0, p_b).astype(BF16))
            pm.append(p_m.astype(BF16))
        o4 = (jnp.dot(jnp.concatenate(pc, axis=0), vc, preferred_element_type=F32)
              + jnp.dot(jnp.concatenate(pp, axis=0), vp, preferred_element_type=F32)
              + jnp.dot(jnp.concatenate(pm, axis=0), vm, preferred_element_type=F32))
        for g in range(Q_PER_KV):
            hq = h * Q_PER_KV + g
            o_ref[:, hq * HD:(hq + 1) * HD] = (o4[g * BLOCK:(g + 1) * BLOCK] / dens[g]).astype(o_ref.dtype)
        fill(1000)


CIN_ROWS = BLOCK + 8
CONV_SLABS = CONV_DIM // LANES
B_SLAB = D_INNER // LANES


def _ssd_body(i, hb_ref, wdt_ref, z_ref, xs_ref, bm_ref, cm_ref,
              cw_ref, cbias_ref, dtb_ref, a_ref, dsk_ref, ng_ref,
              y_ref,
              cin_ref, xc_ref, state_ref, yacc_ref, acs_ref, acst_ref, dtt_ref, wt_ref, ea_ref, fill):
    row = lax.broadcasted_iota(jnp.int32, (BLOCK, 1), 0)
    keep = (i > 0) | (row >= PAD)
    live = jnp.where(keep, 1.0, 0.0)

    @pl.when(i == 0)
    def _():
        state_ref[...] = jnp.zeros_like(state_ref)
        cin_ref[:, 0:8, :] = jnp.zeros((CONV_SLABS, 8, LANES), F32)

    @pl.when(i > 0)
    def _():
        cin_ref[:, 0:8, :] = cin_ref[:, BLOCK:BLOCK + 8, :]

    for sl in range(CONV_SLABS):
        cs = slice(sl * LANES, (sl + 1) * LANES)
        if sl < D_INNER // LANES:
            src = xs_ref[:, cs]
        elif sl < (D_INNER + 512) // LANES:
            src = bm_ref[:, sl * LANES - D_INNER:(sl + 1) * LANES - D_INNER]
        else:
            src = cm_ref[:, sl * LANES - D_INNER - 512:(sl + 1) * LANES - D_INNER - 512]
        cin_ref[sl, 8:CIN_ROWS, :] = src.astype(F32)
        acc = jnp.broadcast_to(cbias_ref[:, cs], (BLOCK, LANES))
        for j in range(CONV_W):
            acc = acc + cw_ref[j:j + 1, cs] * cin_ref[sl, pl.ds(8 - (CONV_W - 1) + j, BLOCK, stride=1), :]
        xc_ref[sl] = jnp.where(keep, _silu(acc), 0.0)
        fill(90)

    dt_raw = lax.dot_general(hb_ref[...], wdt_ref[...], _NT, preferred_element_type=F32) + dtb_ref[...]
    dt = (jnp.maximum(dt_raw, 0.0) + jnp.log1p(jnp.exp(-jnp.abs(dt_raw)))) * live
    adt = dt * a_ref[...]
    rr = lax.broadcasted_iota(jnp.int32, (BLOCK, BLOCK), 0)
    cl = lax.broadcasted_iota(jnp.int32, (BLOCK, BLOCK), 1)
    causal = rr >= cl
    a_cs = jnp.dot(jnp.where(causal, 1.0, 0.0), adt, precision=HIGHEST, preferred_element_type=F32)
    a_cs_t = a_cs.T
    dt_t = dt.T
    last = a_cs_t[:, BLOCK - 1:BLOCK]
    acs_ref[...] = a_cs
    acst_ref[...] = a_cs_t
    dtt_ref[...] = dt_t
    wt_ref[...] = dt_t * jnp.exp(last - a_cs_t)
    ea_ref[...] = jnp.exp(a_cs)
    cd_t = jnp.exp(last)

    lane = lax.broadcasted_iota(jnp.int32, (1, LANES), 1)
    lo = lane < SSD_HEADDIM
    nn = (((1,), (1,)), ((), ()))
    for g in range(SSD_GROUPS):
        bg = xc_ref[B_SLAB + g]
        cg = xc_ref[B_SLAB + SSD_GROUPS + g]
        cb = lax.dot_general(cg.astype(BF16), bg.astype(BF16), nn, preferred_element_type=F32)
        bt = bg.T
        for pp in range(HEADS_PER_GROUP // 2):
            pr = g * (HEADS_PER_GROUP // 2) + pp
            ps = slice(pr * LANES, (pr + 1) * LANES)
            xs_pair = xc_ref[pr]
            st_pair = state_ref[:, ps]
            lhs, lhs_s = [], []
            for hd in (2 * pr, 2 * pr + 1):
                col = jnp.broadcast_to(acs_ref[:, hd:hd + 1], (BLOCK, BLOCK))
                seg = col - acst_ref[hd:hd + 1, :]
                dec = jnp.exp(jnp.where(causal, seg, NEG_INF))
                lhs.append((cb * dec * dtt_ref[hd:hd + 1, :]).astype(BF16))
                lhs.append((cg * jnp.broadcast_to(ea_ref[:, hd:hd + 1], (BLOCK, BLOCK))).astype(BF16))
                lhs_s.append((bt * wt_ref[hd:hd + 1, :]).astype(BF16))
            xs_lo = jnp.where(lo, xs_pair, 0.0).astype(BF16)
            xs_hi = jnp.where(lo, 0.0, xs_pair).astype(BF16)
            st_lo = jnp.where(lo, st_pair, 0.0).astype(BF16)
            st_hi = jnp.where(lo, 0.0, st_pair).astype(BF16)
            y_pair = jnp.dot(jnp.concatenate(lhs, axis=1),
                             jnp.concatenate([xs_lo, st_lo, xs_hi, st_hi], axis=0),
                             preferred_element_type=F32)
            yacc_ref[:, ps] = y_pair + dsk_ref[:, ps] * xs_pair
            st_new = jnp.dot(jnp.concatenate(lhs_s, axis=1),
                             jnp.concatenate([xs_lo, xs_hi], axis=0), preferred_element_type=F32)
            cd_row = jnp.where(lo, jnp.broadcast_to(cd_t[2 * pr:2 * pr + 1, :], (1, LANES)),
                               jnp.broadcast_to(cd_t[2 * pr + 1:2 * pr + 2, :], (1, LANES)))
            state_ref[:, ps] = st_pair * cd_row + st_new
            fill(300)

    gw = D_INNER // SSD_GROUPS
    for g in range(SSD_GROUPS):
        gs = slice(g * gw, (g + 1) * gw)
        yz = yacc_ref[:, gs] * _silu(z_ref[:, gs].astype(F32))
        ms = jnp.mean(yz * yz, axis=-1, keepdims=True)
        y_ref[:, gs] = (yz * lax.rsqrt(ms + RMS_EPS) * ng_ref[:, gs]).astype(y_ref.dtype)


MERGE_TILES = D_MODEL // COL_TILE
MERGE_CHUNK = 256
MIX_VPU_COST = 24 * 90 + 16 * 300 + 4 * 1000


def _mixer_kernel(sinks_ref, q_ref, kc_ref, vc_ref, kp_ref, vp_ref, km_ref, vm_ref,
                  hb_ref, wdt_ref, z_ref, xs_ref, bm_ref, cm_ref,
                  cw_ref, cbias_ref, dtb_ref, a_ref, dsk_ref, ng_ref,
                  hbt_ref, wga_ref, wgs_ref, wa_ref, ws_ref,
                  o_ref,
                  cin_ref, xc_ref, state_ref, yacc_ref, acs_ref, acst_ref, dtt_ref, wt_ref, ea_ref,
                  cur_at, cur_y, prev_at, prev_y, *, nb, n_blk, tile_blocks):
    s = pl.program_id(0)
    n = jnp.minimum(s, n_blk - 1) % nb
    j = s % tile_blocks

    @pl.when(s == 0)
    def _():
        cur_at[...] = jnp.zeros_like(cur_at)
        cur_y[...] = jnp.zeros_like(cur_y)

    @pl.when(j == 0)
    def _():
        prev_at[...] = cur_at[...]
        prev_y[...] = cur_y[...]

    parts = {}

    def merge_piece(c, what):
        cs = slice(c * MERGE_CHUNK, (c + 1) * MERGE_CHUNK)
        if what == "ga":
            parts[c, what] = lax.dot_general(hbt_ref[...], wga_ref[cs, :], _NT, preferred_element_type=F32)
        elif what == "gs":
            parts[c, what] = lax.dot_general(hbt_ref[...], wgs_ref[cs, :], _NT, preferred_element_type=F32)
        elif what == "pa":
            parts[c, what] = jnp.dot(prev_at[...], wa_ref[:, cs], preferred_element_type=F32)
        elif what == "ps":
            parts[c, what] = jnp.dot(prev_y[...], ws_ref[:, cs], preferred_element_type=F32)
        else:
            o_ref[:, cs] = (_sigmoid(parts.pop((c, "ga"))) * parts.pop((c, "pa"))
                            + _sigmoid(parts.pop((c, "gs"))) * parts.pop((c, "ps"))).astype(o_ref.dtype)

    pieces = [(c, what) for c in range(COL_TILE // MERGE_CHUNK) for what in ("ga", "pa", "gs", "ps", "gate")]
    per_piece = MIX_VPU_COST / len(pieces)
    credit = [0.0]

    def fill(cost):
        credit[0] += cost
        while pieces and credit[0] >= per_piece:
            credit[0] -= per_piece
            merge_piece(*pieces.pop(0))

    rows = pl.ds(pl.multiple_of(j * BLOCK, BLOCK), BLOCK)
    _ssd_body(n, hb_ref, wdt_ref, z_ref, xs_ref, bm_ref, cm_ref,
              cw_ref, cbias_ref, dtb_ref, a_ref, dsk_ref, ng_ref,
              cur_y.at[rows, :],
              cin_ref, xc_ref, state_ref, yacc_ref, acs_ref, acst_ref, dtt_ref, wt_ref, ea_ref, fill)
    _attn_body(n, sinks_ref, q_ref, kc_ref, vc_ref, kp_ref, vp_ref, km_ref, vm_ref, cur_at.at[rows, :], fill)
    fill(MIX_VPU_COST)


def _mixer(hb, main, sinks, w_dt, conv_w, conv_b, dt_bias, a_neg, d_skip_x, norm_g,
           w_gates, w_a, w_s, bsz, nb):
    m = hb.shape[0]
    n_blk = bsz * nb
    tile_blocks = _row_tile(n_blk, 5) // BLOCK
    assert tile_blocks >= MERGE_TILES
    tm = tile_blocks * BLOCK
    kb, vb = K_COL // KV_DIM, V_COL // KV_DIM
    blk = lambda s: jnp.minimum(s, n_blk - 1)
    first = lambda s: blk(s) // nb * nb
    prev = lambda s: first(s) + jnp.maximum(blk(s) % nb - 1, 0)
    n_tiles = n_blk // tile_blocks
    mtile = lambda s: jnp.maximum(s // tile_blocks - 1, 0)
    otile = lambda s: jnp.where(s < tile_blocks, n_tiles, s // tile_blocks - 1)
    mcol = lambda s: jnp.minimum(s % tile_blocks, MERGE_TILES - 1)
    full = lambda shape: pl.BlockSpec(shape, lambda s: (0, 0))
    return pl.pallas_call(
        functools.partial(_mixer_kernel, nb=nb, n_blk=n_blk, tile_blocks=tile_blocks),
        out_shape=jax.ShapeDtypeStruct((m + tm, D_MODEL), BF16),
        grid=(n_blk + tile_blocks,),
        in_specs=[
            pl.BlockSpec(memory_space=pltpu.SMEM),
            pl.BlockSpec((BLOCK, Q_DIM), lambda s: (blk(s), Q_COL // Q_DIM)),
            pl.BlockSpec((BLOCK, KV_DIM), lambda s: (blk(s), kb)),
            pl.BlockSpec((BLOCK, KV_DIM), lambda s: (blk(s), vb)),
            pl.BlockSpec((BLOCK, KV_DIM), lambda s: (prev(s), kb)),
            pl.BlockSpec((BLOCK, KV_DIM), lambda s: (prev(s), vb)),
            pl.BlockSpec((BLOCK, KV_DIM), lambda s: (first(s), kb)),
            pl.BlockSpec((BLOCK, KV_DIM), lambda s: (first(s), vb)),
            pl.BlockSpec((BLOCK, D_MODEL), lambda s: (blk(s), 0)),
            full((LANES, D_MODEL)),
            pl.BlockSpec((BLOCK, D_INNER), lambda s: (blk(s), Z_COL // D_INNER)),
            pl.BlockSpec((BLOCK, D_INNER), lambda s: (blk(s), XS_COL // D_INNER)),
            pl.BlockSpec((BLOCK, 512), lambda s: (blk(s), B_COL // 512)),
            pl.BlockSpec((BLOCK, 512), lambda s: (blk(s), C_COL // 512)),
            full((CONV_W, CONV_DIM)), full((1, CONV_DIM)), full((1, LANES)), full((1, LANES)),
            full((1, D_INNER)), full((1, D_INNER)),
            pl.BlockSpec((tm, D_MODEL), lambda s: (mtile(s), 0)),
            pl.BlockSpec((COL_TILE, D_MODEL), lambda s: (mcol(s), 0)),
            pl.BlockSpec((COL_TILE, D_MODEL), lambda s: (mcol(s) + MERGE_TILES, 0)),
            pl.BlockSpec((Q_DIM, COL_TILE), lambda s: (0, mcol(s))),
            pl.BlockSpec((D_INNER, COL_TILE), lambda s: (0, mcol(s))),
        ],
        out_specs=pl.BlockSpec((tm, COL_TILE), lambda s: (otile(s), mcol(s))),
        scratch_shapes=[
            pltpu.VMEM((CONV_SLABS, CIN_ROWS, LANES), F32),
            pltpu.VMEM((CONV_SLABS, BLOCK, LANES), F32),
            pltpu.VMEM((D_STATE, D_INNER), F32),
            pltpu.VMEM((BLOCK, D_INNER), F32),
            pltpu.VMEM((BLOCK, LANES), F32),
            pltpu.VMEM((LANES, BLOCK), F32),
            pltpu.VMEM((LANES, BLOCK), F32),
            pltpu.VMEM((LANES, BLOCK), F32),
            pltpu.VMEM((BLOCK, LANES), F32),
            pltpu.VMEM((tm, Q_DIM), BF16),
            pltpu.VMEM((tm, D_INNER), BF16),
            pltpu.VMEM((tm, Q_DIM), BF16),
            pltpu.VMEM((tm, D_INNER), BF16),
        ],
        compiler_params=_cparams(("arbitrary",)),
        name="mixer",
    )(sinks, main, main, main, main, main, main, main,
      hb, w_dt, main, main, main, main, conv_w, conv_b, dt_bias, a_neg, d_skip_x, norm_g,
      hb, w_gates, w_gates, w_a, w_s)


def _outproj_router_kernel(mg_ref, h_ref, wo_ref, g_ref, b_ref, wrh_ref, wrl_ref, br_ref,
                           o_ref, ids_ref, wts_ref, cnt_ref, carry_ref, *, nb, tm):
    step = pl.program_id(0)

    @pl.when(step == 0)
    def _():
        carry_ref[...] = jnp.zeros_like(carry_ref)

    mix = jnp.dot(mg_ref[...], wo_ref[...], preferred_element_type=F32)
    h1 = _ln_rows(ALPHA * h_ref[...] + mix, g_ref[...], b_ref[...])
    o_ref[...] = h1

    h_hi = h1.astype(BF16)
    h_lo = (h1 - h_hi.astype(F32)).astype(BF16)
    logits = (lax.dot_general(h_hi, wrh_ref[...], _NT, preferred_element_type=F32)
              + (lax.dot_general(h_lo, wrh_ref[...], _NT, preferred_element_type=F32)
                 + lax.dot_general(h_hi, wrl_ref[...], _NT, preferred_element_type=F32))) + br_ref[...]
    lane = lax.broadcasted_iota(jnp.int32, (tm, LANES), 1)
    row = lax.broadcasted_iota(jnp.int32, (BLOCK, 1), 0)
    blocks = tm // BLOCK
    real = jnp.concatenate(
        [(((step * blocks + s) % nb) > 0) | (row >= PAD) for s in range(blocks)], axis=0)
    big = jnp.int32(LANES)

    is_g = lane < MOE_GROUPS
    gl = jnp.where(is_g, logits, -jnp.inf)
    ge = jnp.exp(gl - jnp.max(gl, axis=-1, keepdims=True))
    pg = ge / jnp.sum(ge, axis=-1, keepdims=True)
    p_top = jnp.max(pg, axis=-1, keepdims=True)
    g_idx = jnp.min(jnp.where(is_g & (pg == p_top), lane, big), axis=-1, keepdims=True)

    base = MOE_GROUPS + g_idx * EXPERTS_PER_GROUP
    sel = (lane >= base) & (lane < base + EXPERTS_PER_GROUP)
    el = jnp.where(sel, logits, -jnp.inf)
    ee = jnp.exp(el - jnp.max(el, axis=-1, keepdims=True))
    pe = ee / jnp.sum(ee, axis=-1, keepdims=True)
    v1 = jnp.max(jnp.where(sel, pe, -1.0), axis=-1, keepdims=True)
    i1 = jnp.min(jnp.where(sel & (pe == v1), lane, big), axis=-1, keepdims=True)
    sel2 = sel & (lane != i1)
    v2 = jnp.max(jnp.where(sel2, pe, -1.0), axis=-1, keepdims=True)
    i2 = jnp.min(jnp.where(sel2 & (pe == v2), lane, big), axis=-1, keepdims=True)
    vs = v1 + v2
    w1 = p_top * (v1 / vs)
    w2 = p_top * (v2 / vs)

    hit1 = (lane == i1) & real
    hit2 = (lane == i2) & real
    onehot = jnp.where(hit1 | hit2, 1.0, 0.0)
    rr = lax.broadcasted_iota(jnp.int32, (tm, tm), 0)
    cl = lax.broadcasted_iota(jnp.int32, (tm, tm), 1)
    before = jnp.where(rr > cl, 1.0, 0.0).astype(BF16)
    cum = jnp.dot(before, onehot.astype(BF16), preferred_element_type=F32) + carry_ref[...]
    r1 = jnp.sum(jnp.where(hit1, cum, 0.0), axis=-1, keepdims=True)
    r2 = jnp.sum(jnp.where(hit2, cum, 0.0), axis=-1, keepdims=True)
    carry_ref[...] = carry_ref[...] + jnp.sum(onehot, axis=0, keepdims=True)
    cnt_ref[...] = carry_ref[...].astype(jnp.int32)

    e1 = jnp.where(real, i1 - MOE_GROUPS, -1)
    e2 = jnp.where(real, i2 - MOE_GROUPS, -1)
    ids = jnp.where(lane == 0, e1, jnp.where(lane == 1, e2, jnp.where(
        lane == 2, r1.astype(jnp.int32), jnp.where(lane == 3, r2.astype(jnp.int32), 0))))
    ids_ref[...] = ids.T[0:8, :]
    l8 = lax.broadcasted_iota(jnp.int32, (tm, 8), 1)
    wts_ref[...] = jnp.where(l8 == 0, w1, jnp.where(l8 == 1, w2, 0.0))


def _outproj_router(merged, h0, w_o, g, b, wr_hi, wr_lo, b_r, nb):
    m = h0.shape[0]
    tm = _row_tile(m // BLOCK, 5)
    row_blk = lambda width: pl.BlockSpec((tm, width), lambda i: (i, 0))
    full = lambda shape: pl.BlockSpec(shape, lambda i: (0, 0))
    return pl.pallas_call(
        functools.partial(_outproj_router_kernel, nb=nb, tm=tm),
        out_shape=(jax.ShapeDtypeStruct((m, D_MODEL), F32),
                   jax.ShapeDtypeStruct((8, m), jnp.int32), jax.ShapeDtypeStruct((m, 8), F32),
                   jax.ShapeDtypeStruct((1, LANES), jnp.int32)),
        grid=(m // tm,),
        in_specs=[row_blk(D_MODEL), row_blk(D_MODEL), full((D_MODEL, D_MODEL)),
                  full((1, D_MODEL)), full((1, D_MODEL)),
                  full((LANES, D_MODEL)), full((LANES, D_MODEL)), full((1, LANES))],
        out_specs=(row_blk(D_MODEL), pl.BlockSpec((8, tm), lambda i: (0, i)), row_blk(8),
                   full((1, LANES))),
        scratch_shapes=[pltpu.VMEM((1, LANES), F32)],
        compiler_params=_cparams(("arbitrary",)),
        name="out_proj_ln1_router",
    )(merged, h0, w_o, g, b, wr_hi, wr_lo, b_r)


HALF = D_MODEL // 2
HI_MASK = 0xFFFF0000


def _pack_rows(x):
    lo = lax.bitcast_convert_type(x[:, :HALF].astype(BF16).astype(F32), jnp.uint32) >> 16
    hi = lax.bitcast_convert_type(x[:, HALF:].astype(BF16).astype(F32), jnp.uint32) & jnp.uint32(HI_MASK)
    return hi | lo


def _unpack_rows(w):
    lo = lax.bitcast_convert_type(w << 16, F32)
    hi = lax.bitcast_convert_type(w & jnp.uint32(HI_MASK), F32)
    return lo, hi


ROW_SUB = HALF // LANES


def _put_tile_rows(ref, lead, x):
    for s in range(ROW_SUB):
        ref[lead + (pl.ds(s, x.shape[0], stride=ROW_SUB), slice(None))] = x[:, s * LANES:(s + 1) * LANES]


def _get_tile_rows(ref, lead, r):
    return jnp.concatenate(
        [ref[lead + (pl.ds(s, r, stride=ROW_SUB), slice(None))] for s in range(ROW_SUB)], axis=1)


def _tile_rows(ref, row, n=1):
    return ref.at[pl.ds(pl.multiple_of(row * ROW_SUB, ROW_SUB), n * ROW_SUB), :]


def _dest_kernel(pstart_ref, ids_ref, dest_ref, *, n_slots):
    e = ids_ref[0:TOP_K, :]
    acc = jnp.zeros_like(e)
    for ex in range(N_EXPERTS):
        acc = jnp.where(e == ex, pstart_ref[ex], acc)
    k = lax.broadcasted_iota(jnp.int32, e.shape, 0)
    t = lax.broadcasted_iota(jnp.int32, e.shape, 1)
    dump = n_slots + k * PAD + (t & (BLOCK - 1))
    dest_ref[...] = jnp.where(e >= 0, acc + ids_ref[TOP_K:2 * TOP_K, :], dump)


def _dest_rows(pstart, ids, n_slots):
    m = ids.shape[1]
    return pl.pallas_call(
        functools.partial(_dest_kernel, n_slots=n_slots),
        out_shape=jax.ShapeDtypeStruct((TOP_K, m), jnp.int32),
        in_specs=[pl.BlockSpec(memory_space=pltpu.SMEM), pl.BlockSpec(memory_space=pltpu.VMEM)],
        out_specs=pl.BlockSpec(memory_space=pltpu.VMEM),
        name="moe_dest",
    )(pstart, ids)


def _dispatch_kernel(dest_ref, zlo_ref, zhi_ref, nused_ref, h_ref, xs_hbm, ring, zrow, zblk, sems, zsem,
                     *, n_blocks):
    step = pl.program_id(0)
    n_steps = pl.num_programs(0)
    n_rows = n_steps * BLOCK
    slot = step % 2

    def zero_rows(start):
        def per_expert(ex, carry):
            def per_row(p, c):
                cp = pltpu.make_async_copy(zrow, _tile_rows(xs_hbm, p), zsem)
                cp.start() if start else cp.wait()
                return c
            return lax.fori_loop(zlo_ref[ex], zhi_ref[ex], per_row, carry)
        lax.fori_loop(0, N_EXPERTS, per_expert, 0)

        def per_block(blk, c):
            cp = pltpu.make_async_copy(zblk, _tile_rows(xs_hbm, blk * MOE_ROWS, MOE_ROWS), zsem)
            cp.start() if start else cp.wait()
            return c
        lax.fori_loop(nused_ref[0], n_blocks, per_block, 0)

    @pl.when(step == 0)
    def _():
        zrow[...] = jnp.zeros_like(zrow)
        zblk[...] = jnp.zeros_like(zblk)
        zero_rows(True)

    _put_tile_rows(ring, (slot,), _pack_rows(h_ref[...]))
    base = step * BLOCK

    def body(t, carry):
        for k in range(TOP_K):
            d = dest_ref[k * n_rows + base + t]
            pltpu.make_async_copy(_tile_rows(ring.at[slot], t), _tile_rows(xs_hbm, d),
                                  sems.at[slot]).start(priority=k)
        return carry
    lax.fori_loop(0, BLOCK, body, 0, unroll=4)

    def wait_slot(s):
        for k in range(TOP_K):
            pltpu.make_async_copy(ring.at[s], _tile_rows(xs_hbm, 0, BLOCK), sems.at[s]).wait()

    @pl.when(step > 0)
    def _():
        wait_slot(1 - slot)

    @pl.when(step == n_steps - 1)
    def _():
        wait_slot(slot)
        zero_rows(False)


def _dispatch(h1, dest, zlo, zhi, n_used, n_blocks):
    m = h1.shape[0]
    n_slots = n_blocks * MOE_ROWS
    grid_spec = pltpu.PrefetchScalarGridSpec(
        num_scalar_prefetch=4,
        grid=(m // BLOCK,),
        in_specs=[pl.BlockSpec((BLOCK, D_MODEL), lambda i, *_: (i, 0))],
        out_specs=pl.BlockSpec(memory_space=pl.ANY),
        scratch_shapes=[
            pltpu.VMEM((2, BLOCK * ROW_SUB, LANES), jnp.uint32),
            pltpu.VMEM((ROW_SUB, LANES), jnp.uint32),
            pltpu.VMEM((MOE_ROWS * ROW_SUB, LANES), jnp.uint32),
            pltpu.SemaphoreType.DMA((2,)),
            pltpu.SemaphoreType.DMA(()),
        ],
    )
    return pl.pallas_call(
        functools.partial(_dispatch_kernel, n_blocks=n_blocks),
        out_shape=jax.ShapeDtypeStruct(((n_slots + TOP_K * PAD) * ROW_SUB, LANES), jnp.uint32),
        grid_spec=grid_spec,
        compiler_params=_cparams(("arbitrary",)),
        name="moe_dispatch",
    )(dest, zlo, zhi, n_used, h1)


def _expert_kernel(blk_e_ref, nused_ref, xs_ref, wg_ref, wu_ref, wd_ref, o_ref):
    i = pl.program_id(0)
    n_used = nused_ref[0]

    @pl.when(i < n_used)
    def _():
        lo, hi = _unpack_rows(_get_tile_rows(xs_ref, (), MOE_ROWS))
        x_lo, x_hi = lo.astype(BF16), hi.astype(BF16)
        gate = (jnp.dot(x_lo, wg_ref[0:HALF, :].astype(BF16), preferred_element_type=F32)
                + jnp.dot(x_hi, wg_ref[HALF:D_MODEL, :].astype(BF16), preferred_element_type=F32))
        up = (jnp.dot(x_lo, wu_ref[0:HALF, :].astype(BF16), preferred_element_type=F32)
              + jnp.dot(x_hi, wu_ref[HALF:D_MODEL, :].astype(BF16), preferred_element_type=F32))
        hid = (_silu(gate) * up).astype(BF16)
        _put_tile_rows(o_ref, (), _pack_rows(jnp.dot(hid, wd_ref[...].astype(BF16),
                                                     preferred_element_type=F32)))

    @pl.when(i >= n_used)
    def _():
        o_ref[...] = jnp.zeros_like(o_ref)


def _experts(xs, w_gate, w_up, w_down, blk_e, n_used, n_blocks):
    used = lambda i, nu: jnp.minimum(i, nu[0] - 1)
    grid_spec = pltpu.PrefetchScalarGridSpec(
        num_scalar_prefetch=2,
        grid=(n_blocks,),
        in_specs=[
            pl.BlockSpec((MOE_ROWS * ROW_SUB, LANES), lambda i, be, nu: (used(i, nu), 0)),
            pl.BlockSpec((None, None, D_MODEL, D_EXPERT), lambda i, be, nu: (0, be[i], 0, 0)),
            pl.BlockSpec((None, None, D_MODEL, D_EXPERT), lambda i, be, nu: (0, be[i], 0, 0)),
            pl.BlockSpec((None, None, D_EXPERT, D_MODEL), lambda i, be, nu: (0, be[i], 0, 0)),
        ],
        out_specs=pl.BlockSpec((MOE_ROWS * ROW_SUB, LANES), lambda i, be, nu: (i, 0)),
    )
    return pl.pallas_call(
        _expert_kernel,
        out_shape=jax.ShapeDtypeStruct((n_blocks * MOE_ROWS * ROW_SUB, LANES), jnp.uint32),
        grid_spec=grid_spec,
        compiler_params=_cparams(("arbitrary",)),
        name="moe_experts",
    )(blk_e, n_used, xs, w_gate, w_up, w_down)


def _combine_kernel(dest_ref, h_ref, wts_ref, yb_hbm, g_ref, b_ref, o_ref, ybuf, sems, *, nb):
    bb = pl.program_id(0)
    i = pl.program_id(1)
    n_i = pl.num_programs(1)
    step = bb * n_i + i
    n_steps = pl.num_programs(0) * n_i

    n_rows = pl.num_programs(0) * nb * BLOCK

    def assign_base(b_, i_):
        return (b_ * nb + i_ + 1) * BLOCK

    def start_gather(base, slot):
        def body(t, carry):
            for k in range(TOP_K):
                d = dest_ref[k * n_rows + base + t]
                pltpu.make_async_copy(_tile_rows(yb_hbm, d), _tile_rows(ybuf.at[slot, k], t),
                                      sems.at[slot]).start(priority=k)
            return carry
        lax.fori_loop(0, BLOCK, body, 0, unroll=4)

    def wait_gather(slot):
        for k in range(TOP_K):
            pltpu.make_async_copy(_tile_rows(yb_hbm, 0, BLOCK), ybuf.at[slot, k], sems.at[slot]).wait()

    @pl.when(step == 0)
    def _():
        start_gather(assign_base(0, 0), 0)

    @pl.when(step + 1 < n_steps)
    def _():
        nxt = jnp.where(i + 1 < n_i, assign_base(bb, i + 1), assign_base(bb + 1, 0))
        start_gather(nxt, (step + 1) % 2)

    slot = step % 2
    wait_gather(slot)
    wts = wts_ref[...]
    lo0, hi0 = _unpack_rows(_get_tile_rows(ybuf, (slot, 0), BLOCK))
    lo1, hi1 = _unpack_rows(_get_tile_rows(ybuf, (slot, 1), BLOCK))
    w0, w1 = wts[:, 0:1], wts[:, 1:2]
    r_lo = ALPHA * h_ref[:, 0:HALF] + (lo0 * w0 + lo1 * w1)
    r_hi = ALPHA * h_ref[:, HALF:D_MODEL] + (hi0 * w0 + hi1 * w1)
    mu = (jnp.sum(r_lo, axis=-1, keepdims=True) + jnp.sum(r_hi, axis=-1, keepdims=True)) / D_MODEL
    c_lo, c_hi = r_lo - mu, r_hi - mu
    var = (jnp.sum(c_lo * c_lo, axis=-1, keepdims=True)
           + jnp.sum(c_hi * c_hi, axis=-1, keepdims=True)) / D_MODEL
    inv = lax.rsqrt(var + LN_EPS)
    o_ref[:, 0:HALF] = c_lo * inv * g_ref[:, 0:HALF] + b_ref[:, 0:HALF]
    o_ref[:, HALF:D_MODEL] = c_hi * inv * g_ref[:, HALF:D_MODEL] + b_ref[:, HALF:D_MODEL]


def _combine_ln(h1, wts, yb, dest, g, b, bsz, nb):
    grid_spec = pltpu.PrefetchScalarGridSpec(
        num_scalar_prefetch=1,
        grid=(bsz, nb - 1),
        in_specs=[
            pl.BlockSpec((BLOCK, D_MODEL), lambda bb, i, d: (bb * nb + i + 1, 0)),
            pl.BlockSpec((BLOCK, 8), lambda bb, i, d: (bb * nb + i + 1, 0)),
            pl.BlockSpec(memory_space=pl.ANY),
            pl.BlockSpec((1, D_MODEL), lambda bb, i, d: (0, 0)),
            pl.BlockSpec((1, D_MODEL), lambda bb, i, d: (0, 0)),
        ],
        out_specs=pl.BlockSpec((None, BLOCK, D_MODEL), lambda bb, i, d: (bb, i, 0)),
        scratch_shapes=[
            pltpu.VMEM((2, TOP_K, BLOCK * ROW_SUB, LANES), jnp.uint32),
            pltpu.SemaphoreType.DMA((2,)),
        ],
    )
    return pl.pallas_call(
        functools.partial(_combine_kernel, nb=nb),
        out_shape=jax.ShapeDtypeStruct((bsz, (nb - 1) * BLOCK, D_MODEL), F32),
        grid_spec=grid_spec,
        compiler_params=_cparams(("arbitrary", "arbitrary")),
        name="moe_combine_ln2",
    )(dest, h1, wts, yb, g, b)


def _dispatch_tables(ids, counts, n_blocks):
    counts = counts[0, MOE_GROUPS:MOE_GROUPS + N_EXPERTS]
    pcounts = (counts + MOE_ROWS - 1) // MOE_ROWS * MOE_ROWS
    pend = jnp.cumsum(pcounts)
    pstart = (pend - pcounts).astype(jnp.int32)
    n_used = jnp.maximum(pend[-1] // MOE_ROWS, 1).astype(jnp.int32)
    blk = jnp.arange(n_blocks, dtype=jnp.int32)
    blk_e = jnp.minimum(jnp.sum(pend[None, :] <= (blk * MOE_ROWS)[:, None], axis=1), N_EXPERTS - 1)
    blk_e = jnp.where(blk < n_used, blk_e, blk_e[n_used - 1]).astype(jnp.int32)
    dest = _dest_rows(pstart, ids, n_blocks * MOE_ROWS).reshape(-1)
    return dest, pstart, (pstart + counts).astype(jnp.int32), pend.astype(jnp.int32), blk_e, n_used.reshape(1)


def kernel(x, meta_tokens, ln_emb_g, ln_emb_b, w_in, conv_w, conv_b, dt_bias, a_log, d_skip, ssd_norm_g, sinks, w_br_attn, w_br_ssd, w_o, ln1_g, ln1_b, w_router_group, b_router_group, w_router_expert, b_router_expert, w_gate, w_up, w_down, ln2_g, ln2_b):
    bsz, seq, d = x.shape
    assert d == D_MODEL and seq % BLOCK == 0 and w_in.shape[0] == DEPTH
    nb = seq // BLOCK + 1
    m = bsz * nb * BLOCK
    row2 = lambda v: v.reshape(1, -1).astype(F32)

    meta_pad = jnp.concatenate([jnp.zeros((PAD, d), F32), meta_tokens.astype(F32)], axis=0)
    h0, h0b = _embed_ln(x, meta_pad, row2(ln_emb_g), row2(ln_emb_b))

    w_in_t = jnp.swapaxes(w_in, 1, 2)
    main = _inproj(h0b, w_in_t)
    w_dt = jnp.pad(w_in_t[0, DT_OFF:DT_OFF + SSD_HEADS], ((0, LANES - SSD_HEADS), (0, 0))).astype(BF16)
    pad_h = lambda v: jnp.pad(v.astype(F32), (0, LANES - SSD_HEADS)).reshape(1, LANES)
    merged = _mixer(h0b, main, sinks[0].astype(F32), w_dt, conv_w[0].astype(F32), row2(conv_b[0]),
                    pad_h(dt_bias[0]), pad_h(-jnp.exp(a_log[0].astype(F32))),
                    row2(jnp.repeat(d_skip[0], SSD_HEADDIM)), row2(ssd_norm_g[0]),
                    w_in_t[0, GATE_OFF:].astype(BF16), w_br_attn[0].astype(BF16), w_br_ssd[0].astype(BF16),
                    bsz, nb)
    w_r = jnp.pad(jnp.concatenate([w_router_group[0].T, w_router_expert[0].T], axis=0).astype(F32),
                  ((0, LANES - MOE_GROUPS - N_EXPERTS), (0, 0)))
    wr_hi = w_r.astype(BF16)
    wr_lo = (w_r - wr_hi.astype(F32)).astype(BF16)
    b_r = jnp.pad(jnp.concatenate([b_router_group[0], b_router_expert[0]]).astype(F32),
                  (0, LANES - MOE_GROUPS - N_EXPERTS)).reshape(1, LANES)
    h1, ids, wts, counts = _outproj_router(merged, h0, w_o[0].astype(BF16), row2(ln1_g[0]),
                                           row2(ln1_b[0]), wr_hi, wr_lo, b_r, nb)

    n_assign = bsz * (seq + N_META) * TOP_K
    n_blocks = -(-n_assign // MOE_ROWS) + N_EXPERTS
    dest, _, zlo, zhi, blk_e, n_used = _dispatch_tables(ids, counts, n_blocks)
    xs = _dispatch(h1, dest, zlo, zhi, n_used, n_blocks)
    yb = _experts(xs, w_gate, w_up, w_down, blk_e, n_used, n_blocks)
    return _combine_ln(h1, wts, yb, dest, row2(ln2_g[0]), row2(ln2_b[0]), bsz, nb)
```

```python
import functools

import jax
import jax.numpy as jnp
from jax import lax
from jax.experimental import pallas as pl
from jax.experimental.pallas import tpu as pltpu

F32 = jnp.float32
BF16 = jnp.bfloat16
HIGHEST = lax.Precision.HIGHEST

D_MODEL = 2048
N_META = 16
BLOCK = 128
PAD = BLOCK - N_META
WINDOW = 128
HQ, HKV, HD = 16, 4, 64
Q_PER_KV = HQ // HKV
D_INNER = 2048
SSD_HEADDIM = 64
SSD_HEADS = D_INNER // SSD_HEADDIM
SSD_GROUPS = 4
HEADS_PER_GROUP = SSD_HEADS // SSD_GROUPS
D_STATE = 128
CONV_W = 4
CONV_DIM = D_INNER + 2 * SSD_GROUPS * D_STATE
MOE_GROUPS = 8
EXPERTS_PER_GROUP = 8
N_EXPERTS = MOE_GROUPS * EXPERTS_PER_GROUP
TOP_K = 2
D_EXPERT = 512
LN_EPS = 1e-5
RMS_EPS = 1e-5
NEG_INF = -1e30
DEPTH = 1
ALPHA = (2.0 * DEPTH) ** 0.25

Q_DIM = HQ * HD
KV_DIM = HKV * HD
MAIN_DIM = Q_DIM + 2 * KV_DIM + D_INNER + CONV_DIM
DT_OFF = MAIN_DIM
GATE_OFF = MAIN_DIM + SSD_HEADS
COL_TILE = 512
MAIN_TILES = MAIN_DIM // COL_TILE
Z_COL, XS_COL, Q_COL, K_COL, V_COL, B_COL, C_COL = 0, 2048, 4096, 5120, 5376, 5632, 6144

LANES = 128
MOE_ROWS = 576
VMEM_LIMIT = 56 * 1024 * 1024


def _cparams(sem, vmem=VMEM_LIMIT):
    return pltpu.CompilerParams(dimension_semantics=sem, vmem_limit_bytes=vmem)


def _ln_rows(x, g, b):
    mu = jnp.mean(x, axis=-1, keepdims=True)
    xc = x - mu
    var = jnp.mean(xc * xc, axis=-1, keepdims=True)
    return xc * lax.rsqrt(var + LN_EPS) * g + b


def _sigmoid(x):
    return 1.0 / (1.0 + jnp.exp(-x))


def _silu(x):
    return x * _sigmoid(x)


def _row_tile(n_blocks, max_blocks):
    best = 1
    for c in range(1, max_blocks + 1):
        if n_blocks % c == 0:
            best = c
    return best * BLOCK


def _embed_ln_kernel(*refs, tile_blocks):
    x_refs = refs[:tile_blocks]
    meta_ref, g_ref, b_ref, h_ref, hb_ref = refs[tile_blocks:]
    t = pl.program_id(1)
    row = lax.broadcasted_iota(jnp.int32, (BLOCK, 1), 0)
    for u in range(tile_blocks):
        rows = slice(u * BLOCK, (u + 1) * BLOCK)
        if u == 0:
            src = jnp.where(t == 0, meta_ref[...], x_refs[0][...])
            y = jnp.where((t > 0) | (row >= PAD), _ln_rows(src, g_ref[...], b_ref[...]), 0.0)
        else:
            y = _ln_rows(x_refs[u][...], g_ref[...], b_ref[...])
        h_ref[rows, :] = y
        hb_ref[rows, :] = y.astype(BF16)


def _embed_ln(x, meta_pad, g, b):
    bsz, seq, d = x.shape
    nb = seq // BLOCK + 1
    m = bsz * nb * BLOCK
    tile_blocks = _row_tile(nb, 5) // BLOCK
    tm = tile_blocks * BLOCK
    nt = nb // tile_blocks
    x_spec = lambda u: pl.BlockSpec(
        (None, BLOCK, d), lambda bb, t: (bb, jnp.maximum(t * tile_blocks + u - 1, 0), 0))
    return pl.pallas_call(
        functools.partial(_embed_ln_kernel, tile_blocks=tile_blocks),
        out_shape=(jax.ShapeDtypeStruct((m, d), F32), jax.ShapeDtypeStruct((m, d), BF16)),
        grid=(bsz, nt),
        in_specs=[x_spec(u) for u in range(tile_blocks)] + [
            pl.BlockSpec((BLOCK, d), lambda bb, t: (0, 0)),
            pl.BlockSpec((1, d), lambda bb, t: (0, 0)),
            pl.BlockSpec((1, d), lambda bb, t: (0, 0)),
        ],
        out_specs=(pl.BlockSpec((tm, d), lambda bb, t: (bb * nt + t, 0)),
                   pl.BlockSpec((tm, d), lambda bb, t: (bb * nt + t, 0))),
        compiler_params=_cparams(("parallel", "parallel")),
        name="embed_ln",
    )(*([x] * tile_blocks), meta_pad, g, b)


_NT = (((1,), (1,)), ((), ()))


def _inproj_kernel(a_ref, wt_ref, o_ref):
    o_ref[...] = lax.dot_general(a_ref[...], wt_ref[...].astype(BF16), _NT,
                                 preferred_element_type=F32).astype(o_ref.dtype)


def _main_dest_tile(j):
    return jnp.where(j < 2, j + 8, jnp.where(j == 2, 10, jnp.where(j < 11, j - 3, j)))


def _inproj(hb, w_in_t):
    m, d = hb.shape
    tm = _row_tile(m // BLOCK, 13)
    return pl.pallas_call(
        _inproj_kernel,
        out_shape=jax.ShapeDtypeStruct((m, MAIN_DIM), BF16),
        grid=(m // tm, MAIN_TILES),
        in_specs=[
            pl.BlockSpec((tm, d), lambda i, j: (i, 0)),
            pl.BlockSpec((None, COL_TILE, d), lambda i, j: (0, j, 0)),
        ],
        out_specs=pl.BlockSpec((tm, COL_TILE), lambda i, j: (i, _main_dest_tile(j))),
        compiler_params=_cparams(("parallel", "arbitrary")),
        name="in_proj",
    )(hb, w_in_t)


def _attn_body(n, sinks_ref, q_ref, kc_ref, vc_ref, kp_ref, vp_ref, km_ref, vm_ref, o_ref):
    r = lax.broadcasted_iota(jnp.int32, (BLOCK, BLOCK), 0)
    c = lax.broadcasted_iota(jnp.int32, (BLOCK, BLOCK), 1)
    is_cur = c <= r
    ok_band = (is_cur & (n >= 1)) | ((c > r) & (n >= 2))
    rm = lax.broadcasted_iota(jnp.int32, (BLOCK, N_META), 0)
    cm = lax.broadcasted_iota(jnp.int32, (BLOCK, N_META), 1)
    ok_meta = (n > 0) | (cm <= rm - PAD)
    scale = HD ** -0.5
    nt = (((1,), (1,)), ((), ()))
    for h in range(HKV):
        ks = slice(h * HD, (h + 1) * HD)
        kc, kp, km = kc_ref[:, ks], kp_ref[:, ks], km_ref[PAD:BLOCK, ks]
        vc, vp, vm = vc_ref[:, ks], vp_ref[:, ks], vm_ref[PAD:BLOCK, ks]
        q4 = jnp.concatenate(
            [q_ref[:, (h * Q_PER_KV + g) * HD:(h * Q_PER_KV + g + 1) * HD] for g in range(Q_PER_KV)],
            axis=0)
        s_c4 = lax.dot_general(q4, kc, nt, preferred_element_type=F32)
        s_p4 = lax.dot_general(q4, kp, nt, preferred_element_type=F32)
        s_m4 = lax.dot_general(q4, km, nt, preferred_element_type=F32)
        pc, pp, pm, dens = [], [], [], []
        for g in range(Q_PER_KV):
            rows = slice(g * BLOCK, (g + 1) * BLOCK)
            sink = sinks_ref[h * Q_PER_KV + g]
            s_b = jnp.where(ok_band, jnp.where(is_cur, s_c4[rows], s_p4[rows]) * scale, NEG_INF)
            s_m = jnp.where(ok_meta, s_m4[rows] * scale, NEG_INF)
            mx = jnp.maximum(jnp.maximum(jnp.max(s_b, axis=-1, keepdims=True),
                                         jnp.max(s_m, axis=-1, keepdims=True)), sink)
            p_b = jnp.exp(s_b - mx)
            p_m = jnp.exp(s_m - mx)
            dens.append(jnp.sum(p_b, axis=-1, keepdims=True) + jnp.sum(p_m, axis=-1, keepdims=True)
                        + jnp.exp(sink - mx))
            pc.append(jnp.where(is_cur, p_b, 0.0).astype(BF16))
            pp.append(jnp.where(is_cur, 0.0, p_b).astype(BF16))
            pm.append(p_m.astype(BF16))
        o4 = (jnp.dot(jnp.concatenate(pc, axis=0), vc, preferred_element_type=F32)
              + jnp.dot(jnp.concatenate(pp, axis=0), vp, preferred_element_type=F32)
              + jnp.dot(jnp.concatenate(pm, axis=0), vm, preferred_element_type=F32))
        for g in range(Q_PER_KV):
            hq = h * Q_PER_KV + g
            o_ref[:, hq * HD:(hq + 1) * HD] = (o4[g * BLOCK:(g + 1) * BLOCK] / dens[g]).astype(o_ref.dtype)


def _attn_kernel(*refs):
    _attn_body(pl.program_id(1), *refs)


def _attention(main, sinks, bsz, nb):
    m = main.shape[0]
    kb, vb = K_COL // KV_DIM, V_COL // KV_DIM
    cur = lambda bb, i: bb * nb + i
    prev = lambda bb, i: bb * nb + jnp.maximum(i - 1, 0)
    first = lambda bb, i: bb * nb
    return pl.pallas_call(
        _attn_kernel,
        out_shape=jax.ShapeDtypeStruct((m, Q_DIM), BF16),
        grid=(bsz, nb),
        in_specs=[
            pl.BlockSpec(memory_space=pltpu.SMEM),
            pl.BlockSpec((BLOCK, Q_DIM), lambda bb, i: (cur(bb, i), Q_COL // Q_DIM)),
            pl.BlockSpec((BLOCK, KV_DIM), lambda bb, i: (cur(bb, i), kb)),
            pl.BlockSpec((BLOCK, KV_DIM), lambda bb, i: (cur(bb, i), vb)),
            pl.BlockSpec((BLOCK, KV_DIM), lambda bb, i: (prev(bb, i), kb)),
            pl.BlockSpec((BLOCK, KV_DIM), lambda bb, i: (prev(bb, i), vb)),
            pl.BlockSpec((BLOCK, KV_DIM), lambda bb, i: (first(bb, i), kb)),
            pl.BlockSpec((BLOCK, KV_DIM), lambda bb, i: (first(bb, i), vb)),
        ],
        out_specs=pl.BlockSpec((BLOCK, Q_DIM), lambda bb, i: (cur(bb, i), 0)),
        compiler_params=_cparams(("parallel", "parallel")),
        name="swa_attention",
    )(sinks, main, main, main, main, main, main, main)


CIN_ROWS = BLOCK + 8
CONV_SLABS = CONV_DIM // LANES
B_SLAB = D_INNER // LANES


def _ssd_body(i, hb_ref, wdt_ref, z_ref, xs_ref, bm_ref, cm_ref,
              cw_ref, cbias_ref, dtb_ref, a_ref, dsk_ref, ng_ref,
              y_ref,
              cin_ref, xc_ref, state_ref, yacc_ref, acs_ref, acst_ref, dtt_ref, wt_ref, ea_ref):
    row = lax.broadcasted_iota(jnp.int32, (BLOCK, 1), 0)
    live = jnp.where((i > 0) | (row >= PAD), 1.0, 0.0)

    @pl.when(i == 0)
    def _():
        state_ref[...] = jnp.zeros_like(state_ref)
        cin_ref[:, 0:8, :] = jnp.zeros((CONV_SLABS, 8, LANES), F32)

    @pl.when(i > 0)
    def _():
        cin_ref[:, 0:8, :] = cin_ref[:, BLOCK:BLOCK + 8, :]

    for sl in range(CONV_SLABS):
        cs = slice(sl * LANES, (sl + 1) * LANES)
        if sl < D_INNER // LANES:
            src = xs_ref[:, cs]
        elif sl < (D_INNER + 512) // LANES:
            src = bm_ref[:, sl * LANES - D_INNER:(sl + 1) * LANES - D_INNER]
        else:
            src = cm_ref[:, sl * LANES - D_INNER - 512:(sl + 1) * LANES - D_INNER - 512]
        cin_ref[sl, 8:CIN_ROWS, :] = src.astype(F32)
        acc = jnp.broadcast_to(cbias_ref[:, cs], (BLOCK, LANES))
        for j in range(CONV_W):
            acc = acc + cw_ref[j:j + 1, cs] * cin_ref[sl, pl.ds(8 - (CONV_W - 1) + j, BLOCK, stride=1), :]
        xc_ref[sl] = _silu(acc)

    @pl.when(i == 0)
    def _():
        xc_ref[:, 0:PAD, :] = jnp.zeros((CONV_SLABS, PAD, LANES), F32)

    dt_raw = lax.dot_general(hb_ref[...], wdt_ref[...], _NT, preferred_element_type=F32) + dtb_ref[...]
    dt = (jnp.maximum(dt_raw, 0.0) + jnp.log1p(jnp.exp(-jnp.abs(dt_raw)))) * live
    adt = dt * a_ref[...]
    rr = lax.broadcasted_iota(jnp.int32, (BLOCK, BLOCK), 0)
    cl = lax.broadcasted_iota(jnp.int32, (BLOCK, BLOCK), 1)
    causal = rr >= cl
    a_cs = jnp.dot(jnp.where(causal, 1.0, 0.0), adt, precision=HIGHEST, preferred_element_type=F32)
    a_cs_t = a_cs.T
    dt_t = dt.T
    last = a_cs_t[:, BLOCK - 1:BLOCK]
    acs_ref[...] = a_cs
    acst_ref[...] = a_cs_t
    dtt_ref[...] = dt_t
    wt_ref[...] = dt_t * jnp.exp(last - a_cs_t)
    ea_ref[...] = jnp.exp(a_cs)
    cd_t = jnp.exp(last)

    lane = lax.broadcasted_iota(jnp.int32, (1, LANES), 1)
    lo = lane < SSD_HEADDIM
    nn = (((1,), (1,)), ((), ()))
    for g in range(SSD_GROUPS):
        bg = xc_ref[B_SLAB + g]
        cg = xc_ref[B_SLAB + SSD_GROUPS + g]
        cb = lax.dot_general(cg.astype(BF16), bg.astype(BF16), nn, preferred_element_type=F32)
        bt = bg.T
        for pp in range(HEADS_PER_GROUP // 2):
            pr = g * (HEADS_PER_GROUP // 2) + pp
            ps = slice(pr * LANES, (pr + 1) * LANES)
            xs_pair = xc_ref[pr]
            st_pair = state_ref[:, ps]
            lhs, lhs_s = [], []
            for hd in (2 * pr, 2 * pr + 1):
                col = jnp.broadcast_to(acs_ref[:, hd:hd + 1], (BLOCK, BLOCK))
                seg = col - acst_ref[hd:hd + 1, :]
                dec = jnp.exp(jnp.where(causal, seg, NEG_INF))
                lhs.append((cb * dec * dtt_ref[hd:hd + 1, :]).astype(BF16))
                lhs.append((cg * jnp.broadcast_to(ea_ref[:, hd:hd + 1], (BLOCK, BLOCK))).astype(BF16))
                lhs_s.append((bt * wt_ref[hd:hd + 1, :]).astype(BF16))
            xs_lo = jnp.where(lo, xs_pair, 0.0).astype(BF16)
            xs_hi = jnp.where(lo, 0.0, xs_pair).astype(BF16)
            st_lo = jnp.where(lo, st_pair, 0.0).astype(BF16)
            st_hi = jnp.where(lo, 0.0, st_pair).astype(BF16)
            y_pair = jnp.dot(jnp.concatenate(lhs, axis=1),
                             jnp.concatenate([xs_lo, st_lo, xs_hi, st_hi], axis=0),
                             preferred_element_type=F32)
            yacc_ref[:, ps] = y_pair + dsk_ref[:, ps] * xs_pair
            st_new = jnp.dot(jnp.concatenate(lhs_s, axis=1),
                             jnp.concatenate([xs_lo, xs_hi], axis=0), preferred_element_type=F32)
            cd_row = jnp.where(lo, jnp.broadcast_to(cd_t[2 * pr:2 * pr + 1, :], (1, LANES)),
                               jnp.broadcast_to(cd_t[2 * pr + 1:2 * pr + 2, :], (1, LANES)))
            state_ref[:, ps] = st_pair * cd_row + st_new

    gw = D_INNER // SSD_GROUPS
    for g in range(SSD_GROUPS):
        gs = slice(g * gw, (g + 1) * gw)
        yz = yacc_ref[:, gs] * _silu(z_ref[:, gs].astype(F32))
        ms = jnp.mean(yz * yz, axis=-1, keepdims=True)
        y_ref[:, gs] = (yz * lax.rsqrt(ms + RMS_EPS) * ng_ref[:, gs]).astype(y_ref.dtype)


def _ssd_kernel(*refs):
    _ssd_body(pl.program_id(1), *refs)


def _ssd(hb, main, w_dt, conv_w, conv_b, dt_bias, a_neg, d_skip_x, norm_g, bsz, nb):
    m = hb.shape[0]
    cur = lambda bb, i: bb * nb + i
    full = lambda shape: pl.BlockSpec(shape, lambda bb, i: (0, 0))
    return pl.pallas_call(
        _ssd_kernel,
        out_shape=jax.ShapeDtypeStruct((m, D_INNER), BF16),
        grid=(bsz, nb),
        in_specs=[
            pl.BlockSpec((BLOCK, D_MODEL), lambda bb, i: (cur(bb, i), 0)),
            full((LANES, D_MODEL)),
            pl.BlockSpec((BLOCK, D_INNER), lambda bb, i: (cur(bb, i), Z_COL // D_INNER)),
            pl.BlockSpec((BLOCK, D_INNER), lambda bb, i: (cur(bb, i), XS_COL // D_INNER)),
            pl.BlockSpec((BLOCK, 512), lambda bb, i: (cur(bb, i), B_COL // 512)),
            pl.BlockSpec((BLOCK, 512), lambda bb, i: (cur(bb, i), C_COL // 512)),
            full((CONV_W, CONV_DIM)), full((1, CONV_DIM)), full((1, LANES)), full((1, LANES)),
            full((1, D_INNER)), full((1, D_INNER)),
        ],
        out_specs=pl.BlockSpec((BLOCK, D_INNER), lambda bb, i: (cur(bb, i), 0)),
        scratch_shapes=[
            pltpu.VMEM((CONV_SLABS, CIN_ROWS, LANES), F32),
            pltpu.VMEM((CONV_SLABS, BLOCK, LANES), F32),
            pltpu.VMEM((D_STATE, D_INNER), F32),
            pltpu.VMEM((BLOCK, D_INNER), F32),
            pltpu.VMEM((BLOCK, LANES), F32),
            pltpu.VMEM((LANES, BLOCK), F32),
            pltpu.VMEM((LANES, BLOCK), F32),
            pltpu.VMEM((LANES, BLOCK), F32),
            pltpu.VMEM((BLOCK, LANES), F32),
        ],
        compiler_params=_cparams(("arbitrary", "arbitrary")),
        name="ssd",
    )(hb, w_dt, main, main, main, main, conv_w, conv_b, dt_bias, a_neg, d_skip_x, norm_g)


def _merge_kernel(hb_ref, at_ref, y_ref, wga_ref, wgs_ref, wa_ref, ws_ref, o_ref,
                  wga_b, wgs_b, wa_b, ws_b):
    @pl.when(pl.program_id(1) == 0)
    def _():
        wga_b[...] = wga_ref[...].astype(BF16)
        wgs_b[...] = wgs_ref[...].astype(BF16)
        wa_b[...] = wa_ref[...].astype(BF16)
        ws_b[...] = ws_ref[...].astype(BF16)

    hb = hb_ref[...]
    ga = lax.dot_general(hb, wga_b[...], _NT, preferred_element_type=F32)
    gs = lax.dot_general(hb, wgs_b[...], _NT, preferred_element_type=F32)
    pa = jnp.dot(at_ref[...], wa_b[...], preferred_element_type=F32)
    ps = jnp.dot(y_ref[...], ws_b[...], preferred_element_type=F32)
    o_ref[...] = (_sigmoid(ga) * pa + _sigmoid(gs) * ps).astype(o_ref.dtype)


def _merge(hb, attn, y, w_gates, w_a, w_s):
    m = hb.shape[0]
    tm = _row_tile(m // BLOCK, 5)
    nt = D_MODEL // COL_TILE
    once = pl.Buffered(1)
    return pl.pallas_call(
        _merge_kernel,
        out_shape=jax.ShapeDtypeStruct((m, D_MODEL), BF16),
        grid=(nt, m // tm),
        in_specs=[
            pl.BlockSpec((tm, D_MODEL), lambda j, i: (i, 0)),
            pl.BlockSpec((tm, Q_DIM), lambda j, i: (i, 0)),
            pl.BlockSpec((tm, D_INNER), lambda j, i: (i, 0)),
            pl.BlockSpec((COL_TILE, D_MODEL), lambda j, i: (j, 0), pipeline_mode=once),
            pl.BlockSpec((COL_TILE, D_MODEL), lambda j, i: (j + nt, 0), pipeline_mode=once),
            pl.BlockSpec((Q_DIM, COL_TILE), lambda j, i: (0, j), pipeline_mode=once),
            pl.BlockSpec((D_INNER, COL_TILE), lambda j, i: (0, j), pipeline_mode=once),
        ],
        out_specs=pl.BlockSpec((tm, COL_TILE), lambda j, i: (i, j)),
        scratch_shapes=[
            pltpu.VMEM((COL_TILE, D_MODEL), BF16), pltpu.VMEM((COL_TILE, D_MODEL), BF16),
            pltpu.VMEM((Q_DIM, COL_TILE), BF16), pltpu.VMEM((D_INNER, COL_TILE), BF16),
        ],
        compiler_params=_cparams(("arbitrary", "arbitrary")),
        name="branch_merge",
    )(hb, attn, y, w_gates, w_gates, w_a, w_s)


def _outproj_router_kernel(mg_ref, h_ref, wo_ref, g_ref, b_ref, wr_ref, br_ref,
                           o_ref, ids_ref, wts_ref, cnt_ref, carry_ref, *, nb, tm):
    step = pl.program_id(0)

    @pl.when(step == 0)
    def _():
        carry_ref[...] = jnp.zeros_like(carry_ref)

    mix = jnp.dot(mg_ref[...], wo_ref[...], preferred_element_type=F32)
    h1 = _ln_rows(ALPHA * h_ref[...] + mix, g_ref[...], b_ref[...])
    o_ref[...] = h1

    h_hi = h1.astype(BF16)
    h_lo = (h1 - h_hi.astype(F32)).astype(BF16)
    hh = lax.dot_general(h_hi, wr_ref[...], _NT, preferred_element_type=F32)
    lh = lax.dot_general(h_lo, wr_ref[0:LANES, :], _NT, preferred_element_type=F32)
    logits = hh[:, 0:LANES] + (lh + hh[:, LANES:2 * LANES]) + br_ref[...]
    lt = logits.T
    lane128 = lax.broadcasted_iota(jnp.int32, (1, BLOCK), 1)
    blocks = tm // BLOCK
    real = jnp.concatenate(
        [(((step * blocks + s) % nb) > 0) | (lane128 >= PAD) for s in range(blocks)], axis=1)
    sub8 = lax.broadcasted_iota(jnp.int32, (EXPERTS_PER_GROUP, tm), 0)
    sub = lax.broadcasted_iota(jnp.int32, (LANES, tm), 0)

    gl = lt[0:MOE_GROUPS]
    ge = jnp.exp(gl - jnp.max(gl, axis=0, keepdims=True))
    pg = ge / jnp.sum(ge, axis=0, keepdims=True)
    p_top = jnp.max(pg, axis=0, keepdims=True)
    g_idx = jnp.min(jnp.where(pg == p_top, sub8, MOE_GROUPS), axis=0, keepdims=True)

    el = lt[MOE_GROUPS:MOE_GROUPS + EXPERTS_PER_GROUP]
    for g in range(1, MOE_GROUPS):
        r0 = MOE_GROUPS + g * EXPERTS_PER_GROUP
        el = jnp.where(g_idx == g, lt[r0:r0 + EXPERTS_PER_GROUP], el)
    ee = jnp.exp(el - jnp.max(el, axis=0, keepdims=True))
    pe = ee / jnp.sum(ee, axis=0, keepdims=True)
    v1 = jnp.max(pe, axis=0, keepdims=True)
    i1 = jnp.min(jnp.where(pe == v1, sub8, EXPERTS_PER_GROUP), axis=0, keepdims=True)
    pe2 = jnp.where(sub8 == i1, -1.0, pe)
    v2 = jnp.max(pe2, axis=0, keepdims=True)
    i2 = jnp.min(jnp.where(pe2 == v2, sub8, EXPERTS_PER_GROUP), axis=0, keepdims=True)
    vs = v1 + v2
    w1 = p_top * (v1 / vs)
    w2 = p_top * (v2 / vs)
    e1 = g_idx * EXPERTS_PER_GROUP + i1
    e2 = g_idx * EXPERTS_PER_GROUP + i2

    hit1 = (sub == e1 + MOE_GROUPS) & real
    hit2 = (sub == e2 + MOE_GROUPS) & real
    onehot = jnp.where(hit1 | hit2, 1.0, 0.0)
    rr = lax.broadcasted_iota(jnp.int32, (tm, tm), 0)
    cl = lax.broadcasted_iota(jnp.int32, (tm, tm), 1)
    before = jnp.where(rr > cl, 1.0, 0.0).astype(BF16)
    cum = jnp.dot(before, onehot.T.astype(BF16), preferred_element_type=F32).T + carry_ref[...]
    r1 = jnp.sum(jnp.where(hit1, cum, 0.0), axis=0, keepdims=True)
    r2 = jnp.sum(jnp.where(hit2, cum, 0.0), axis=0, keepdims=True)
    carry_ref[...] = carry_ref[...] + jnp.sum(onehot, axis=1, keepdims=True)
    cnt_ref[...] = carry_ref[...].astype(jnp.int32)

    ids_ref[...] = jnp.where(sub8 == 0, jnp.where(real, e1, -1), jnp.where(
        sub8 == 1, jnp.where(real, e2, -1), jnp.where(
            sub8 == 2, r1.astype(jnp.int32), jnp.where(sub8 == 3, r2.astype(jnp.int32), 0))))
    wts_ref[...] = jnp.where(sub == 0, w1, jnp.where(sub == 1, w2, 0.0)).T[:, 0:8]


def _outproj_router(merged, h0, w_o, g, b, wr_hilo, b_r, nb):
    m = h0.shape[0]
    tm = _row_tile(m // BLOCK, 5)
    row_blk = lambda width: pl.BlockSpec((tm, width), lambda i: (i, 0))
    full = lambda shape: pl.BlockSpec(shape, lambda i: (0, 0))
    return pl.pallas_call(
        functools.partial(_outproj_router_kernel, nb=nb, tm=tm),
        out_shape=(jax.ShapeDtypeStruct((m, D_MODEL), F32),
                   jax.ShapeDtypeStruct((8, m), jnp.int32), jax.ShapeDtypeStruct((m, 8), F32),
                   jax.ShapeDtypeStruct((LANES, 1), jnp.int32)),
        grid=(m // tm,),
        in_specs=[row_blk(D_MODEL), row_blk(D_MODEL), full((D_MODEL, D_MODEL)),
                  full((1, D_MODEL)), full((1, D_MODEL)),
                  full((2 * LANES, D_MODEL)), full((1, LANES))],
        out_specs=(row_blk(D_MODEL), pl.BlockSpec((8, tm), lambda i: (0, i)), row_blk(8),
                   full((LANES, 1))),
        scratch_shapes=[pltpu.VMEM((LANES, 1), F32)],
        compiler_params=_cparams(("arbitrary",)),
        name="out_proj_ln1_router",
    )(merged, h0, w_o, g, b, wr_hilo, b_r)


HALF = D_MODEL // 2
HI_MASK = 0xFFFF0000


def _pack_rows(x):
    lo = lax.bitcast_convert_type(x[:, :HALF].astype(BF16).astype(F32), jnp.uint32) >> 16
    hi = lax.bitcast_convert_type(x[:, HALF:].astype(BF16).astype(F32), jnp.uint32) & jnp.uint32(HI_MASK)
    return hi | lo


def _unpack_rows(w):
    lo = lax.bitcast_convert_type(w << 16, F32)
    hi = lax.bitcast_convert_type(w & jnp.uint32(HI_MASK), F32)
    return lo, hi


ROW_SUB = HALF // LANES


def _put_tile_rows(ref, lead, x):
    for s in range(ROW_SUB):
        ref[lead + (pl.ds(s, x.shape[0], stride=ROW_SUB), slice(None))] = x[:, s * LANES:(s + 1) * LANES]


def _get_tile_rows(ref, lead, r):
    return jnp.concatenate(
        [ref[lead + (pl.ds(s, r, stride=ROW_SUB), slice(None))] for s in range(ROW_SUB)], axis=1)


def _tile_rows(ref, row, n=1):
    return ref.at[pl.ds(pl.multiple_of(row * ROW_SUB, ROW_SUB), n * ROW_SUB), :]


def _dest_kernel(pstart_ref, ids_ref, dest_ref, *, n_slots):
    e = ids_ref[0:TOP_K, :]
    acc = jnp.zeros_like(e)
    for ex in range(N_EXPERTS):
        acc = jnp.where(e == ex, pstart_ref[ex], acc)
    k = lax.broadcasted_iota(jnp.int32, e.shape, 0)
    t = lax.broadcasted_iota(jnp.int32, e.shape, 1)
    dump = n_slots + k * PAD + (t & (BLOCK - 1))
    dest_ref[...] = jnp.where(e >= 0, acc + ids_ref[TOP_K:2 * TOP_K, :], dump)


def _dest_rows(pstart, ids, n_slots):
    m = ids.shape[1]
    return pl.pallas_call(
        functools.partial(_dest_kernel, n_slots=n_slots),
        out_shape=jax.ShapeDtypeStruct((TOP_K, m), jnp.int32),
        in_specs=[pl.BlockSpec(memory_space=pltpu.SMEM), pl.BlockSpec(memory_space=pltpu.VMEM)],
        out_specs=pl.BlockSpec(memory_space=pltpu.VMEM),
        name="moe_dest",
    )(pstart, ids)


def _dispatch_kernel(dest_ref, zlo_ref, zhi_ref, nused_ref, h_ref, xs_hbm, ring, zrow, zblk, sems, zsem,
                     *, n_blocks):
    step = pl.program_id(0)
    n_steps = pl.num_programs(0)
    n_rows = n_steps * BLOCK
    slot = step % 2

    def zero_rows(start):
        def per_expert(ex, carry):
            def per_row(p, c):
                cp = pltpu.make_async_copy(zrow, _tile_rows(xs_hbm, p), zsem)
                cp.start() if start else cp.wait()
                return c
            return lax.fori_loop(zlo_ref[ex], zhi_ref[ex], per_row, carry)
        lax.fori_loop(0, N_EXPERTS, per_expert, 0)

        def per_block(blk, c):
            cp = pltpu.make_async_copy(zblk, _tile_rows(xs_hbm, blk * MOE_ROWS, MOE_ROWS), zsem)
            cp.start() if start else cp.wait()
            return c
        lax.fori_loop(nused_ref[0], n_blocks, per_block, 0)

    @pl.when(step == 0)
    def _():
        zrow[...] = jnp.zeros_like(zrow)
        zblk[...] = jnp.zeros_like(zblk)
        zero_rows(True)

    _put_tile_rows(ring, (slot,), _pack_rows(h_ref[...]))
    base = step * BLOCK

    def body(t, carry):
        for k in range(TOP_K):
            d = dest_ref[k * n_rows + base + t]
            pltpu.make_async_copy(_tile_rows(ring.at[slot], t), _tile_rows(xs_hbm, d),
                                  sems.at[slot]).start(priority=k)
        return carry
    lax.fori_loop(0, BLOCK, body, 0, unroll=4)

    def wait_slot(s):
        for k in range(TOP_K):
            pltpu.make_async_copy(ring.at[s], _tile_rows(xs_hbm, 0, BLOCK), sems.at[s]).wait()

    @pl.when(step > 0)
    def _():
        wait_slot(1 - slot)

    @pl.when(step == n_steps - 1)
    def _():
        wait_slot(slot)
        zero_rows(False)


def _dispatch(h1, dest, zlo, zhi, n_used, n_blocks):
    m = h1.shape[0]
    n_slots = n_blocks * MOE_ROWS
    grid_spec = pltpu.PrefetchScalarGridSpec(
        num_scalar_prefetch=4,
        grid=(m // BLOCK,),
        in_specs=[pl.BlockSpec((BLOCK, D_MODEL), lambda i, *_: (i, 0))],
        out_specs=pl.BlockSpec(memory_space=pl.ANY),
        scratch_shapes=[
            pltpu.VMEM((2, BLOCK * ROW_SUB, LANES), jnp.uint32),
            pltpu.VMEM((ROW_SUB, LANES), jnp.uint32),
            pltpu.VMEM((MOE_ROWS * ROW_SUB, LANES), jnp.uint32),
            pltpu.SemaphoreType.DMA((2,)),
            pltpu.SemaphoreType.DMA(()),
        ],
    )
    return pl.pallas_call(
        functools.partial(_dispatch_kernel, n_blocks=n_blocks),
        out_shape=jax.ShapeDtypeStruct(((n_slots + TOP_K * PAD) * ROW_SUB, LANES), jnp.uint32),
        grid_spec=grid_spec,
        compiler_params=_cparams(("arbitrary",)),
        name="moe_dispatch",
    )(dest, zlo, zhi, n_used, h1)


def _expert_kernel(blk_e_ref, nused_ref, xs_ref, wg_ref, wu_ref, wd_ref, o_ref):
    i = pl.program_id(0)
    n_used = nused_ref[0]

    @pl.when(i < n_used)
    def _():
        lo, hi = _unpack_rows(_get_tile_rows(xs_ref, (), MOE_ROWS))
        x_lo, x_hi = lo.astype(BF16), hi.astype(BF16)
        gate = (jnp.dot(x_lo, wg_ref[0:HALF, :].astype(BF16), preferred_element_type=F32)
                + jnp.dot(x_hi, wg_ref[HALF:D_MODEL, :].astype(BF16), preferred_element_type=F32))
        up = (jnp.dot(x_lo, wu_ref[0:HALF, :].astype(BF16), preferred_element_type=F32)
              + jnp.dot(x_hi, wu_ref[HALF:D_MODEL, :].astype(BF16), preferred_element_type=F32))
        hid = (_silu(gate) * up).astype(BF16)
        _put_tile_rows(o_ref, (), _pack_rows(jnp.dot(hid, wd_ref[...].astype(BF16),
                                                     preferred_element_type=F32)))

    @pl.when(i >= n_used)
    def _():
        o_ref[...] = jnp.zeros_like(o_ref)


def _experts(xs, w_gate, w_up, w_down, blk_e, n_used, n_blocks):
    used = lambda i, nu: jnp.minimum(i, nu[0] - 1)
    grid_spec = pltpu.PrefetchScalarGridSpec(
        num_scalar_prefetch=2,
        grid=(n_blocks,),
        in_specs=[
            pl.BlockSpec((MOE_ROWS * ROW_SUB, LANES), lambda i, be, nu: (used(i, nu), 0)),
            pl.BlockSpec((None, None, D_MODEL, D_EXPERT), lambda i, be, nu: (0, be[i], 0, 0)),
            pl.BlockSpec((None, None, D_MODEL, D_EXPERT), lambda i, be, nu: (0, be[i], 0, 0)),
            pl.BlockSpec((None, None, D_EXPERT, D_MODEL), lambda i, be, nu: (0, be[i], 0, 0)),
        ],
        out_specs=pl.BlockSpec((MOE_ROWS * ROW_SUB, LANES), lambda i, be, nu: (i, 0)),
    )
    return pl.pallas_call(
        _expert_kernel,
        out_shape=jax.ShapeDtypeStruct((n_blocks * MOE_ROWS * ROW_SUB, LANES), jnp.uint32),
        grid_spec=grid_spec,
        compiler_params=_cparams(("arbitrary",)),
        name="moe_experts",
    )(blk_e, n_used, xs, w_gate, w_up, w_down)


def _combine_kernel(dest_ref, h_ref, wts_ref, yb_hbm, g_ref, b_ref, o_ref, ybuf, sems, *, nb):
    bb = pl.program_id(0)
    i = pl.program_id(1)
    n_i = pl.num_programs(1)
    step = bb * n_i + i
    n_steps = pl.num_programs(0) * n_i

    n_rows = pl.num_programs(0) * nb * BLOCK

    def assign_base(b_, i_):
        return (b_ * nb + i_ + 1) * BLOCK

    def start_gather(base, slot):
        def body(t, carry):
            for k in range(TOP_K):
                d = dest_ref[k * n_rows + base + t]
                pltpu.make_async_copy(_tile_rows(yb_hbm, d), _tile_rows(ybuf.at[slot, k], t),
                                      sems.at[slot]).start(priority=k)
            return carry
        lax.fori_loop(0, BLOCK, body, 0, unroll=4)

    def wait_gather(slot):
        for k in range(TOP_K):
            pltpu.make_async_copy(_tile_rows(yb_hbm, 0, BLOCK), ybuf.at[slot, k], sems.at[slot]).wait()

    @pl.when(step == 0)
    def _():
        start_gather(assign_base(0, 0), 0)

    @pl.when(step + 1 < n_steps)
    def _():
        nxt = jnp.where(i + 1 < n_i, assign_base(bb, i + 1), assign_base(bb + 1, 0))
        start_gather(nxt, (step + 1) % 2)

    slot = step % 2
    wait_gather(slot)
    wts = wts_ref[...]
    lo0, hi0 = _unpack_rows(_get_tile_rows(ybuf, (slot, 0), BLOCK))
    lo1, hi1 = _unpack_rows(_get_tile_rows(ybuf, (slot, 1), BLOCK))
    w0, w1 = wts[:, 0:1], wts[:, 1:2]
    r_lo = ALPHA * h_ref[:, 0:HALF] + (lo0 * w0 + lo1 * w1)
    r_hi = ALPHA * h_ref[:, HALF:D_MODEL] + (hi0 * w0 + hi1 * w1)
    mu = (jnp.sum(r_lo, axis=-1, keepdims=True) + jnp.sum(r_hi, axis=-1, keepdims=True)) / D_MODEL
    c_lo, c_hi = r_lo - mu, r_hi - mu
    var = (jnp.sum(c_lo * c_lo, axis=-1, keepdims=True)
           + jnp.sum(c_hi * c_hi, axis=-1, keepdims=True)) / D_MODEL
    inv = lax.rsqrt(var + LN_EPS)
    o_ref[:, 0:HALF] = c_lo * inv * g_ref[:, 0:HALF] + b_ref[:, 0:HALF]
    o_ref[:, HALF:D_MODEL] = c_hi * inv * g_ref[:, HALF:D_MODEL] + b_ref[:, HALF:D_MODEL]


def _combine_ln(h1, wts, yb, dest, g, b, bsz, nb):
    grid_spec = pltpu.PrefetchScalarGridSpec(
        num_scalar_prefetch=1,
        grid=(bsz, nb - 1),
        in_specs=[
            pl.BlockSpec((BLOCK, D_MODEL), lambda bb, i, d: (bb * nb + i + 1, 0)),
            pl.BlockSpec((BLOCK, 8), lambda bb, i, d: (bb * nb + i + 1, 0)),
            pl.BlockSpec(memory_space=pl.ANY),
            pl.BlockSpec((1, D_MODEL), lambda bb, i, d: (0, 0)),
            pl.BlockSpec((1, D_MODEL), lambda bb, i, d: (0, 0)),
        ],
        out_specs=pl.BlockSpec((None, BLOCK, D_MODEL), lambda bb, i, d: (bb, i, 0)),
        scratch_shapes=[
            pltpu.VMEM((2, TOP_K, BLOCK * ROW_SUB, LANES), jnp.uint32),
            pltpu.SemaphoreType.DMA((2,)),
        ],
    )
    return pl.pallas_call(
        functools.partial(_combine_kernel, nb=nb),
        out_shape=jax.ShapeDtypeStruct((bsz, (nb - 1) * BLOCK, D_MODEL), F32),
        grid_spec=grid_spec,
        compiler_params=_cparams(("arbitrary", "arbitrary")),
        name="moe_combine_ln2",
    )(dest, h1, wts, yb, g, b)


def _dispatch_tables(ids, counts, n_blocks):
    counts = counts[MOE_GROUPS:MOE_GROUPS + N_EXPERTS, 0]
    pcounts = (counts + MOE_ROWS - 1) // MOE_ROWS * MOE_ROWS
    pend = jnp.cumsum(pcounts)
    pstart = (pend - pcounts).astype(jnp.int32)
    n_used = jnp.maximum(pend[-1] // MOE_ROWS, 1).astype(jnp.int32)
    blk = jnp.arange(n_blocks, dtype=jnp.int32)
    blk_e = jnp.minimum(jnp.sum(pend[None, :] <= (blk * MOE_ROWS)[:, None], axis=1), N_EXPERTS - 1)
    blk_e = jnp.where(blk < n_used, blk_e, blk_e[n_used - 1]).astype(jnp.int32)
    dest = _dest_rows(pstart, ids, n_blocks * MOE_ROWS).reshape(-1)
    return dest, pstart, (pstart + counts).astype(jnp.int32), pend.astype(jnp.int32), blk_e, n_used.reshape(1)


def kernel(x, meta_tokens, ln_emb_g, ln_emb_b, w_in, conv_w, conv_b, dt_bias, a_log, d_skip, ssd_norm_g, sinks, w_br_attn, w_br_ssd, w_o, ln1_g, ln1_b, w_router_group, b_router_group, w_router_expert, b_router_expert, w_gate, w_up, w_down, ln2_g, ln2_b):
    bsz, seq, d = x.shape
    assert d == D_MODEL and seq % BLOCK == 0 and w_in.shape[0] == DEPTH
    nb = seq // BLOCK + 1
    m = bsz * nb * BLOCK
    row2 = lambda v: v.reshape(1, -1).astype(F32)

    meta_pad = jnp.concatenate([jnp.zeros((PAD, d), F32), meta_tokens.astype(F32)], axis=0)
    h0, h0b = _embed_ln(x, meta_pad, row2(ln_emb_g), row2(ln_emb_b))

    w_in_t = jnp.swapaxes(w_in, 1, 2)
    main = _inproj(h0b, w_in_t)
    w_dt = jnp.pad(w_in_t[0, DT_OFF:DT_OFF + SSD_HEADS], ((0, LANES - SSD_HEADS), (0, 0))).astype(BF16)
    pad_h = lambda v: jnp.pad(v.astype(F32), (0, LANES - SSD_HEADS)).reshape(1, LANES)
    attn = _attention(main, sinks[0].astype(F32), bsz, nb)
    y = _ssd(h0b, main, w_dt, conv_w[0].astype(F32), row2(conv_b[0]), pad_h(dt_bias[0]),
             pad_h(-jnp.exp(a_log[0].astype(F32))), row2(jnp.repeat(d_skip[0], SSD_HEADDIM)),
             row2(ssd_norm_g[0]), bsz, nb)
    merged = _merge(h0b, attn, y, w_in_t[0, GATE_OFF:], w_br_attn[0], w_br_ssd[0])
    w_r = jnp.pad(jnp.concatenate([w_router_group[0].T, w_router_expert[0].T], axis=0).astype(F32),
                  ((0, LANES - MOE_GROUPS - N_EXPERTS), (0, 0)))
    wr_hi = w_r.astype(BF16)
    wr_lo = (w_r - wr_hi.astype(F32)).astype(BF16)
    b_r = jnp.pad(jnp.concatenate([b_router_group[0], b_router_expert[0]]).astype(F32),
                  (0, LANES - MOE_GROUPS - N_EXPERTS)).reshape(1, LANES)
    h1, ids, wts, counts = _outproj_router(merged, h0, w_o[0].astype(BF16), row2(ln1_g[0]),
                                           row2(ln1_b[0]), jnp.concatenate([wr_hi, wr_lo], axis=0), b_r, nb)

    n_assign = bsz * (seq + N_META) * TOP_K
    n_blocks = -(-n_assign // MOE_ROWS) + N_EXPERTS
    dest, _, zlo, zhi, blk_e, n_used = _dispatch_tables(ids, counts, n_blocks)
    xs = _dispatch(h1, dest, zlo, zhi, n_used, n_blocks)
    yb = _experts(xs, w_gate, w_up, w_down, blk_e, n_used, n_blocks)
    return _combine_ln(h1, wts, yb, dest, row2(ln2_g[0]), row2(ln2_b[0]), bsz, nb)
```

```python
import functools

import jax
import jax.numpy as jnp
from jax import lax
from jax.experimental import pallas as pl
from jax.experimental.pallas import tpu as pltpu

F32 = jnp.float32
BF16 = jnp.bfloat16
HIGHEST = lax.Precision.HIGHEST

D_MODEL = 2048
N_META = 16
BLOCK = 128
PAD = BLOCK - N_META
WINDOW = 128
HQ, HKV, HD = 16, 4, 64
Q_PER_KV = HQ // HKV
D_INNER = 2048
SSD_HEADDIM = 64
SSD_HEADS = D_INNER // SSD_HEADDIM
SSD_GROUPS = 4
HEADS_PER_GROUP = SSD_HEADS // SSD_GROUPS
D_STATE = 128
CONV_W = 4
CONV_DIM = D_INNER + 2 * SSD_GROUPS * D_STATE
MOE_GROUPS = 8
EXPERTS_PER_GROUP = 8
N_EXPERTS = MOE_GROUPS * EXPERTS_PER_GROUP
TOP_K = 2
D_EXPERT = 512
LN_EPS = 1e-5
RMS_EPS = 1e-5
NEG_INF = -1e30
DEPTH = 1
ALPHA = (2.0 * DEPTH) ** 0.25

Q_DIM = HQ * HD
KV_DIM = HKV * HD
MAIN_DIM = Q_DIM + 2 * KV_DIM + D_INNER + CONV_DIM
DT_OFF = MAIN_DIM
GATE_OFF = MAIN_DIM + SSD_HEADS
COL_TILE = 512
MAIN_TILES = MAIN_DIM // COL_TILE
Z_COL, XS_COL, Q_COL, K_COL, V_COL, B_COL, C_COL = 0, 2048, 4096, 5120, 5376, 5632, 6144

LANES = 128
MOE_ROWS = 576
VMEM_LIMIT = 56 * 1024 * 1024


def _cparams(sem, vmem=VMEM_LIMIT):
    return pltpu.CompilerParams(dimension_semantics=sem, vmem_limit_bytes=vmem)


def _ln_rows(x, g, b):
    mu = jnp.mean(x, axis=-1, keepdims=True)
    xc = x - mu
    var = jnp.mean(xc * xc, axis=-1, keepdims=True)
    return xc * lax.rsqrt(var + LN_EPS) * g + b


def _sigmoid(x):
    return 1.0 / (1.0 + jnp.exp(-x))


def _silu(x):
    return x * _sigmoid(x)


def _row_tile(n_blocks, max_blocks):
    best = 1
    for c in range(1, max_blocks + 1):
        if n_blocks % c == 0:
            best = c
    return best * BLOCK


def _embed_ln_kernel(*refs, tile_blocks):
    x_refs = refs[:tile_blocks]
    meta_ref, g_ref, b_ref, h_ref, hb_ref = refs[tile_blocks:]
    t = pl.program_id(1)
    row = lax.broadcasted_iota(jnp.int32, (BLOCK, 1), 0)
    for u in range(tile_blocks):
        rows = slice(u * BLOCK, (u + 1) * BLOCK)
        if u == 0:
            src = jnp.where(t == 0, meta_ref[...], x_refs[0][...])
            y = jnp.where((t > 0) | (row >= PAD), _ln_rows(src, g_ref[...], b_ref[...]), 0.0)
        else:
            y = _ln_rows(x_refs[u][...], g_ref[...], b_ref[...])
        h_ref[rows, :] = y
        hb_ref[rows, :] = y.astype(BF16)


def _embed_ln(x, meta_pad, g, b):
    bsz, seq, d = x.shape
    nb = seq // BLOCK + 1
    m = bsz * nb * BLOCK
    tile_blocks = _row_tile(nb, 5) // BLOCK
    tm = tile_blocks * BLOCK
    nt = nb // tile_blocks
    x_spec = lambda u: pl.BlockSpec(
        (None, BLOCK, d), lambda bb, t: (bb, jnp.maximum(t * tile_blocks + u - 1, 0), 0))
    return pl.pallas_call(
        functools.partial(_embed_ln_kernel, tile_blocks=tile_blocks),
        out_shape=(jax.ShapeDtypeStruct((m, d), F32), jax.ShapeDtypeStruct((m, d), BF16)),
        grid=(bsz, nt),
        in_specs=[x_spec(u) for u in range(tile_blocks)] + [
            pl.BlockSpec((BLOCK, d), lambda bb, t: (0, 0)),
            pl.BlockSpec((1, d), lambda bb, t: (0, 0)),
            pl.BlockSpec((1, d), lambda bb, t: (0, 0)),
        ],
        out_specs=(pl.BlockSpec((tm, d), lambda bb, t: (bb * nt + t, 0)),
                   pl.BlockSpec((tm, d), lambda bb, t: (bb * nt + t, 0))),
        compiler_params=_cparams(("parallel", "parallel")),
        name="embed_ln",
    )(*([x] * tile_blocks), meta_pad, g, b)


_NT = (((1,), (1,)), ((), ()))


def _inproj_kernel(a_ref, wt_ref, o_ref):
    o_ref[...] = lax.dot_general(a_ref[...], wt_ref[...].astype(BF16), _NT,
                                 preferred_element_type=F32).astype(o_ref.dtype)


def _main_dest_tile(j):
    return jnp.where(j < 2, j + 8, jnp.where(j == 2, 10, jnp.where(j < 11, j - 3, j)))


def _inproj(hb, w_in_t):
    m, d = hb.shape
    tm = _row_tile(m // BLOCK, 13)
    return pl.pallas_call(
        _inproj_kernel,
        out_shape=jax.ShapeDtypeStruct((m, MAIN_DIM), BF16),
        grid=(m // tm, MAIN_TILES),
        in_specs=[
            pl.BlockSpec((tm, d), lambda i, j: (i, 0)),
            pl.BlockSpec((None, COL_TILE, d), lambda i, j: (0, j, 0)),
        ],
        out_specs=pl.BlockSpec((tm, COL_TILE), lambda i, j: (i, _main_dest_tile(j))),
        compiler_params=_cparams(("parallel", "arbitrary")),
        name="in_proj",
    )(hb, w_in_t)


def _attn_body(n, sinks_ref, q_ref, kc_ref, vc_ref, kp_ref, vp_ref, km_ref, vm_ref, o_ref):
    r = lax.broadcasted_iota(jnp.int32, (BLOCK, BLOCK), 0)
    c = lax.broadcasted_iota(jnp.int32, (BLOCK, BLOCK), 1)
    is_cur = c <= r
    ok_band = (is_cur & (n >= 1)) | ((c > r) & (n >= 2))
    rm = lax.broadcasted_iota(jnp.int32, (BLOCK, N_META), 0)
    cm = lax.broadcasted_iota(jnp.int32, (BLOCK, N_META), 1)
    ok_meta = (n > 0) | (cm <= rm - PAD)
    scale = HD ** -0.5
    nt = (((1,), (1,)), ((), ()))
    for h in range(HKV):
        ks = slice(h * HD, (h + 1) * HD)
        kc, kp, km = kc_ref[:, ks], kp_ref[:, ks], km_ref[PAD:BLOCK, ks]
        vc, vp, vm = vc_ref[:, ks], vp_ref[:, ks], vm_ref[PAD:BLOCK, ks]
        q4 = jnp.concatenate(
            [q_ref[:, (h * Q_PER_KV + g) * HD:(h * Q_PER_KV + g + 1) * HD] for g in range(Q_PER_KV)],
            axis=0)
        s_cp4 = lax.dot_general(q4, jnp.concatenate([kc, kp], axis=0), nt,
                                preferred_element_type=F32)
        s_c4, s_p4 = s_cp4[:, 0:BLOCK], s_cp4[:, BLOCK:2 * BLOCK]
        s_m4 = lax.dot_general(q4, km, nt, preferred_element_type=F32)
        pcp, pm, dens = [], [], []
        for g in range(Q_PER_KV):
            rows = slice(g * BLOCK, (g + 1) * BLOCK)
            sink = sinks_ref[h * Q_PER_KV + g]
            s_b = jnp.where(ok_band, jnp.where(is_cur, s_c4[rows], s_p4[rows]) * scale, NEG_INF)
            s_m = jnp.where(ok_meta, s_m4[rows] * scale, NEG_INF)
            mx = jnp.maximum(jnp.maximum(jnp.max(s_b, axis=-1, keepdims=True),
                                         jnp.max(s_m, axis=-1, keepdims=True)), sink)
            p_b = jnp.exp(s_b - mx)
            p_m = jnp.exp(s_m - mx)
            dens.append(jnp.sum(p_b, axis=-1, keepdims=True) + jnp.sum(p_m, axis=-1, keepdims=True)
                        + jnp.exp(sink - mx))
            pcp.append(jnp.concatenate([jnp.where(is_cur, p_b, 0.0).astype(BF16),
                                        jnp.where(is_cur, 0.0, p_b).astype(BF16)], axis=1))
            pm.append(p_m.astype(BF16))
        o4 = (jnp.dot(jnp.concatenate(pcp, axis=0), jnp.concatenate([vc, vp], axis=0),
                      preferred_element_type=F32)
              + jnp.dot(jnp.concatenate(pm, axis=0), vm, preferred_element_type=F32))
        for g in range(Q_PER_KV):
            hq = h * Q_PER_KV + g
            o_ref[:, hq * HD:(hq + 1) * HD] = (o4[g * BLOCK:(g + 1) * BLOCK] / dens[g]).astype(o_ref.dtype)


def _attn_kernel(*refs):
    _attn_body(pl.program_id(1), *refs)


def _attention(main, sinks, bsz, nb):
    m = main.shape[0]
    kb, vb = K_COL // KV_DIM, V_COL // KV_DIM
    cur = lambda bb, i: bb * nb + i
    prev = lambda bb, i: bb * nb + jnp.maximum(i - 1, 0)
    first = lambda bb, i: bb * nb
    return pl.pallas_call(
        _attn_kernel,
        out_shape=jax.ShapeDtypeStruct((m, Q_DIM), BF16),
        grid=(bsz, nb),
        in_specs=[
            pl.BlockSpec(memory_space=pltpu.SMEM),
            pl.BlockSpec((BLOCK, Q_DIM), lambda bb, i: (cur(bb, i), Q_COL // Q_DIM)),
            pl.BlockSpec((BLOCK, KV_DIM), lambda bb, i: (cur(bb, i), kb)),
            pl.BlockSpec((BLOCK, KV_DIM), lambda bb, i: (cur(bb, i), vb)),
            pl.BlockSpec((BLOCK, KV_DIM), lambda bb, i: (prev(bb, i), kb)),
            pl.BlockSpec((BLOCK, KV_DIM), lambda bb, i: (prev(bb, i), vb)),
            pl.BlockSpec((BLOCK, KV_DIM), lambda bb, i: (first(bb, i), kb)),
            pl.BlockSpec((BLOCK, KV_DIM), lambda bb, i: (first(bb, i), vb)),
        ],
        out_specs=pl.BlockSpec((BLOCK, Q_DIM), lambda bb, i: (cur(bb, i), 0)),
        compiler_params=_cparams(("parallel", "parallel")),
        name="swa_attention",
    )(sinks, main, main, main, main, main, main, main)


CIN_ROWS = BLOCK + 8
CONV_SLABS = CONV_DIM // LANES
B_SLAB = D_INNER // LANES


def _ssd_body(i, hb_ref, wdt_ref, z_ref, xs_ref, bm_ref, cm_ref,
              cw_ref, cbias_ref, dtb_ref, a_ref, dsk_ref, ng_ref,
              y_ref,
              cin_ref, xc_ref, state_ref, yacc_ref, acs_ref, acst_ref, dtt_ref, wt_ref, ea_ref):
    row = lax.broadcasted_iota(jnp.int32, (BLOCK, 1), 0)
    live = jnp.where((i > 0) | (row >= PAD), 1.0, 0.0)

    @pl.when(i == 0)
    def _():
        state_ref[...] = jnp.zeros_like(state_ref)
        cin_ref[:, 0:8, :] = jnp.zeros((CONV_SLABS, 8, LANES), F32)

    @pl.when(i > 0)
    def _():
        cin_ref[:, 0:8, :] = cin_ref[:, BLOCK:BLOCK + 8, :]

    for sl in range(CONV_SLABS):
        cs = slice(sl * LANES, (sl + 1) * LANES)
        if sl < D_INNER // LANES:
            src = xs_ref[:, cs]
        elif sl < (D_INNER + 512) // LANES:
            src = bm_ref[:, sl * LANES - D_INNER:(sl + 1) * LANES - D_INNER]
        else:
            src = cm_ref[:, sl * LANES - D_INNER - 512:(sl + 1) * LANES - D_INNER - 512]
        cin_ref[sl, 8:CIN_ROWS, :] = src.astype(F32)
        acc = jnp.broadcast_to(cbias_ref[:, cs], (BLOCK, LANES))
        for j in range(CONV_W):
            acc = acc + cw_ref[j:j + 1, cs] * cin_ref[sl, pl.ds(8 - (CONV_W - 1) + j, BLOCK, stride=1), :]
        xc_ref[sl] = _silu(acc)

    @pl.when(i == 0)
    def _():
        xc_ref[:, 0:PAD, :] = jnp.zeros((CONV_SLABS, PAD, LANES), F32)

    dt_raw = lax.dot_general(hb_ref[...], wdt_ref[...], _NT, preferred_element_type=F32) + dtb_ref[...]
    dt = (jnp.maximum(dt_raw, 0.0) + jnp.log1p(jnp.exp(-jnp.abs(dt_raw)))) * live
    adt = dt * a_ref[...]
    rr = lax.broadcasted_iota(jnp.int32, (BLOCK, BLOCK), 0)
    cl = lax.broadcasted_iota(jnp.int32, (BLOCK, BLOCK), 1)
    causal = rr >= cl
    a_cs = jnp.dot(jnp.where(causal, 1.0, 0.0), adt, precision=HIGHEST, preferred_element_type=F32)
    a_cs_t = a_cs.T
    dt_t = dt.T
    last = a_cs_t[:, BLOCK - 1:BLOCK]
    acs_ref[...] = a_cs
    acst_ref[...] = a_cs_t
    dtt_ref[...] = dt_t
    wt_ref[...] = dt_t * jnp.exp(last - a_cs_t)
    ea_ref[...] = jnp.exp(a_cs)
    cd_t = jnp.exp(last)

    lane = lax.broadcasted_iota(jnp.int32, (1, LANES), 1)
    lo = lane < SSD_HEADDIM
    nn = (((1,), (1,)), ((), ()))
    for g in range(SSD_GROUPS):
        bg = xc_ref[B_SLAB + g]
        cg = xc_ref[B_SLAB + SSD_GROUPS + g]
        cb = lax.dot_general(cg.astype(BF16), bg.astype(BF16), nn, preferred_element_type=F32)
        bt = bg.T
        for pp in range(HEADS_PER_GROUP // 2):
            pr = g * (HEADS_PER_GROUP // 2) + pp
            ps = slice(pr * LANES, (pr + 1) * LANES)
            xs_pair = xc_ref[pr]
            st_pair = state_ref[:, ps]
            lhs, lhs_s = [], []
            for hd in (2 * pr, 2 * pr + 1):
                col = jnp.broadcast_to(acs_ref[:, hd:hd + 1], (BLOCK, BLOCK))
                seg = col - acst_ref[hd:hd + 1, :]
                dec = jnp.exp(jnp.where(causal, seg, NEG_INF))
                lhs.append((cb * dec * dtt_ref[hd:hd + 1, :]).astype(BF16))
                lhs.append((cg * jnp.broadcast_to(ea_ref[:, hd:hd + 1], (BLOCK, BLOCK))).astype(BF16))
                lhs_s.append((bt * wt_ref[hd:hd + 1, :]).astype(BF16))
            xs_lo = jnp.where(lo, xs_pair, 0.0).astype(BF16)
            xs_hi = jnp.where(lo, 0.0, xs_pair).astype(BF16)
            st_lo = jnp.where(lo, st_pair, 0.0).astype(BF16)
            st_hi = jnp.where(lo, 0.0, st_pair).astype(BF16)
            y_pair = jnp.dot(jnp.concatenate(lhs, axis=1),
                             jnp.concatenate([xs_lo, st_lo, xs_hi, st_hi], axis=0),
                             preferred_element_type=F32)
            yacc_ref[:, ps] = y_pair + dsk_ref[:, ps] * xs_pair
            st_new = jnp.dot(jnp.concatenate(lhs_s, axis=1),
                             jnp.concatenate([xs_lo, xs_hi], axis=0), preferred_element_type=F32)
            cd_row = jnp.where(lo, jnp.broadcast_to(cd_t[2 * pr:2 * pr + 1, :], (1, LANES)),
                               jnp.broadcast_to(cd_t[2 * pr + 1:2 * pr + 2, :], (1, LANES)))
            state_ref[:, ps] = st_pair * cd_row + st_new

    gw = D_INNER // SSD_GROUPS
    for g in range(SSD_GROUPS):
        gs = slice(g * gw, (g + 1) * gw)
        yz = yacc_ref[:, gs] * _silu(z_ref[:, gs].astype(F32))
        ms = jnp.mean(yz * yz, axis=-1, keepdims=True)
        y_ref[:, gs] = (yz * lax.rsqrt(ms + RMS_EPS) * ng_ref[:, gs]).astype(y_ref.dtype)


def _ssd_kernel(*refs):
    _ssd_body(pl.program_id(1), *refs)


def _ssd(hb, main, w_dt, conv_w, conv_b, dt_bias, a_neg, d_skip_x, norm_g, bsz, nb):
    m = hb.shape[0]
    cur = lambda bb, i: bb * nb + i
    full = lambda shape: pl.BlockSpec(shape, lambda bb, i: (0, 0))
    return pl.pallas_call(
        _ssd_kernel,
        out_shape=jax.ShapeDtypeStruct((m, D_INNER), BF16),
        grid=(bsz, nb),
        in_specs=[
            pl.BlockSpec((BLOCK, D_MODEL), lambda bb, i: (cur(bb, i), 0)),
            full((LANES, D_MODEL)),
            pl.BlockSpec((BLOCK, D_INNER), lambda bb, i: (cur(bb, i), Z_COL // D_INNER)),
            pl.BlockSpec((BLOCK, D_INNER), lambda bb, i: (cur(bb, i), XS_COL // D_INNER)),
            pl.BlockSpec((BLOCK, 512), lambda bb, i: (cur(bb, i), B_COL // 512)),
            pl.BlockSpec((BLOCK, 512), lambda bb, i: (cur(bb, i), C_COL // 512)),
            full((CONV_W, CONV_DIM)), full((1, CONV_DIM)), full((1, LANES)), full((1, LANES)),
            full((1, D_INNER)), full((1, D_INNER)),
        ],
        out_specs=pl.BlockSpec((BLOCK, D_INNER), lambda bb, i: (cur(bb, i), 0)),
        scratch_shapes=[
            pltpu.VMEM((CONV_SLABS, CIN_ROWS, LANES), F32),
            pltpu.VMEM((CONV_SLABS, BLOCK, LANES), F32),
            pltpu.VMEM((D_STATE, D_INNER), F32),
            pltpu.VMEM((BLOCK, D_INNER), F32),
            pltpu.VMEM((BLOCK, LANES), F32),
            pltpu.VMEM((LANES, BLOCK), F32),
            pltpu.VMEM((LANES, BLOCK), F32),
            pltpu.VMEM((LANES, BLOCK), F32),
            pltpu.VMEM((BLOCK, LANES), F32),
        ],
        compiler_params=_cparams(("arbitrary", "arbitrary")),
        name="ssd",
    )(hb, w_dt, main, main, main, main, conv_w, conv_b, dt_bias, a_neg, d_skip_x, norm_g)


def _merge_kernel(hb_ref, at_ref, y_ref, wga_ref, wgs_ref, wa_ref, ws_ref, o_ref,
                  wga_b, wgs_b, wa_b, ws_b):
    @pl.when(pl.program_id(1) == 0)
    def _():
        wga_b[...] = wga_ref[...].astype(BF16)
        wgs_b[...] = wgs_ref[...].astype(BF16)
        wa_b[...] = wa_ref[...].astype(BF16)
        ws_b[...] = ws_ref[...].astype(BF16)

    hb = hb_ref[...]
    ga = lax.dot_general(hb, wga_b[...], _NT, preferred_element_type=F32)
    gs = lax.dot_general(hb, wgs_b[...], _NT, preferred_element_type=F32)
    pa = jnp.dot(at_ref[...], wa_b[...], preferred_element_type=F32)
    ps = jnp.dot(y_ref[...], ws_b[...], preferred_element_type=F32)
    o_ref[...] = (_sigmoid(ga) * pa + _sigmoid(gs) * ps).astype(o_ref.dtype)


def _merge(hb, attn, y, w_gates, w_a, w_s):
    m = hb.shape[0]
    tm = _row_tile(m // BLOCK, 5)
    nt = D_MODEL // COL_TILE
    once = pl.Buffered(1)
    return pl.pallas_call(
        _merge_kernel,
        out_shape=jax.ShapeDtypeStruct((m, D_MODEL), BF16),
        grid=(nt, m // tm),
        in_specs=[
            pl.BlockSpec((tm, D_MODEL), lambda j, i: (i, 0)),
            pl.BlockSpec((tm, Q_DIM), lambda j, i: (i, 0)),
            pl.BlockSpec((tm, D_INNER), lambda j, i: (i, 0)),
            pl.BlockSpec((COL_TILE, D_MODEL), lambda j, i: (j, 0), pipeline_mode=once),
            pl.BlockSpec((COL_TILE, D_MODEL), lambda j, i: (j + nt, 0), pipeline_mode=once),
            pl.BlockSpec((Q_DIM, COL_TILE), lambda j, i: (0, j), pipeline_mode=once),
            pl.BlockSpec((D_INNER, COL_TILE), lambda j, i: (0, j), pipeline_mode=once),
        ],
        out_specs=pl.BlockSpec((tm, COL_TILE), lambda j, i: (i, j)),
        scratch_shapes=[
            pltpu.VMEM((COL_TILE, D_MODEL), BF16), pltpu.VMEM((COL_TILE, D_MODEL), BF16),
            pltpu.VMEM((Q_DIM, COL_TILE), BF16), pltpu.VMEM((D_INNER, COL_TILE), BF16),
        ],
        compiler_params=_cparams(("arbitrary", "arbitrary")),
        name="branch_merge",
    )(hb, attn, y, w_gates, w_gates, w_a, w_s)


def _outproj_router_kernel(mg_ref, h_ref, wo_ref, g_ref, b_ref, wr_ref, br_ref,
                           o_ref, ids_ref, wts_ref, cnt_ref, carry_ref, *, nb, tm):
    step = pl.program_id(0)

    @pl.when(step == 0)
    def _():
        carry_ref[...] = jnp.zeros_like(carry_ref)

    mix = jnp.dot(mg_ref[...], wo_ref[...], preferred_element_type=F32)
    h1 = _ln_rows(ALPHA * h_ref[...] + mix, g_ref[...], b_ref[...])
    o_ref[...] = h1

    h_hi = h1.astype(BF16)
    h_lo = (h1 - h_hi.astype(F32)).astype(BF16)
    hh = lax.dot_general(h_hi, wr_ref[...], _NT, preferred_element_type=F32)
    lh = lax.dot_general(h_lo, wr_ref[0:LANES, :], _NT, preferred_element_type=F32)
    logits = hh[:, 0:LANES] + (lh + hh[:, LANES:2 * LANES]) + br_ref[...]
    lt = logits.T
    lane128 = lax.broadcasted_iota(jnp.int32, (1, BLOCK), 1)
    blocks = tm // BLOCK
    real = jnp.concatenate(
        [(((step * blocks + s) % nb) > 0) | (lane128 >= PAD) for s in range(blocks)], axis=1)
    sub8 = lax.broadcasted_iota(jnp.int32, (EXPERTS_PER_GROUP, tm), 0)
    sub = lax.broadcasted_iota(jnp.int32, (LANES, tm), 0)

    gl = lt[0:MOE_GROUPS]
    ge = jnp.exp(gl - jnp.max(gl, axis=0, keepdims=True))
    pg = ge / jnp.sum(ge, axis=0, keepdims=True)
    p_top = jnp.max(pg, axis=0, keepdims=True)
    g_idx = jnp.min(jnp.where(pg == p_top, sub8, MOE_GROUPS), axis=0, keepdims=True)

    el = lt[MOE_GROUPS:MOE_GROUPS + EXPERTS_PER_GROUP]
    for g in range(1, MOE_GROUPS):
        r0 = MOE_GROUPS + g * EXPERTS_PER_GROUP
        el = jnp.where(g_idx == g, lt[r0:r0 + EXPERTS_PER_GROUP], el)
    ee = jnp.exp(el - jnp.max(el, axis=0, keepdims=True))
    pe = ee / jnp.sum(ee, axis=0, keepdims=True)
    v1 = jnp.max(pe, axis=0, keepdims=True)
    i1 = jnp.min(jnp.where(pe == v1, sub8, EXPERTS_PER_GROUP), axis=0, keepdims=True)
    pe2 = jnp.where(sub8 == i1, -1.0, pe)
    v2 = jnp.max(pe2, axis=0, keepdims=True)
    i2 = jnp.min(jnp.where(pe2 == v2, sub8, EXPERTS_PER_GROUP), axis=0, keepdims=True)
    vs = v1 + v2
    w1 = p_top * (v1 / vs)
    w2 = p_top * (v2 / vs)
    e1 = g_idx * EXPERTS_PER_GROUP + i1
    e2 = g_idx * EXPERTS_PER_GROUP + i2

    hit1 = (sub == e1 + MOE_GROUPS) & real
    hit2 = (sub == e2 + MOE_GROUPS) & real
    onehot = jnp.where(hit1 | hit2, 1.0, 0.0)
    rr = lax.broadcasted_iota(jnp.int32, (tm, tm), 0)
    cl = lax.broadcasted_iota(jnp.int32, (tm, tm), 1)
    before = jnp.where(rr > cl, 1.0, 0.0).astype(BF16)
    cum = jnp.dot(before, onehot.T.astype(BF16), preferred_element_type=F32).T + carry_ref[...]
    r1 = jnp.sum(jnp.where(hit1, cum, 0.0), axis=0, keepdims=True)
    r2 = jnp.sum(jnp.where(hit2, cum, 0.0), axis=0, keepdims=True)
    carry_ref[...] = carry_ref[...] + jnp.sum(onehot, axis=1, keepdims=True)
    cnt_ref[...] = carry_ref[...].astype(jnp.int32)

    ids_ref[...] = jnp.where(sub8 == 0, jnp.where(real, e1, -1), jnp.where(
        sub8 == 1, jnp.where(real, e2, -1), jnp.where(
            sub8 == 2, r1.astype(jnp.int32), jnp.where(sub8 == 3, r2.astype(jnp.int32), 0))))
    wts_ref[...] = jnp.where(sub == 0, w1, jnp.where(sub == 1, w2, 0.0)).T[:, 0:8]


def _outproj_router(merged, h0, w_o, g, b, wr_hilo, b_r, nb):
    m = h0.shape[0]
    tm = _row_tile(m // BLOCK, 5)
    row_blk = lambda width: pl.BlockSpec((tm, width), lambda i: (i, 0))
    full = lambda shape: pl.BlockSpec(shape, lambda i: (0, 0))
    return pl.pallas_call(
        functools.partial(_outproj_router_kernel, nb=nb, tm=tm),
        out_shape=(jax.ShapeDtypeStruct((m, D_MODEL), F32),
                   jax.ShapeDtypeStruct((8, m), jnp.int32), jax.ShapeDtypeStruct((m, 8), F32),
                   jax.ShapeDtypeStruct((LANES, 1), jnp.int32)),
        grid=(m // tm,),
        in_specs=[row_blk(D_MODEL), row_blk(D_MODEL), full((D_MODEL, D_MODEL)),
                  full((1, D_MODEL)), full((1, D_MODEL)),
                  full((2 * LANES, D_MODEL)), full((1, LANES))],
        out_specs=(row_blk(D_MODEL), pl.BlockSpec((8, tm), lambda i: (0, i)), row_blk(8),
                   full((LANES, 1))),
        scratch_shapes=[pltpu.VMEM((LANES, 1), F32)],
        compiler_params=_cparams(("arbitrary",)),
        name="out_proj_ln1_router",
    )(merged, h0, w_o, g, b, wr_hilo, b_r)


HALF = D_MODEL // 2
HI_MASK = 0xFFFF0000


def _pack_rows(x):
    lo = lax.bitcast_convert_type(x[:, :HALF].astype(BF16).astype(F32), jnp.uint32) >> 16
    hi = lax.bitcast_convert_type(x[:, HALF:].astype(BF16).astype(F32), jnp.uint32) & jnp.uint32(HI_MASK)
    return hi | lo


def _unpack_rows(w):
    lo = lax.bitcast_convert_type(w << 16, F32)
    hi = lax.bitcast_convert_type(w & jnp.uint32(HI_MASK), F32)
    return lo, hi


ROW_SUB = HALF // LANES


def _put_tile_rows(ref, lead, x):
    for s in range(ROW_SUB):
        ref[lead + (pl.ds(s, x.shape[0], stride=ROW_SUB), slice(None))] = x[:, s * LANES:(s + 1) * LANES]


def _get_tile_rows(ref, lead, r):
    return jnp.concatenate(
        [ref[lead + (pl.ds(s, r, stride=ROW_SUB), slice(None))] for s in range(ROW_SUB)], axis=1)


def _tile_rows(ref, row, n=1):
    return ref.at[pl.ds(pl.multiple_of(row * ROW_SUB, ROW_SUB), n * ROW_SUB), :]


def _dest_kernel(pstart_ref, ids_ref, dest_ref, *, n_slots):
    e = ids_ref[0:TOP_K, :]
    acc = jnp.zeros_like(e)
    for ex in range(N_EXPERTS):
        acc = jnp.where(e == ex, pstart_ref[ex], acc)
    k = lax.broadcasted_iota(jnp.int32, e.shape, 0)
    t = lax.broadcasted_iota(jnp.int32, e.shape, 1)
    dump = n_slots + k * PAD + (t & (BLOCK - 1))
    dest_ref[...] = jnp.where(e >= 0, acc + ids_ref[TOP_K:2 * TOP_K, :], dump)


def _dest_rows(pstart, ids, n_slots):
    m = ids.shape[1]
    return pl.pallas_call(
        functools.partial(_dest_kernel, n_slots=n_slots),
        out_shape=jax.ShapeDtypeStruct((TOP_K, m), jnp.int32),
        in_specs=[pl.BlockSpec(memory_space=pltpu.SMEM), pl.BlockSpec(memory_space=pltpu.VMEM)],
        out_specs=pl.BlockSpec(memory_space=pltpu.VMEM),
        name="moe_dest",
    )(pstart, ids)


def _dispatch_kernel(dest_ref, zlo_ref, zhi_ref, nused_ref, h_ref, xs_hbm, ring, zrow, zblk, sems, zsem,
                     *, n_blocks):
    step = pl.program_id(0)
    n_steps = pl.num_programs(0)
    n_rows = n_steps * BLOCK
    slot = step % 2

    def zero_rows(start):
        def per_expert(ex, carry):
            def per_row(p, c):
                cp = pltpu.make_async_copy(zrow, _tile_rows(xs_hbm, p), zsem)
                cp.start() if start else cp.wait()
                return c
            return lax.fori_loop(zlo_ref[ex], zhi_ref[ex], per_row, carry)
        lax.fori_loop(0, N_EXPERTS, per_expert, 0)

        def per_block(blk, c):
            cp = pltpu.make_async_copy(zblk, _tile_rows(xs_hbm, blk * MOE_ROWS, MOE_ROWS), zsem)
            cp.start() if start else cp.wait()
            return c
        lax.fori_loop(nused_ref[0], n_blocks, per_block, 0)

    @pl.when(step == 0)
    def _():
        zrow[...] = jnp.zeros_like(zrow)
        zblk[...] = jnp.zeros_like(zblk)
        zero_rows(True)

    _put_tile_rows(ring, (slot,), _pack_rows(h_ref[...]))
    base = step * BLOCK

    def body(t, carry):
        for k in range(TOP_K):
            d = dest_ref[k * n_rows + base + t]
            pltpu.make_async_copy(_tile_rows(ring.at[slot], t), _tile_rows(xs_hbm, d),
                                  sems.at[slot]).start(priority=k)
        return carry
    lax.fori_loop(0, BLOCK, body, 0, unroll=4)

    def wait_slot(s):
        for k in range(TOP_K):
            pltpu.make_async_copy(ring.at[s], _tile_rows(xs_hbm, 0, BLOCK), sems.at[s]).wait()

    @pl.when(step > 0)
    def _():
        wait_slot(1 - slot)

    @pl.when(step == n_steps - 1)
    def _():
        wait_slot(slot)
        zero_rows(False)


def _dispatch(h1, dest, zlo, zhi, n_used, n_blocks):
    m = h1.shape[0]
    n_slots = n_blocks * MOE_ROWS
    grid_spec = pltpu.PrefetchScalarGridSpec(
        num_scalar_prefetch=4,
        grid=(m // BLOCK,),
        in_specs=[pl.BlockSpec((BLOCK, D_MODEL), lambda i, *_: (i, 0))],
        out_specs=pl.BlockSpec(memory_space=pl.ANY),
        scratch_shapes=[
            pltpu.VMEM((2, BLOCK * ROW_SUB, LANES), jnp.uint32),
            pltpu.VMEM((ROW_SUB, LANES), jnp.uint32),
            pltpu.VMEM((MOE_ROWS * ROW_SUB, LANES), jnp.uint32),
            pltpu.SemaphoreType.DMA((2,)),
            pltpu.SemaphoreType.DMA(()),
        ],
    )
    return pl.pallas_call(
        functools.partial(_dispatch_kernel, n_blocks=n_blocks),
        out_shape=jax.ShapeDtypeStruct(((n_slots + TOP_K * PAD) * ROW_SUB, LANES), jnp.uint32),
        grid_spec=grid_spec,
        compiler_params=_cparams(("arbitrary",)),
        name="moe_dispatch",
    )(dest, zlo, zhi, n_used, h1)


def _expert_kernel(blk_e_ref, nused_ref, xs_ref, wg_ref, wu_ref, wd_ref, o_ref):
    i = pl.program_id(0)
    n_used = nused_ref[0]

    @pl.when(i < n_used)
    def _():
        lo, hi = _unpack_rows(_get_tile_rows(xs_ref, (), MOE_ROWS))
        x_lo, x_hi = lo.astype(BF16), hi.astype(BF16)
        gate = (jnp.dot(x_lo, wg_ref[0:HALF, :].astype(BF16), preferred_element_type=F32)
                + jnp.dot(x_hi, wg_ref[HALF:D_MODEL, :].astype(BF16), preferred_element_type=F32))
        up = (jnp.dot(x_lo, wu_ref[0:HALF, :].astype(BF16), preferred_element_type=F32)
              + jnp.dot(x_hi, wu_ref[HALF:D_MODEL, :].astype(BF16), preferred_element_type=F32))
        hid = (_silu(gate) * up).astype(BF16)
        _put_tile_rows(o_ref, (), _pack_rows(jnp.dot(hid, wd_ref[...].astype(BF16),
                                                     preferred_element_type=F32)))


def _experts(xs, w_gate, w_up, w_down, blk_e, n_used, n_blocks):
    used = lambda i, nu: jnp.minimum(i, nu[0] - 1)
    grid_spec = pltpu.PrefetchScalarGridSpec(
        num_scalar_prefetch=2,
        grid=(n_blocks,),
        in_specs=[
            pl.BlockSpec((MOE_ROWS * ROW_SUB, LANES), lambda i, be, nu: (used(i, nu), 0)),
            pl.BlockSpec((None, None, D_MODEL, D_EXPERT), lambda i, be, nu: (0, be[i], 0, 0)),
            pl.BlockSpec((None, None, D_MODEL, D_EXPERT), lambda i, be, nu: (0, be[i], 0, 0)),
            pl.BlockSpec((None, None, D_EXPERT, D_MODEL), lambda i, be, nu: (0, be[i], 0, 0)),
        ],
        out_specs=pl.BlockSpec((MOE_ROWS * ROW_SUB, LANES), lambda i, be, nu: (used(i, nu), 0)),
    )
    return pl.pallas_call(
        _expert_kernel,
        out_shape=jax.ShapeDtypeStruct(xs.shape, xs.dtype),
        grid_spec=grid_spec,
        input_output_aliases={2: 0},
        compiler_params=_cparams(("arbitrary",)),
        name="moe_experts",
    )(blk_e, n_used, xs, w_gate, w_up, w_down)


def _combine_kernel(dest_ref, h_ref, wts_ref, yb_hbm, g_ref, b_ref, o_ref, ybuf, sems, *, nb):
    bb = pl.program_id(0)
    i = pl.program_id(1)
    n_i = pl.num_programs(1)
    step = bb * n_i + i
    n_steps = pl.num_programs(0) * n_i

    n_rows = pl.num_programs(0) * nb * BLOCK

    def assign_base(b_, i_):
        return (b_ * nb + i_ + 1) * BLOCK

    def start_gather(base, slot):
        def body(t, carry):
            for k in range(TOP_K):
                d = dest_ref[k * n_rows + base + t]
                pltpu.make_async_copy(_tile_rows(yb_hbm, d), _tile_rows(ybuf.at[slot, k], t),
                                      sems.at[slot]).start(priority=k)
            return carry
        lax.fori_loop(0, BLOCK, body, 0, unroll=4)

    def wait_gather(slot):
        for k in range(TOP_K):
            pltpu.make_async_copy(_tile_rows(yb_hbm, 0, BLOCK), ybuf.at[slot, k], sems.at[slot]).wait()

    @pl.when(step == 0)
    def _():
        start_gather(assign_base(0, 0), 0)

    @pl.when(step + 1 < n_steps)
    def _():
        nxt = jnp.where(i + 1 < n_i, assign_base(bb, i + 1), assign_base(bb + 1, 0))
        start_gather(nxt, (step + 1) % 2)

    slot = step % 2
    wait_gather(slot)
    wts = wts_ref[...]
    lo0, hi0 = _unpack_rows(_get_tile_rows(ybuf, (slot, 0), BLOCK))
    lo1, hi1 = _unpack_rows(_get_tile_rows(ybuf, (slot, 1), BLOCK))
    w0, w1 = wts[:, 0:1], wts[:, 1:2]
    r_lo = ALPHA * h_ref[:, 0:HALF] + (lo0 * w0 + lo1 * w1)
    r_hi = ALPHA * h_ref[:, HALF:D_MODEL] + (hi0 * w0 + hi1 * w1)
    mu = (jnp.sum(r_lo, axis=-1, keepdims=True) + jnp.sum(r_hi, axis=-1, keepdims=True)) / D_MODEL
    c_lo, c_hi = r_lo - mu, r_hi - mu
    var = (jnp.sum(c_lo * c_lo, axis=-1, keepdims=True)
           + jnp.sum(c_hi * c_hi, axis=-1, keepdims=True)) / D_MODEL
    inv = lax.rsqrt(var + LN_EPS)
    o_ref[:, 0:HALF] = c_lo * inv * g_ref[:, 0:HALF] + b_ref[:, 0:HALF]
    o_ref[:, HALF:D_MODEL] = c_hi * inv * g_ref[:, HALF:D_MODEL] + b_ref[:, HALF:D_MODEL]


def _combine_ln(h1, wts, yb, dest, g, b, bsz, nb):
    grid_spec = pltpu.PrefetchScalarGridSpec(
        num_scalar_prefetch=1,
        grid=(bsz, nb - 1),
        in_specs=[
            pl.BlockSpec((BLOCK, D_MODEL), lambda bb, i, d: (bb * nb + i + 1, 0)),
            pl.BlockSpec((BLOCK, 8), lambda bb, i, d: (bb * nb + i + 1, 0)),
            pl.BlockSpec(memory_space=pl.ANY),
            pl.BlockSpec((1, D_MODEL), lambda bb, i, d: (0, 0)),
            pl.BlockSpec((1, D_MODEL), lambda bb, i, d: (0, 0)),
        ],
        out_specs=pl.BlockSpec((None, BLOCK, D_MODEL), lambda bb, i, d: (bb, i, 0)),
        scratch_shapes=[
            pltpu.VMEM((2, TOP_K, BLOCK * ROW_SUB, LANES), jnp.uint32),
            pltpu.SemaphoreType.DMA((2,)),
        ],
    )
    return pl.pallas_call(
        functools.partial(_combine_kernel, nb=nb),
        out_shape=jax.ShapeDtypeStruct((bsz, (nb - 1) * BLOCK, D_MODEL), F32),
        grid_spec=grid_spec,
        compiler_params=_cparams(("arbitrary", "arbitrary")),
        name="moe_combine_ln2",
    )(dest, h1, wts, yb, g, b)


def _dispatch_tables(ids, counts, n_blocks):
    counts = counts[MOE_GROUPS:MOE_GROUPS + N_EXPERTS, 0]
    pcounts = (counts + MOE_ROWS - 1) // MOE_ROWS * MOE_ROWS
    pend = jnp.cumsum(pcounts)
    pstart = (pend - pcounts).astype(jnp.int32)
    n_used = jnp.maximum(pend[-1] // MOE_ROWS, 1).astype(jnp.int32)
    blk = jnp.arange(n_blocks, dtype=jnp.int32)
    blk_e = jnp.minimum(jnp.sum(pend[None, :] <= (blk * MOE_ROWS)[:, None], axis=1), N_EXPERTS - 1)
    blk_e = jnp.where(blk < n_used, blk_e, blk_e[n_used - 1]).astype(jnp.int32)
    dest = _dest_rows(pstart, ids, n_blocks * MOE_ROWS).reshape(-1)
    return dest, pstart, (pstart + counts).astype(jnp.int32), pend.astype(jnp.int32), blk_e, n_used.reshape(1)


def kernel(x, meta_tokens, ln_emb_g, ln_emb_b, w_in, conv_w, conv_b, dt_bias, a_log, d_skip, ssd_norm_g, sinks, w_br_attn, w_br_ssd, w_o, ln1_g, ln1_b, w_router_group, b_router_group, w_router_expert, b_router_expert, w_gate, w_up, w_down, ln2_g, ln2_b):
    bsz, seq, d = x.shape
    assert d == D_MODEL and seq % BLOCK == 0 and w_in.shape[0] == DEPTH
    nb = seq // BLOCK + 1
    m = bsz * nb * BLOCK
    row2 = lambda v: v.reshape(1, -1).astype(F32)

    meta_pad = jnp.concatenate([jnp.zeros((PAD, d), F32), meta_tokens.astype(F32)], axis=0)
    h0, h0b = _embed_ln(x, meta_pad, row2(ln_emb_g), row2(ln_emb_b))

    w_in_t = jnp.swapaxes(w_in, 1, 2)
    main = _inproj(h0b, w_in_t)
    w_dt = jnp.pad(w_in_t[0, DT_OFF:DT_OFF + SSD_HEADS], ((0, LANES - SSD_HEADS), (0, 0))).astype(BF16)
    pad_h = lambda v: jnp.pad(v.astype(F32), (0, LANES - SSD_HEADS)).reshape(1, LANES)
    attn = _attention(main, sinks[0].astype(F32), bsz, nb)
    y = _ssd(h0b, main, w_dt, conv_w[0].astype(F32), row2(conv_b[0]), pad_h(dt_bias[0]),
             pad_h(-jnp.exp(a_log[0].astype(F32))), row2(jnp.repeat(d_skip[0], SSD_HEADDIM)),
             row2(ssd_norm_g[0]), bsz, nb)
    merged = _merge(h0b, attn, y, w_in_t[0, GATE_OFF:], w_br_attn[0], w_br_ssd[0])
    w_r = jnp.pad(jnp.concatenate([w_router_group[0].T, w_router_expert[0].T], axis=0).astype(F32),
                  ((0, LANES - MOE_GROUPS - N_EXPERTS), (0, 0)))
    wr_hi = w_r.astype(BF16)
    wr_lo = (w_r - wr_hi.astype(F32)).astype(BF16)
    b_r = jnp.pad(jnp.concatenate([b_router_group[0], b_router_expert[0]]).astype(F32),
                  (0, LANES - MOE_GROUPS - N_EXPERTS)).reshape(1, LANES)
    h1, ids, wts, counts = _outproj_router(merged, h0, w_o[0].astype(BF16), row2(ln1_g[0]),
                                           row2(ln1_b[0]), jnp.concatenate([wr_hi, wr_lo], axis=0), b_r, nb)

    n_assign = bsz * (seq + N_META) * TOP_K
    n_blocks = -(-n_assign // MOE_ROWS) + N_EXPERTS
    dest, _, zlo, zhi, blk_e, n_used = _dispatch_tables(ids, counts, n_blocks)
    xs = _dispatch(h1, dest, zlo, zhi, n_used, n_blocks)
    yb = _experts(xs, w_gate, w_up, w_down, blk_e, n_used, n_blocks)
    return _combine_ln(h1, wts, yb, dest, row2(ln2_g[0]), row2(ln2_b[0]), bsz, nb)
```

```python
import functools

import jax
import jax.numpy as jnp
from jax import lax
from jax.experimental import pallas as pl
from jax.experimental.pallas import tpu as pltpu

F32 = jnp.float32
BF16 = jnp.bfloat16
HIGHEST = lax.Precision.HIGHEST

D_MODEL = 2048
N_META = 16
BLOCK = 128
PAD = BLOCK - N_META
WINDOW = 128
HQ, HKV, HD = 16, 4, 64
Q_PER_KV = HQ // HKV
D_INNER = 2048
SSD_HEADDIM = 64
SSD_HEADS = D_INNER // SSD_HEADDIM
SSD_GROUPS = 4
HEADS_PER_GROUP = SSD_HEADS // SSD_GROUPS
D_STATE = 128
CONV_W = 4
CONV_DIM = D_INNER + 2 * SSD_GROUPS * D_STATE
MOE_GROUPS = 8
EXPERTS_PER_GROUP = 8
N_EXPERTS = MOE_GROUPS * EXPERTS_PER_GROUP
TOP_K = 2
D_EXPERT = 512
LN_EPS = 1e-5
RMS_EPS = 1e-5
NEG_INF = -1e30
DEPTH = 1
ALPHA = (2.0 * DEPTH) ** 0.25

Q_DIM = HQ * HD
KV_DIM = HKV * HD
MAIN_DIM = Q_DIM + 2 * KV_DIM + D_INNER + CONV_DIM
DT_OFF = MAIN_DIM
GATE_OFF = MAIN_DIM + SSD_HEADS
COL_TILE = 512
MAIN_TILES = MAIN_DIM // COL_TILE
Z_COL, XS_COL, Q_COL, K_COL, V_COL, B_COL, C_COL = 0, 2048, 4096, 5120, 5376, 5632, 6144

LANES = 128
SUBLANES = 8
MOE_ROWS = 576
VMEM_LIMIT = 56 * 1024 * 1024
ROW_TILE_BLOCKS = 5
IN_PROJ_TILE_BLOCKS = 13
BC_DIM = SSD_GROUPS * D_STATE
DMA_UNROLL = 4
RING_SLOTS = 3


def _cparams(sem, vmem=VMEM_LIMIT):
    return pltpu.CompilerParams(dimension_semantics=sem, vmem_limit_bytes=vmem)


def _ln_rows(x, g, b):
    mu = jnp.mean(x, axis=-1, keepdims=True)
    xc = x - mu
    var = jnp.mean(xc * xc, axis=-1, keepdims=True)
    return xc * lax.rsqrt(var + LN_EPS) * g + b


def _sigmoid(x):
    return 1.0 / (1.0 + jnp.exp(-x))


def _silu(x):
    return x * _sigmoid(x)


def _row_tile(n_blocks, max_blocks):
    best = 1
    for c in range(1, max_blocks + 1):
        if n_blocks % c == 0:
            best = c
    return best * BLOCK


def _embed_ln_kernel(*refs, tile_blocks):
    x_refs = refs[:tile_blocks]
    meta_ref, g_ref, b_ref, h_ref, hb_ref = refs[tile_blocks:]
    t = pl.program_id(1)
    row = lax.broadcasted_iota(jnp.int32, (BLOCK, 1), 0)
    for u in range(tile_blocks):
        rows = slice(u * BLOCK, (u + 1) * BLOCK)
        if u == 0:
            src = jnp.where(t == 0, meta_ref[...], x_refs[0][...])
            y = jnp.where((t > 0) | (row >= PAD), _ln_rows(src, g_ref[...], b_ref[...]), 0.0)
        else:
            y = _ln_rows(x_refs[u][...], g_ref[...], b_ref[...])
        h_ref[rows, :] = y
        hb_ref[rows, :] = y.astype(BF16)


def _embed_ln(x, meta_pad, g, b):
    bsz, seq, d = x.shape
    nb = seq // BLOCK + 1
    m = bsz * nb * BLOCK
    tile_blocks = _row_tile(nb, ROW_TILE_BLOCKS) // BLOCK
    tm = tile_blocks * BLOCK
    nt = nb // tile_blocks
    x_spec = lambda u: pl.BlockSpec(
        (None, BLOCK, d), lambda bb, t: (bb, jnp.maximum(t * tile_blocks + u - 1, 0), 0))
    return pl.pallas_call(
        functools.partial(_embed_ln_kernel, tile_blocks=tile_blocks),
        out_shape=(jax.ShapeDtypeStruct((m, d), F32), jax.ShapeDtypeStruct((m, d), BF16)),
        grid=(bsz, nt),
        in_specs=[x_spec(u) for u in range(tile_blocks)] + [
            pl.BlockSpec((BLOCK, d), lambda bb, t: (0, 0)),
            pl.BlockSpec((1, d), lambda bb, t: (0, 0)),
            pl.BlockSpec((1, d), lambda bb, t: (0, 0)),
        ],
        out_specs=(pl.BlockSpec((tm, d), lambda bb, t: (bb * nt + t, 0)),
                   pl.BlockSpec((tm, d), lambda bb, t: (bb * nt + t, 0))),
        compiler_params=_cparams(("parallel", "parallel")),
        name="embed_ln",
    )(*([x] * tile_blocks), meta_pad, g, b)


_NT = (((1,), (1,)), ((), ()))


def _inproj_kernel(a_ref, wt_ref, o_ref):
    o_ref[...] = lax.dot_general(a_ref[...], wt_ref[...].astype(BF16), _NT,
                                 preferred_element_type=F32).astype(o_ref.dtype)


def _main_dest_tile(j):
    return jnp.where(j < 2, j + 8, jnp.where(j == 2, 10, jnp.where(j < 11, j - 3, j)))


def _inproj(hb, w_in_t):
    m, d = hb.shape
    tm = _row_tile(m // BLOCK, IN_PROJ_TILE_BLOCKS)
    return pl.pallas_call(
        _inproj_kernel,
        out_shape=jax.ShapeDtypeStruct((m, MAIN_DIM), BF16),
        grid=(m // tm, MAIN_TILES),
        in_specs=[
            pl.BlockSpec((tm, d), lambda i, j: (i, 0)),
            pl.BlockSpec((None, COL_TILE, d), lambda i, j: (0, j, 0)),
        ],
        out_specs=pl.BlockSpec((tm, COL_TILE), lambda i, j: (i, _main_dest_tile(j))),
        compiler_params=_cparams(("parallel", "arbitrary")),
        name="in_proj",
    )(hb, w_in_t)


def _attn_body(n, sinks_ref, q_ref, kc_ref, vc_ref, kp_ref, vp_ref, km_ref, vm_ref, o_ref):
    r = lax.broadcasted_iota(jnp.int32, (BLOCK, BLOCK), 0)
    c = lax.broadcasted_iota(jnp.int32, (BLOCK, BLOCK), 1)
    assert WINDOW == BLOCK
    is_cur = c <= r
    ok_band = (is_cur & (n >= 1)) | ((c > r) & (n >= 2))
    rm = lax.broadcasted_iota(jnp.int32, (BLOCK, N_META), 0)
    cm = lax.broadcasted_iota(jnp.int32, (BLOCK, N_META), 1)
    ok_meta = (n > 0) | (cm <= rm - PAD)
    scale = HD ** -0.5
    nt = (((1,), (1,)), ((), ()))
    for h in range(HKV):
        ks = slice(h * HD, (h + 1) * HD)
        kc, kp, km = kc_ref[:, ks], kp_ref[:, ks], km_ref[PAD:BLOCK, ks]
        vc, vp, vm = vc_ref[:, ks], vp_ref[:, ks], vm_ref[PAD:BLOCK, ks]
        q4 = jnp.concatenate(
            [q_ref[:, (h * Q_PER_KV + g) * HD:(h * Q_PER_KV + g + 1) * HD] for g in range(Q_PER_KV)],
            axis=0)
        s_cp4 = lax.dot_general(q4, jnp.concatenate([kc, kp], axis=0), nt,
                                preferred_element_type=F32)
        s_c4, s_p4 = s_cp4[:, 0:BLOCK], s_cp4[:, BLOCK:2 * BLOCK]
        s_m4 = lax.dot_general(q4, km, nt, preferred_element_type=F32)
        pcp, pm, dens = [], [], []
        for g in range(Q_PER_KV):
            rows = slice(g * BLOCK, (g + 1) * BLOCK)
            sink = sinks_ref[h * Q_PER_KV + g]
            s_b = jnp.where(ok_band, jnp.where(is_cur, s_c4[rows], s_p4[rows]) * scale, NEG_INF)
            s_m = jnp.where(ok_meta, s_m4[rows] * scale, NEG_INF)
            mx = jnp.maximum(jnp.maximum(jnp.max(s_b, axis=-1, keepdims=True),
                                         jnp.max(s_m, axis=-1, keepdims=True)), sink)
            p_b = jnp.exp(s_b - mx)
            p_m = jnp.exp(s_m - mx)
            dens.append(jnp.sum(p_b, axis=-1, keepdims=True) + jnp.sum(p_m, axis=-1, keepdims=True)
                        + jnp.exp(sink - mx))
            pcp.append(jnp.concatenate([jnp.where(is_cur, p_b, 0.0).astype(BF16),
                                        jnp.where(is_cur, 0.0, p_b).astype(BF16)], axis=1))
            pm.append(p_m.astype(BF16))
        o4 = (jnp.dot(jnp.concatenate(pcp, axis=0), jnp.concatenate([vc, vp], axis=0),
                      preferred_element_type=F32)
              + jnp.dot(jnp.concatenate(pm, axis=0), vm, preferred_element_type=F32))
        for g in range(Q_PER_KV):
            hq = h * Q_PER_KV + g
            o_ref[:, hq * HD:(hq + 1) * HD] = (o4[g * BLOCK:(g + 1) * BLOCK] / dens[g]).astype(o_ref.dtype)


def _attn_kernel(*refs):
    _attn_body(pl.program_id(1), *refs)


def _attention(main, sinks, bsz, nb):
    m = main.shape[0]
    kb, vb = K_COL // KV_DIM, V_COL // KV_DIM
    cur = lambda bb, i: bb * nb + i
    prev = lambda bb, i: bb * nb + jnp.maximum(i - 1, 0)
    first = lambda bb, i: bb * nb
    return pl.pallas_call(
        _attn_kernel,
        out_shape=jax.ShapeDtypeStruct((m, Q_DIM), BF16),
        grid=(bsz, nb),
        in_specs=[
            pl.BlockSpec(memory_space=pltpu.SMEM),
            pl.BlockSpec((BLOCK, Q_DIM), lambda bb, i: (cur(bb, i), Q_COL // Q_DIM)),
            pl.BlockSpec((BLOCK, KV_DIM), lambda bb, i: (cur(bb, i), kb)),
            pl.BlockSpec((BLOCK, KV_DIM), lambda bb, i: (cur(bb, i), vb)),
            pl.BlockSpec((BLOCK, KV_DIM), lambda bb, i: (prev(bb, i), kb)),
            pl.BlockSpec((BLOCK, KV_DIM), lambda bb, i: (prev(bb, i), vb)),
            pl.BlockSpec((BLOCK, KV_DIM), lambda bb, i: (first(bb, i), kb)),
            pl.BlockSpec((BLOCK, KV_DIM), lambda bb, i: (first(bb, i), vb)),
        ],
        out_specs=pl.BlockSpec((BLOCK, Q_DIM), lambda bb, i: (cur(bb, i), 0)),
        compiler_params=_cparams(("parallel", "parallel")),
        name="swa_attention",
    )(sinks, main, main, main, main, main, main, main)


CIN_ROWS = BLOCK + SUBLANES
CONV_SLABS = CONV_DIM // LANES
B_SLAB = D_INNER // LANES


def _ssd_body(i, hb_ref, wdt_ref, z_ref, xs_ref, bm_ref, cm_ref,
              cw_ref, cbias_ref, dtb_ref, a_ref, dsk_ref, ng_ref,
              y_ref,
              cin_ref, xc_ref, state_ref, yacc_ref, acs_ref, acst_ref, dtt_ref, wt_ref, ea_ref):
    row = lax.broadcasted_iota(jnp.int32, (BLOCK, 1), 0)
    live = jnp.where((i > 0) | (row >= PAD), 1.0, 0.0)

    @pl.when(i == 0)
    def _():
        state_ref[...] = jnp.zeros_like(state_ref)
        cin_ref[:, 0:SUBLANES, :] = jnp.zeros((CONV_SLABS, SUBLANES, LANES), F32)

    @pl.when(i > 0)
    def _():
        cin_ref[:, 0:SUBLANES, :] = cin_ref[:, BLOCK:CIN_ROWS, :]

    for sl in range(CONV_SLABS):
        cs = slice(sl * LANES, (sl + 1) * LANES)
        if sl < D_INNER // LANES:
            src = xs_ref[:, cs]
        elif sl < (D_INNER + BC_DIM) // LANES:
            src = bm_ref[:, sl * LANES - D_INNER:(sl + 1) * LANES - D_INNER]
        else:
            src = cm_ref[:, sl * LANES - D_INNER - BC_DIM:(sl + 1) * LANES - D_INNER - BC_DIM]
        cin_ref[sl, SUBLANES:CIN_ROWS, :] = src.astype(F32)
        acc = jnp.broadcast_to(cbias_ref[:, cs], (BLOCK, LANES))
        for j in range(CONV_W):
            acc = acc + cw_ref[j:j + 1, cs] * cin_ref[sl, pl.ds(SUBLANES - (CONV_W - 1) + j, BLOCK, stride=1), :]
        xc_ref[sl] = _silu(acc)

    @pl.when(i == 0)
    def _():
        xc_ref[:, 0:PAD, :] = jnp.zeros((CONV_SLABS, PAD, LANES), F32)

    dt_raw = lax.dot_general(hb_ref[...], wdt_ref[...], _NT, preferred_element_type=F32) + dtb_ref[...]
    dt = (jnp.maximum(dt_raw, 0.0) + jnp.log1p(jnp.exp(-jnp.abs(dt_raw)))) * live
    adt = dt * a_ref[...]
    rr = lax.broadcasted_iota(jnp.int32, (BLOCK, BLOCK), 0)
    cl = lax.broadcasted_iota(jnp.int32, (BLOCK, BLOCK), 1)
    causal = rr >= cl
    a_cs = jnp.dot(jnp.where(causal, 1.0, 0.0), adt, precision=HIGHEST, preferred_element_type=F32)
    a_cs_t = a_cs.T
    dt_t = dt.T
    last = a_cs_t[:, BLOCK - 1:BLOCK]
    acs_ref[...] = a_cs
    acst_ref[...] = a_cs_t
    dtt_ref[...] = dt_t
    wt_ref[...] = dt_t * jnp.exp(last - a_cs_t)
    ea_ref[...] = jnp.exp(a_cs)
    cd_t = jnp.exp(last)

    lane = lax.broadcasted_iota(jnp.int32, (1, LANES), 1)
    lo = lane < SSD_HEADDIM
    nn = (((1,), (1,)), ((), ()))
    for g in range(SSD_GROUPS):
        bg = xc_ref[B_SLAB + g]
        cg = xc_ref[B_SLAB + SSD_GROUPS + g]
        cb = lax.dot_general(cg.astype(BF16), bg.astype(BF16), nn, preferred_element_type=F32)
        bt = bg.T
        for pp in range(HEADS_PER_GROUP // 2):
            pr = g * (HEADS_PER_GROUP // 2) + pp
            ps = slice(pr * LANES, (pr + 1) * LANES)
            xs_pair = xc_ref[pr]
            st_pair = state_ref[:, ps]
            lhs, lhs_s = [], []
            for hd in (2 * pr, 2 * pr + 1):
                col = jnp.broadcast_to(acs_ref[:, hd:hd + 1], (BLOCK, BLOCK))
                seg = col - acst_ref[hd:hd + 1, :]
                dec = jnp.exp(jnp.where(causal, seg, NEG_INF))
                lhs.append((cb * dec * dtt_ref[hd:hd + 1, :]).astype(BF16))
                lhs.append((cg * jnp.broadcast_to(ea_ref[:, hd:hd + 1], (BLOCK, BLOCK))).astype(BF16))
                lhs_s.append((bt * wt_ref[hd:hd + 1, :]).astype(BF16))
            xs_lo = jnp.where(lo, xs_pair, 0.0).astype(BF16)
            xs_hi = jnp.where(lo, 0.0, xs_pair).astype(BF16)
            st_lo = jnp.where(lo, st_pair, 0.0).astype(BF16)
            st_hi = jnp.where(lo, 0.0, st_pair).astype(BF16)
            y_pair = jnp.dot(jnp.concatenate(lhs, axis=1),
                             jnp.concatenate([xs_lo, st_lo, xs_hi, st_hi], axis=0),
                             preferred_element_type=F32)
            yacc_ref[:, ps] = y_pair + dsk_ref[:, ps] * xs_pair
            st_new = jnp.dot(jnp.concatenate(lhs_s, axis=1),
                             jnp.concatenate([xs_lo, xs_hi], axis=0), preferred_element_type=F32)
            cd_row = jnp.where(lo, jnp.broadcast_to(cd_t[2 * pr:2 * pr + 1, :], (1, LANES)),
                               jnp.broadcast_to(cd_t[2 * pr + 1:2 * pr + 2, :], (1, LANES)))
            state_ref[:, ps] = st_pair * cd_row + st_new

    gw = D_INNER // SSD_GROUPS
    for g in range(SSD_GROUPS):
        gs = slice(g * gw, (g + 1) * gw)
        yz = yacc_ref[:, gs] * _silu(z_ref[:, gs].astype(F32))
        ms = jnp.mean(yz * yz, axis=-1, keepdims=True)
        y_ref[:, gs] = (yz * lax.rsqrt(ms + RMS_EPS) * ng_ref[:, gs]).astype(y_ref.dtype)


def _ssd_kernel(*refs):
    _ssd_body(pl.program_id(1), *refs)


def _ssd(hb, main, w_dt, conv_w, conv_b, dt_bias, a_neg, d_skip_x, norm_g, bsz, nb):
    m = hb.shape[0]
    cur = lambda bb, i: bb * nb + i
    full = lambda shape: pl.BlockSpec(shape, lambda bb, i: (0, 0))
    return pl.pallas_call(
        _ssd_kernel,
        out_shape=jax.ShapeDtypeStruct((m, D_INNER), BF16),
        grid=(bsz, nb),
        in_specs=[
            pl.BlockSpec((BLOCK, D_MODEL), lambda bb, i: (cur(bb, i), 0)),
            full((LANES, D_MODEL)),
            pl.BlockSpec((BLOCK, D_INNER), lambda bb, i: (cur(bb, i), Z_COL // D_INNER)),
            pl.BlockSpec((BLOCK, D_INNER), lambda bb, i: (cur(bb, i), XS_COL // D_INNER)),
            pl.BlockSpec((BLOCK, BC_DIM), lambda bb, i: (cur(bb, i), B_COL // BC_DIM)),
            pl.BlockSpec((BLOCK, BC_DIM), lambda bb, i: (cur(bb, i), C_COL // BC_DIM)),
            full((CONV_W, CONV_DIM)), full((1, CONV_DIM)), full((1, LANES)), full((1, LANES)),
            full((1, D_INNER)), full((1, D_INNER)),
        ],
        out_specs=pl.BlockSpec((BLOCK, D_INNER), lambda bb, i: (cur(bb, i), 0)),
        scratch_shapes=[
            pltpu.VMEM((CONV_SLABS, CIN_ROWS, LANES), F32),
            pltpu.VMEM((CONV_SLABS, BLOCK, LANES), F32),
            pltpu.VMEM((D_STATE, D_INNER), F32),
            pltpu.VMEM((BLOCK, D_INNER), F32),
            pltpu.VMEM((BLOCK, LANES), F32),
            pltpu.VMEM((LANES, BLOCK), F32),
            pltpu.VMEM((LANES, BLOCK), F32),
            pltpu.VMEM((LANES, BLOCK), F32),
            pltpu.VMEM((BLOCK, LANES), F32),
        ],
        compiler_params=_cparams(("arbitrary", "arbitrary")),
        name="ssd",
    )(hb, w_dt, main, main, main, main, conv_w, conv_b, dt_bias, a_neg, d_skip_x, norm_g)


def _merge_kernel(hb_ref, at_ref, y_ref, wga_ref, wgs_ref, wa_ref, ws_ref, o_ref,
                  wga_b, wgs_b, wa_b, ws_b):
    @pl.when(pl.program_id(1) == 0)
    def _():
        wga_b[...] = wga_ref[...].astype(BF16)
        wgs_b[...] = wgs_ref[...].astype(BF16)
        wa_b[...] = wa_ref[...].astype(BF16)
        ws_b[...] = ws_ref[...].astype(BF16)

    hb = hb_ref[...]
    ga = lax.dot_general(hb, wga_b[...], _NT, preferred_element_type=F32)
    gs = lax.dot_general(hb, wgs_b[...], _NT, preferred_element_type=F32)
    pa = jnp.dot(at_ref[...], wa_b[...], preferred_element_type=F32)
    ps = jnp.dot(y_ref[...], ws_b[...], preferred_element_type=F32)
    o_ref[...] = (_sigmoid(ga) * pa + _sigmoid(gs) * ps).astype(o_ref.dtype)


def _merge(hb, attn, y, w_gates, w_a, w_s):
    m = hb.shape[0]
    tm = _row_tile(m // BLOCK, ROW_TILE_BLOCKS)
    nt = D_MODEL // COL_TILE
    once = pl.Buffered(1)
    return pl.pallas_call(
        _merge_kernel,
        out_shape=jax.ShapeDtypeStruct((m, D_MODEL), BF16),
        grid=(nt, m // tm),
        in_specs=[
            pl.BlockSpec((tm, D_MODEL), lambda j, i: (i, 0)),
            pl.BlockSpec((tm, Q_DIM), lambda j, i: (i, 0)),
            pl.BlockSpec((tm, D_INNER), lambda j, i: (i, 0)),
            pl.BlockSpec((COL_TILE, D_MODEL), lambda j, i: (j, 0), pipeline_mode=once),
            pl.BlockSpec((COL_TILE, D_MODEL), lambda j, i: (j + nt, 0), pipeline_mode=once),
            pl.BlockSpec((Q_DIM, COL_TILE), lambda j, i: (0, j), pipeline_mode=once),
            pl.BlockSpec((D_INNER, COL_TILE), lambda j, i: (0, j), pipeline_mode=once),
        ],
        out_specs=pl.BlockSpec((tm, COL_TILE), lambda j, i: (i, j)),
        scratch_shapes=[
            pltpu.VMEM((COL_TILE, D_MODEL), BF16), pltpu.VMEM((COL_TILE, D_MODEL), BF16),
            pltpu.VMEM((Q_DIM, COL_TILE), BF16), pltpu.VMEM((D_INNER, COL_TILE), BF16),
        ],
        compiler_params=_cparams(("arbitrary", "arbitrary")),
        name="branch_merge",
    )(hb, attn, y, w_gates, w_gates, w_a, w_s)


def _outproj_router_kernel(mg_ref, h_ref, wo_ref, g_ref, b_ref, wr_ref, br_ref,
                           o_ref, ids_ref, wts_ref, cnt_ref, carry_ref, *, nb, tm):
    step = pl.program_id(0)

    @pl.when(step == 0)
    def _():
        carry_ref[...] = jnp.zeros_like(carry_ref)

    mix = jnp.dot(mg_ref[...], wo_ref[...], preferred_element_type=F32)
    h1 = _ln_rows(ALPHA * h_ref[...] + mix, g_ref[...], b_ref[...])
    o_ref[...] = h1

    h_hi = h1.astype(BF16)
    h_lo = (h1 - h_hi.astype(F32)).astype(BF16)
    hh = lax.dot_general(h_hi, wr_ref[...], _NT, preferred_element_type=F32)
    lh = lax.dot_general(h_lo, wr_ref[0:LANES, :], _NT, preferred_element_type=F32)
    logits = hh[:, 0:LANES] + (lh + hh[:, LANES:2 * LANES]) + br_ref[...]
    lt = logits.T
    lane128 = lax.broadcasted_iota(jnp.int32, (1, BLOCK), 1)
    blocks = tm // BLOCK
    real = jnp.concatenate(
        [(((step * blocks + s) % nb) > 0) | (lane128 >= PAD) for s in range(blocks)], axis=1)
    sub8 = lax.broadcasted_iota(jnp.int32, (EXPERTS_PER_GROUP, tm), 0)
    sub = lax.broadcasted_iota(jnp.int32, (LANES, tm), 0)

    gl = lt[0:MOE_GROUPS]
    ge = jnp.exp(gl - jnp.max(gl, axis=0, keepdims=True))
    pg = ge / jnp.sum(ge, axis=0, keepdims=True)
    p_top = jnp.max(pg, axis=0, keepdims=True)
    g_idx = jnp.min(jnp.where(pg == p_top, sub8, MOE_GROUPS), axis=0, keepdims=True)

    el = lt[MOE_GROUPS:MOE_GROUPS + EXPERTS_PER_GROUP]
    for g in range(1, MOE_GROUPS):
        r0 = MOE_GROUPS + g * EXPERTS_PER_GROUP
        el = jnp.where(g_idx == g, lt[r0:r0 + EXPERTS_PER_GROUP], el)
    ee = jnp.exp(el - jnp.max(el, axis=0, keepdims=True))
    pe = ee / jnp.sum(ee, axis=0, keepdims=True)
    v1 = jnp.max(pe, axis=0, keepdims=True)
    i1 = jnp.min(jnp.where(pe == v1, sub8, EXPERTS_PER_GROUP), axis=0, keepdims=True)
    pe2 = jnp.where(sub8 == i1, -1.0, pe)
    v2 = jnp.max(pe2, axis=0, keepdims=True)
    i2 = jnp.min(jnp.where(pe2 == v2, sub8, EXPERTS_PER_GROUP), axis=0, keepdims=True)
    vs = v1 + v2
    w1 = p_top * (v1 / vs)
    w2 = p_top * (v2 / vs)
    e1 = g_idx * EXPERTS_PER_GROUP + i1
    e2 = g_idx * EXPERTS_PER_GROUP + i2

    hit1 = (sub == e1 + MOE_GROUPS) & real
    hit2 = (sub == e2 + MOE_GROUPS) & real
    onehot = jnp.where(hit1 | hit2, 1.0, 0.0)
    rr = lax.broadcasted_iota(jnp.int32, (tm, tm), 0)
    cl = lax.broadcasted_iota(jnp.int32, (tm, tm), 1)
    before = jnp.where(rr > cl, 1.0, 0.0).astype(BF16)
    cum = jnp.dot(before, onehot.T.astype(BF16), preferred_element_type=F32).T + carry_ref[...]
    r1 = jnp.sum(jnp.where(hit1, cum, 0.0), axis=0, keepdims=True)
    r2 = jnp.sum(jnp.where(hit2, cum, 0.0), axis=0, keepdims=True)
    carry_ref[...] = carry_ref[...] + jnp.sum(onehot, axis=1, keepdims=True)
    cnt_ref[...] = carry_ref[...].astype(jnp.int32)

    ids_ref[...] = jnp.where(sub8 == 0, jnp.where(real, e1, -1), jnp.where(
        sub8 == 1, jnp.where(real, e2, -1), jnp.where(
            sub8 == 2, r1.astype(jnp.int32), jnp.where(sub8 == 3, r2.astype(jnp.int32), 0))))
    wts_ref[...] = jnp.where(sub == 0, w1, jnp.where(sub == 1, w2, 0.0)).T[:, 0:SUBLANES]


def _outproj_router(merged, h0, w_o, g, b, wr_hilo, b_r, nb):
    m = h0.shape[0]
    tm = _row_tile(m // BLOCK, ROW_TILE_BLOCKS)
    row_blk = lambda width: pl.BlockSpec((tm, width), lambda i: (i, 0))
    full = lambda shape: pl.BlockSpec(shape, lambda i: (0, 0))
    return pl.pallas_call(
        functools.partial(_outproj_router_kernel, nb=nb, tm=tm),
        out_shape=(jax.ShapeDtypeStruct((m, D_MODEL), F32),
                   jax.ShapeDtypeStruct((SUBLANES, m), jnp.int32), jax.ShapeDtypeStruct((m, SUBLANES), F32),
                   jax.ShapeDtypeStruct((LANES, 1), jnp.int32)),
        grid=(m // tm,),
        in_specs=[row_blk(D_MODEL), row_blk(D_MODEL), full((D_MODEL, D_MODEL)),
                  full((1, D_MODEL)), full((1, D_MODEL)),
                  full((2 * LANES, D_MODEL)), full((1, LANES))],
        out_specs=(row_blk(D_MODEL), pl.BlockSpec((SUBLANES, tm), lambda i: (0, i)), row_blk(SUBLANES),
                   full((LANES, 1))),
        scratch_shapes=[pltpu.VMEM((LANES, 1), F32)],
        compiler_params=_cparams(("arbitrary",)),
        name="out_proj_ln1_router",
    )(merged, h0, w_o, g, b, wr_hilo, b_r)


HALF = D_MODEL // 2
HI_MASK = 0xFFFF0000


def _pack_rows(x):
    lo = lax.bitcast_convert_type(x[:, :HALF].astype(BF16).astype(F32), jnp.uint32) >> 16
    hi = lax.bitcast_convert_type(x[:, HALF:].astype(BF16).astype(F32), jnp.uint32) & jnp.uint32(HI_MASK)
    return hi | lo


def _unpack_rows(w):
    lo = lax.bitcast_convert_type(w << 16, F32)
    hi = lax.bitcast_convert_type(w & jnp.uint32(HI_MASK), F32)
    return lo, hi


ROW_SUB = HALF // LANES


def _put_tile_rows(ref, lead, x):
    for s in range(ROW_SUB):
        ref[lead + (pl.ds(s, x.shape[0], stride=ROW_SUB), slice(None))] = x[:, s * LANES:(s + 1) * LANES]


def _get_tile_rows(ref, lead, r):
    return jnp.concatenate(
        [ref[lead + (pl.ds(s, r, stride=ROW_SUB), slice(None))] for s in range(ROW_SUB)], axis=1)


def _tile_rows(ref, row, n=1):
    return ref.at[pl.ds(pl.multiple_of(row * ROW_SUB, ROW_SUB), n * ROW_SUB), :]


def _dest_kernel(pstart_ref, ids_ref, dest_ref, *, n_slots):
    e = ids_ref[0:TOP_K, :]
    acc = jnp.zeros_like(e)
    for ex in range(N_EXPERTS):
        acc = jnp.where(e == ex, pstart_ref[ex], acc)
    k = lax.broadcasted_iota(jnp.int32, e.shape, 0)
    t = lax.broadcasted_iota(jnp.int32, e.shape, 1)
    dump = n_slots + k * PAD + (t & (BLOCK - 1))
    dest_ref[...] = jnp.where(e >= 0, acc + ids_ref[TOP_K:2 * TOP_K, :], dump)


def _dest_rows(pstart, ids, n_slots):
    m = ids.shape[1]
    return pl.pallas_call(
        functools.partial(_dest_kernel, n_slots=n_slots),
        out_shape=jax.ShapeDtypeStruct((TOP_K, m), jnp.int32),
        in_specs=[pl.BlockSpec(memory_space=pltpu.SMEM), pl.BlockSpec(memory_space=pltpu.VMEM)],
        out_specs=pl.BlockSpec(memory_space=pltpu.VMEM),
        name="moe_dest",
    )(pstart, ids)


def _dispatch_kernel(dest_ref, zlo_ref, zhi_ref, nused_ref, h_ref, xs_hbm, ring, zrow, zblk, sems, zsem,
                     *, n_blocks, n_steps):
    step = pl.program_id(0)
    n_rows = n_steps * BLOCK
    slot = step % RING_SLOTS
    items_per_step = -(-(N_EXPERTS + n_blocks) // n_steps)

    def zero_item(w, start):
        @pl.when(w < N_EXPERTS)
        def _():
            def per_row(p, c):
                cp = pltpu.make_async_copy(zrow, _tile_rows(xs_hbm, p), zsem)
                cp.start() if start else cp.wait()
                return c
            lax.fori_loop(zlo_ref[w], zhi_ref[w], per_row, 0)

        blk = nused_ref[0] + (w - N_EXPERTS)

        @pl.when((w >= N_EXPERTS) & (blk < n_blocks))
        def _():
            cp = pltpu.make_async_copy(zblk, _tile_rows(xs_hbm, blk * MOE_ROWS, MOE_ROWS), zsem)
            cp.start() if start else cp.wait()

    @pl.when(step == 0)
    def _():
        zrow[...] = jnp.zeros_like(zrow)
        zblk[...] = jnp.zeros_like(zblk)

    for u in range(items_per_step):
        zero_item(step * items_per_step + u, True)

    _put_tile_rows(ring, (slot,), _pack_rows(h_ref[...]))
    base = step * BLOCK

    def body(t, carry):
        for k in range(TOP_K):
            d = dest_ref[k * n_rows + base + t]
            pltpu.make_async_copy(_tile_rows(ring.at[slot], t), _tile_rows(xs_hbm, d),
                                  sems.at[slot]).start(priority=k)
        return carry
    lax.fori_loop(0, BLOCK, body, 0, unroll=DMA_UNROLL)

    def wait_slot(s):
        for k in range(TOP_K):
            pltpu.make_async_copy(ring.at[s], _tile_rows(xs_hbm, 0, BLOCK), sems.at[s]).wait()

    @pl.when(step > 1)
    def _():
        wait_slot((step + 1) % RING_SLOTS)

    @pl.when(step == n_steps - 1)
    def _():
        @pl.when(step > 0)
        def _():
            wait_slot((step + 2) % RING_SLOTS)
        wait_slot(slot)

        def wait_item(w, c):
            zero_item(w, False)
            return c
        lax.fori_loop(0, n_steps * items_per_step, wait_item, 0)


def _dispatch(h1, dest, zlo, zhi, n_used, n_blocks):
    m = h1.shape[0]
    n_slots = n_blocks * MOE_ROWS
    grid_spec = pltpu.PrefetchScalarGridSpec(
        num_scalar_prefetch=4,
        grid=(m // BLOCK,),
        in_specs=[pl.BlockSpec((BLOCK, D_MODEL), lambda i, *_: (i, 0))],
        out_specs=pl.BlockSpec(memory_space=pl.ANY),
        scratch_shapes=[
            pltpu.VMEM((RING_SLOTS, BLOCK * ROW_SUB, LANES), jnp.uint32),
            pltpu.VMEM((ROW_SUB, LANES), jnp.uint32),
            pltpu.VMEM((MOE_ROWS * ROW_SUB, LANES), jnp.uint32),
            pltpu.SemaphoreType.DMA((RING_SLOTS,)),
            pltpu.SemaphoreType.DMA(()),
        ],
    )
    return pl.pallas_call(
        functools.partial(_dispatch_kernel, n_blocks=n_blocks, n_steps=m // BLOCK),
        out_shape=jax.ShapeDtypeStruct(((n_slots + TOP_K * PAD) * ROW_SUB, LANES), jnp.uint32),
        grid_spec=grid_spec,
        compiler_params=_cparams(("arbitrary",)),
        name="moe_dispatch",
    )(dest, zlo, zhi, n_used, h1)


def _expert_kernel(blk_e_ref, nused_ref, xs_ref, wg_ref, wu_ref, wd_ref, o_ref):
    i = pl.program_id(0)
    n_used = nused_ref[0]

    @pl.when(i < n_used)
    def _():
        lo, hi = _unpack_rows(_get_tile_rows(xs_ref, (), MOE_ROWS))
        x_lo, x_hi = lo.astype(BF16), hi.astype(BF16)
        gate = (jnp.dot(x_lo, wg_ref[0:HALF, :].astype(BF16), preferred_element_type=F32)
                + jnp.dot(x_hi, wg_ref[HALF:D_MODEL, :].astype(BF16), preferred_element_type=F32))
        up = (jnp.dot(x_lo, wu_ref[0:HALF, :].astype(BF16), preferred_element_type=F32)
              + jnp.dot(x_hi, wu_ref[HALF:D_MODEL, :].astype(BF16), preferred_element_type=F32))
        hid = (_silu(gate) * up).astype(BF16)
        _put_tile_rows(o_ref, (), _pack_rows(jnp.dot(hid, wd_ref[...].astype(BF16),
                                                     preferred_element_type=F32)))


def _experts(xs, w_gate, w_up, w_down, blk_e, n_used, n_blocks):
    used = lambda i, nu: jnp.minimum(i, nu[0] - 1)
    grid_spec = pltpu.PrefetchScalarGridSpec(
        num_scalar_prefetch=2,
        grid=(n_blocks,),
        in_specs=[
            pl.BlockSpec((MOE_ROWS * ROW_SUB, LANES), lambda i, be, nu: (used(i, nu), 0)),
            pl.BlockSpec((None, None, D_MODEL, D_EXPERT), lambda i, be, nu: (0, be[i], 0, 0)),
            pl.BlockSpec((None, None, D_MODEL, D_EXPERT), lambda i, be, nu: (0, be[i], 0, 0)),
            pl.BlockSpec((None, None, D_EXPERT, D_MODEL), lambda i, be, nu: (0, be[i], 0, 0)),
        ],
        out_specs=pl.BlockSpec((MOE_ROWS * ROW_SUB, LANES), lambda i, be, nu: (used(i, nu), 0)),
    )
    return pl.pallas_call(
        _expert_kernel,
        out_shape=jax.ShapeDtypeStruct(xs.shape, xs.dtype),
        grid_spec=grid_spec,
        input_output_aliases={2: 0},
        compiler_params=_cparams(("arbitrary",)),
        name="moe_experts",
    )(blk_e, n_used, xs, w_gate, w_up, w_down)


def _combine_kernel(dest_ref, h_ref, wts_ref, yb_hbm, g_ref, b_ref, o_ref, ybuf, sems, *, nb):
    bb = pl.program_id(0)
    i = pl.program_id(1)
    n_i = pl.num_programs(1)
    step = bb * n_i + i
    n_steps = pl.num_programs(0) * n_i

    n_rows = pl.num_programs(0) * nb * BLOCK

    def assign_base(b_, i_):
        return (b_ * nb + i_ + 1) * BLOCK

    def start_gather(base, slot):
        def body(t, carry):
            for k in range(TOP_K):
                d = dest_ref[k * n_rows + base + t]
                pltpu.make_async_copy(_tile_rows(yb_hbm, d), _tile_rows(ybuf.at[slot, k], t),
                                      sems.at[slot]).start(priority=k)
            return carry
        lax.fori_loop(0, BLOCK, body, 0, unroll=DMA_UNROLL)

    def wait_gather(slot):
        for k in range(TOP_K):
            pltpu.make_async_copy(_tile_rows(yb_hbm, 0, BLOCK), ybuf.at[slot, k], sems.at[slot]).wait()

    @pl.when(step == 0)
    def _():
        start_gather(assign_base(0, 0), 0)

    @pl.when(step + 1 < n_steps)
    def _():
        nxt = jnp.where(i + 1 < n_i, assign_base(bb, i + 1), assign_base(bb + 1, 0))
        start_gather(nxt, (step + 1) % 2)

    slot = step % 2
    wait_gather(slot)
    wts = wts_ref[...]
    lo0, hi0 = _unpack_rows(_get_tile_rows(ybuf, (slot, 0), BLOCK))
    lo1, hi1 = _unpack_rows(_get_tile_rows(ybuf, (slot, 1), BLOCK))
    w0, w1 = wts[:, 0:1], wts[:, 1:2]
    r_lo = ALPHA * h_ref[:, 0:HALF] + (lo0 * w0 + lo1 * w1)
    r_hi = ALPHA * h_ref[:, HALF:D_MODEL] + (hi0 * w0 + hi1 * w1)
    mu = (jnp.sum(r_lo, axis=-1, keepdims=True) + jnp.sum(r_hi, axis=-1, keepdims=True)) / D_MODEL
    c_lo, c_hi = r_lo - mu, r_hi - mu
    var = (jnp.sum(c_lo * c_lo, axis=-1, keepdims=True)
           + jnp.sum(c_hi * c_hi, axis=-1, keepdims=True)) / D_MODEL
    inv = lax.rsqrt(var + LN_EPS)
    o_ref[:, 0:HALF] = c_lo * inv * g_ref[:, 0:HALF] + b_ref[:, 0:HALF]
    o_ref[:, HALF:D_MODEL] = c_hi * inv * g_ref[:, HALF:D_MODEL] + b_ref[:, HALF:D_MODEL]


def _combine_ln(h1, wts, yb, dest, g, b, bsz, nb):
    grid_spec = pltpu.PrefetchScalarGridSpec(
        num_scalar_prefetch=1,
        grid=(bsz, nb - 1),
        in_specs=[
            pl.BlockSpec((BLOCK, D_MODEL), lambda bb, i, d: (bb * nb + i + 1, 0)),
            pl.BlockSpec((BLOCK, SUBLANES), lambda bb, i, d: (bb * nb + i + 1, 0)),
            pl.BlockSpec(memory_space=pl.ANY),
            pl.BlockSpec((1, D_MODEL), lambda bb, i, d: (0, 0)),
            pl.BlockSpec((1, D_MODEL), lambda bb, i, d: (0, 0)),
        ],
        out_specs=pl.BlockSpec((None, BLOCK, D_MODEL), lambda bb, i, d: (bb, i, 0)),
        scratch_shapes=[
            pltpu.VMEM((2, TOP_K, BLOCK * ROW_SUB, LANES), jnp.uint32),
            pltpu.SemaphoreType.DMA((2,)),
        ],
    )
    return pl.pallas_call(
        functools.partial(_combine_kernel, nb=nb),
        out_shape=jax.ShapeDtypeStruct((bsz, (nb - 1) * BLOCK, D_MODEL), F32),
        grid_spec=grid_spec,
        compiler_params=_cparams(("arbitrary", "arbitrary")),
        name="moe_combine_ln2",
    )(dest, h1, wts, yb, g, b)


def _dispatch_tables(ids, counts, n_blocks):
    counts = counts[MOE_GROUPS:MOE_GROUPS + N_EXPERTS, 0]
    pcounts = (counts + MOE_ROWS - 1) // MOE_ROWS * MOE_ROWS
    pend = jnp.cumsum(pcounts)
    pstart = (pend - pcounts).astype(jnp.int32)
    n_used = jnp.maximum(pend[-1] // MOE_ROWS, 1).astype(jnp.int32)
    blk = jnp.arange(n_blocks, dtype=jnp.int32)
    blk_e = jnp.minimum(jnp.sum(pend[None, :] <= (blk * MOE_ROWS)[:, None], axis=1), N_EXPERTS - 1)
    blk_e = jnp.where(blk < n_used, blk_e, blk_e[n_used - 1]).astype(jnp.int32)
    dest = _dest_rows(pstart, ids, n_blocks * MOE_ROWS).reshape(-1)
    return dest, (pstart + counts).astype(jnp.int32), pend.astype(jnp.int32), blk_e, n_used.reshape(1)


def kernel(x, meta_tokens, ln_emb_g, ln_emb_b, w_in, conv_w, conv_b, dt_bias, a_log, d_skip, ssd_norm_g, sinks, w_br_attn, w_br_ssd, w_o, ln1_g, ln1_b, w_router_group, b_router_group, w_router_expert, b_router_expert, w_gate, w_up, w_down, ln2_g, ln2_b):
    bsz, seq, d = x.shape
    assert d == D_MODEL and seq % BLOCK == 0 and w_in.shape[0] == DEPTH
    nb = seq // BLOCK + 1
    row2 = lambda v: v.reshape(1, -1).astype(F32)

    meta_pad = jnp.concatenate([jnp.zeros((PAD, d), F32), meta_tokens.astype(F32)], axis=0)
    h0, h0b = _embed_ln(x, meta_pad, row2(ln_emb_g), row2(ln_emb_b))

    w_in_t = jnp.swapaxes(w_in, 1, 2)
    main = _inproj(h0b, w_in_t)
    w_dt = jnp.pad(w_in_t[0, DT_OFF:DT_OFF + SSD_HEADS], ((0, LANES - SSD_HEADS), (0, 0))).astype(BF16)
    pad_h = lambda v: jnp.pad(v.astype(F32), (0, LANES - SSD_HEADS)).reshape(1, LANES)
    attn = _attention(main, sinks[0].astype(F32), bsz, nb)
    y = _ssd(h0b, main, w_dt, conv_w[0].astype(F32), row2(conv_b[0]), pad_h(dt_bias[0]),
             pad_h(-jnp.exp(a_log[0].astype(F32))), row2(jnp.repeat(d_skip[0], SSD_HEADDIM)),
             row2(ssd_norm_g[0]), bsz, nb)
    merged = _merge(h0b, attn, y, w_in_t[0, GATE_OFF:], w_br_attn[0], w_br_ssd[0])
    w_r = jnp.pad(jnp.concatenate([w_router_group[0].T, w_router_expert[0].T], axis=0).astype(F32),
                  ((0, LANES - MOE_GROUPS - N_EXPERTS), (0, 0)))
    wr_hi = w_r.astype(BF16)
    wr_lo = (w_r - wr_hi.astype(F32)).astype(BF16)
    b_r = jnp.pad(jnp.concatenate([b_router_group[0], b_router_expert[0]]).astype(F32),
                  (0, LANES - MOE_GROUPS - N_EXPERTS)).reshape(1, LANES)
    h1, ids, wts, counts = _outproj_router(merged, h0, w_o[0].astype(BF16), row2(ln1_g[0]),
                                           row2(ln1_b[0]), jnp.concatenate([wr_hi, wr_lo], axis=0), b_r, nb)

    n_assign = bsz * (seq + N_META) * TOP_K
    n_blocks = -(-n_assign // MOE_ROWS) + N_EXPERTS
    dest, zlo, zhi, blk_e, n_used = _dispatch_tables(ids, counts, n_blocks)
    xs = _dispatch(h1, dest, zlo, zhi, n_used, n_blocks)
    yb = _experts(xs, w_gate, w_up, w_down, blk_e, n_used, n_blocks)
    return _combine_ln(h1, wts, yb, dest, row2(ln2_g[0]), row2(ln2_b[0]), bsz, nb)
```
